```python
import math
import jax, jax.numpy as jnp
from jax import lax
import numpy as np

D_MODEL = 1024
BATCH = 8
SEQ = 2048
DEPTH = 2

N_MIXERS = 2
N_HEADS = 16
A_LATENT = 128
A_VHEAD = 64
IDX_HEADS = 8
IDX_DIM = 64
TOPK_MAX = 256
Q_BLOCK = 128
B_KV_HEADS = 2
B_HEAD_DIM = 64
WINDOW = 128
D_FF = ((8 * D_MODEL + 3 * 256 - 1) // (3 * 256)) * 256
REL_BUCKETS = 32
REL_MAX_DIST = 128
EPS = 1e-6

A_SIZES = [N_HEADS * A_LATENT, A_LATENT, IDX_HEADS * IDX_DIM, IDX_DIM, IDX_HEADS]
A_IN = sum(A_SIZES)
B_SIZES = [N_HEADS * B_HEAD_DIM, B_KV_HEADS * B_HEAD_DIM, B_KV_HEADS * B_HEAD_DIM]
B_IN = sum(B_SIZES)
N_A_LAYERS = (DEPTH + 1) // 2
N_B_LAYERS = DEPTH // 2

kernel_name = "hybrid_dsa_swa_sink_t5bias_swiglu"


def _split(proj, sizes):
    offs, o = [], 0
    for s in sizes[:-1]:
        o += s
        offs.append(o)
    return jnp.split(proj, offs, axis=-1)


def rms_norm(x, g):
    xf = x.astype(jnp.float32)
    y = xf * lax.rsqrt(jnp.mean(xf * xf, axis=-1, keepdims=True) + EPS)
    return (y * g.astype(jnp.float32)).astype(x.dtype)


def rel_bucket(rel):
    n = jnp.maximum(rel, 0)
    max_exact = REL_BUCKETS // 2
    nf = jnp.maximum(n, max_exact).astype(jnp.float32)
    large = max_exact + (jnp.log(nf / max_exact) / math.log(REL_MAX_DIST / max_exact)
                         * (REL_BUCKETS - max_exact)).astype(jnp.int32)
    large = jnp.minimum(large, REL_BUCKETS - 1)
    return jnp.where(n < max_exact, n, large)


def dsa_mixer(h, w_in, latent_g, w_uv, w_out, rel_bias):
    B, T, _ = h.shape
    H, R = N_HEADS, A_LATENT
    k_sel = min(TOPK_MAX, T // 4)
    nblk = T // Q_BLOCK
    q_lat, c_kv, q_idx, k_idx, w_idx = _split(h @ w_in, A_SIZES)
    q_lat = q_lat.reshape(B, T, H, R)
    q_idx = q_idx.reshape(B, T, IDX_HEADS, IDX_DIM)
    c_kv = rms_norm(c_kv, latent_g)
    w_idx = w_idx * (IDX_HEADS ** -0.5)

    def to_blocks(a):
        return jnp.moveaxis(a.reshape((B, nblk, Q_BLOCK) + a.shape[2:]), 1, 0)

    s_pos = jnp.arange(T)

    def block(args):
        i, qlb, qib, wib = args
        t_pos = i * Q_BLOCK + jnp.arange(Q_BLOCK)
        dots = jnp.einsum('bqjd,bsd->bqjs', qib, k_idx) * (IDX_DIM ** -0.5)
        score = jnp.einsum('bqj,bqjs->bqs', wib, jax.nn.relu(dots)).astype(jnp.float32)
        causal = s_pos[None, :] <= t_pos[:, None]
        score = jnp.where(causal[None], score, -jnp.inf)
        _, idx = lax.top_k(score, k_sel)
        valid = idx <= t_pos[None, :, None]
        kv = jax.vmap(lambda c, ix: c[ix])(c_kv, idx)
        logits = jnp.einsum('bqhr,bqkr->bhqk', qlb, kv).astype(jnp.float32) * (R ** -0.5)
        bias = rel_bias[rel_bucket(t_pos[None, :, None] - idx)]
        logits = logits + jnp.transpose(bias, (0, 3, 1, 2)).astype(jnp.float32)
        logits = jnp.where(valid[:, None], logits, -jnp.inf)
        p = jax.nn.softmax(logits, axis=-1).astype(kv.dtype)
        return jnp.einsum('bhqk,bqkr->bqhr', p, kv)

    o = lax.map(block, (jnp.arange(nblk), to_blocks(q_lat), to_blocks(q_idx), to_blocks(w_idx)))
    o = jnp.moveaxis(o, 0, 1).reshape(B, T, H, R)
    o = jnp.einsum('bthr,hrd->bthd', o, w_uv).reshape(B, T, H * A_VHEAD)
    return o @ w_out


def swa_mixer(h, w_in, b_in, sinks, w_out, b_out, rel_bias):
    B, T, _ = h.shape
    W, G, KV, Dh = WINDOW, N_HEADS // B_KV_HEADS, B_KV_HEADS, B_HEAD_DIM
    nb = T // W
    q, k, v = _split(h @ w_in + b_in, B_SIZES)
    q = q.reshape(B, nb, W, KV, G, Dh)
    k = k.reshape(B, nb, W, KV, Dh)
    v = v.reshape(B, nb, W, KV, Dh)

    def with_prev(a):
        prev = jnp.pad(a, ((0, 0), (1, 0), (0, 0), (0, 0), (0, 0)))[:, :-1]
        return jnp.concatenate([prev, a], axis=2)

    kk, vv = with_prev(k), with_prev(v)
    logits = jnp.einsum('bnqkgd,bnskd->bnkgqs', q, kk).astype(jnp.float32) * (Dh ** -0.5)
    rel = (W + jnp.arange(W))[:, None] - jnp.arange(2 * W)[None, :]
    in_win = (rel >= 0) & (rel < W)
    key_exists = (jnp.arange(nb)[:, None] * W + jnp.arange(2 * W)[None, :] - W) >= 0
    mask = in_win[None] & key_exists[:, None, :]
    bias = rel_bias[rel_bucket(rel)].reshape(W, 2 * W, KV, G)
    logits = logits + jnp.transpose(bias, (2, 3, 0, 1)).astype(jnp.float32)
    logits = jnp.where(mask[None, :, None, None], logits, -jnp.inf)
    sink = jnp.broadcast_to(sinks.reshape(KV, G)[None, None, :, :, None, None].astype(jnp.float32),
                            logits.shape[:-1] + (1,))
    p = jax.nn.softmax(jnp.concatenate([logits, sink], axis=-1), axis=-1)[..., :-1]
    o = jnp.einsum('bnkgqs,bnskd->bnqkgd', p.astype(vv.dtype), vv)
    return o.reshape(B, T, N_HEADS * Dh) @ w_out + b_out


def swiglu(h, w_gate_up, w_down):
    g, u = jnp.split(h @ w_gate_up, 2, axis=-1)
    return (jax.nn.silu(g) * u) @ w_down


def setup_inputs(seed: int = 0) -> dict:
    key = jax.random.key(seed)
    ks = jax.random.split(key, 20)
    f = jnp.float32

    def w(k, shape, fan_in):
        return jax.random.normal(k, shape, f) * (fan_in ** -0.5)

    def gain(k, shape):
        return 1.0 + 0.02 * jax.random.normal(k, shape, f)

    return {
        "x": jax.random.normal(ks[0], (BATCH, SEQ, D_MODEL), f),
        "rel_bias": 0.5 * jax.random.normal(ks[1], (REL_BUCKETS, N_HEADS), f),
        "attn_norm": gain(ks[2], (DEPTH, D_MODEL)),
        "ffn_norm": gain(ks[3], (DEPTH, D_MODEL)),
        "final_norm": gain(ks[4], (D_MODEL,)),
        "a_w_in": w(ks[5], (N_A_LAYERS, D_MODEL, A_IN), D_MODEL),
        "a_latent_norm": gain(ks[6], (N_A_LAYERS, A_LATENT)),
        "a_w_uv": w(ks[7], (N_A_LAYERS, N_HEADS, A_LATENT, A_VHEAD), A_LATENT),
        "a_w_out": w(ks[8], (N_A_LAYERS, N_HEADS * A_VHEAD, D_MODEL), N_HEADS * A_VHEAD),
        "b_w_in": w(ks[9], (N_B_LAYERS, D_MODEL, B_IN), D_MODEL),
        "b_b_in": 0.02 * jax.random.normal(ks[10], (N_B_LAYERS, B_IN), f),
        "b_sinks": 0.5 * jax.random.normal(ks[11], (N_B_LAYERS, N_HEADS), f),
        "b_w_out": w(ks[12], (N_B_LAYERS, N_HEADS * B_HEAD_DIM, D_MODEL), N_HEADS * B_HEAD_DIM),
        "b_b_out": 0.02 * jax.random.normal(ks[13], (N_B_LAYERS, D_MODEL), f),
        "ffn_w_gate_up": w(ks[14], (DEPTH, D_MODEL, 2 * D_FF), D_MODEL),
        "ffn_w_down": w(ks[15], (DEPTH, D_FF, D_MODEL), D_FF),
    }


def reference(x, rel_bias, attn_norm, ffn_norm, final_norm,
              a_w_in, a_latent_norm, a_w_uv, a_w_out,
              b_w_in, b_b_in, b_sinks, b_w_out, b_b_out,
              ffn_w_gate_up, ffn_w_down):
    for layer in range(DEPTH):
        j = layer // N_MIXERS
        h = rms_norm(x, attn_norm[layer])
        if layer % N_MIXERS == 0:
            x = x + dsa_mixer(h, a_w_in[j], a_latent_norm[j], a_w_uv[j], a_w_out[j], rel_bias)
        else:
            x = x + swa_mixer(h, b_w_in[j], b_b_in[j], b_sinks[j], b_w_out[j], b_b_out[j], rel_bias)
        h = rms_norm(x, ffn_norm[layer])
        x = x + swiglu(h, ffn_w_gate_up[layer], ffn_w_down[layer])
    return rms_norm(x, final_norm)
```

```python
import functools
import math

import numpy as np
import jax
import jax.numpy as jnp
from jax import lax
from jax.experimental import pallas as pl
from jax.experimental.pallas import tpu as pltpu

F32 = jnp.float32
BF16 = jnp.bfloat16
I32 = jnp.int32

EPS = 1e-6
N_HEADS = 16
A_LATENT = 128
A_VHEAD = 64
IDX_HEADS = 8
IDX_DIM = 64
TOPK_MAX = 256
B_KV_HEADS = 2
B_HEAD_DIM = 64
REL_BUCKETS = 32
REL_MAX_DIST = 128

BLK = 128
INT_MIN = -(2 ** 31)
NEG_INF = float("-inf")
VMEM_LIMIT = 56 * 1024 * 1024


def _rms(xf, g):
    ms = jnp.mean(xf * xf, axis=-1, keepdims=True)
    return xf * lax.rsqrt(ms + EPS) * g


def _dot(a, b):
    return jnp.dot(a, b, preferred_element_type=F32)


def _dot_nt(a, b):
    return lax.dot_general(a, b, (((1,), (1,)), ((), ())), preferred_element_type=F32)


def _params(n_axes):
    return pltpu.CompilerParams(
        dimension_semantics=("arbitrary",) * n_axes, vmem_limit_bytes=VMEM_LIMIT)


def _resident(shape):
    zeros = (0,) * len(shape)
    return pl.BlockSpec(shape, lambda *_: zeros, pipeline_mode=pl.Buffered(1))


def _bucket_table():
    r = np.arange(BLK)[:, None]
    c = np.arange(BLK)[None, :]
    rel = np.stack([r - c, BLK + r - c])
    n = np.maximum(rel, 0)
    max_exact = REL_BUCKETS // 2
    nf = np.maximum(n, max_exact).astype(np.float64)
    large = max_exact + (np.log(nf / max_exact) / math.log(REL_MAX_DIST / max_exact)
                         * (REL_BUCKETS - max_exact)).astype(np.int32)
    large = np.minimum(large, REL_BUCKETS - 1)
    return np.where(n < max_exact, n, large).astype(np.int32)


def _bias_kernel(rb_ref, bidx_ref, bt_ref, btm_ref):
    for kind in range(2):
        bidx = bidx_ref[kind]
        for h in range(N_HEADS):
            def body(b, t, h=h, bidx=bidx):
                return jnp.where(bidx == b, rb_ref[b, h], t)
            t = lax.fori_loop(0, REL_BUCKETS, body, jnp.zeros((BLK, BLK), F32))
            bt_ref[kind, h] = t
            btm_ref[kind, h] = t - rb_ref[REL_BUCKETS - 1, h]


def _bias_tiles(rel_bias):
    shp = jax.ShapeDtypeStruct((2, N_HEADS, BLK, BLK), F32)
    return pl.pallas_call(
        _bias_kernel,
        out_shape=(shp, shp),
        in_specs=[pl.BlockSpec(memory_space=pltpu.SMEM),
                  pl.BlockSpec(memory_space=pltpu.VMEM)],
        out_specs=(pl.BlockSpec(memory_space=pltpu.VMEM),
                   pl.BlockSpec(memory_space=pltpu.VMEM)),
        name="bias_tiles",
    )(rel_bias, jnp.asarray(_bucket_table()))


A_QLAT = N_HEADS * A_LATENT
A_QIDX = IDX_HEADS * IDX_DIM
A_IN_PAD = 2816


def _a_proj_kernel(x_ref, g_ref, w_ref, lg_ref,
                   qlat_ref, ckv_ref, qidx_ref, kidx_ref, widx_ref, h_ref):
    h_ref[...] = _rms(x_ref[...], g_ref[...]).astype(BF16)
    for c in range(A_QLAT // 512):
        sl = slice(c * 512, (c + 1) * 512)
        qlat_ref[:, sl] = _dot(h_ref[...], w_ref[:, sl]).astype(BF16)
    t = _dot(h_ref[...], w_ref[:, 2048:2560])
    ckv_ref[...] = _rms(t[:, :A_LATENT], lg_ref[...]).astype(BF16)
    qidx_ref[:, :384] = t[:, A_LATENT:].astype(BF16)
    t = _dot(h_ref[...], w_ref[:, 2560:2816])
    qidx_ref[:, 384:] = t[:, :128].astype(BF16)
    kidx_ref[...] = t[:, 128:128 + IDX_DIM].astype(BF16)
    widx_ref[...] = t[:, 192:192 + IDX_HEADS] * (IDX_HEADS ** -0.5)


def _a_proj(x2, g, w, lg, tm=512):
    n, d = x2.shape
    row = lambda i: (i, 0)
    return pl.pallas_call(
        _a_proj_kernel,
        grid=(n // tm,),
        in_specs=[pl.BlockSpec((tm, d), row),
                  _resident((1, d)),
                  _resident((d, A_IN_PAD)),
                  _resident((1, A_LATENT))],
        out_specs=(pl.BlockSpec((tm, A_QLAT), row),
                   pl.BlockSpec((tm, A_LATENT), row),
                   pl.BlockSpec((tm, A_QIDX), row),
                   pl.BlockSpec((tm, IDX_DIM), row),
                   pl.BlockSpec((tm, IDX_HEADS), row)),
        out_shape=(jax.ShapeDtypeStruct((n, A_QLAT), BF16),
                   jax.ShapeDtypeStruct((n, A_LATENT), BF16),
                   jax.ShapeDtypeStruct((n, A_QIDX), BF16),
                   jax.ShapeDtypeStruct((n, IDX_DIM), BF16),
                   jax.ShapeDtypeStruct((n, IDX_HEADS), F32)),
        scratch_shapes=[pltpu.VMEM((tm, d), BF16)],
        compiler_params=_params(1),
        name="a_proj",
    )(x2, g, w, lg)


def _dsa_kernel(qlat_ref, qidx_ref, widx_ref, kidx_ref, ckv_ref, wuv_ref, btm_ref,
                out_ref,
                qs_ref, qi_ref, wb_ref, key_ref, madd_ref, l_ref, m_ref, s_ref,
                acc_ref, pb_ref, *, k_sel):
    i = pl.program_id(1)
    row = lax.broadcasted_iota(I32, (BLK, BLK), 0)
    col = lax.broadcasted_iota(I32, (BLK, BLK), 1)
    causal = col <= row
    scale = A_LATENT ** -0.5

    def chunk(ref, c):
        return ref[pl.ds(pl.multiple_of(c * BLK, BLK), BLK), :]

    for h in range(N_HEADS):
        qs_ref[h * BLK:(h + 1) * BLK, :] = qlat_ref[:, h * A_LATENT:(h + 1) * A_LATENT]
    for j in range(IDX_HEADS):
        qi_ref[j * BLK:(j + 1) * BLK, :] = qidx_ref[:, j * IDX_DIM:(j + 1) * IDX_DIM]
    w2 = widx_ref[...] * (IDX_DIM ** -0.5)
    for j in range(IDX_HEADS):
        wb_ref[j] = jnp.broadcast_to(w2[:, j:j + 1], (BLK, BLK))

    def index_chunk(c, diag):
        d = _dot_nt(qi_ref[...], chunk(kidx_ref, c))
        s = wb_ref[0] * jnp.maximum(d[0:BLK], 0.0)
        for j in range(1, IDX_HEADS):
            s = s + wb_ref[j] * jnp.maximum(d[j * BLK:(j + 1) * BLK], 0.0)
        bits = lax.bitcast_convert_type(s, I32)
        key = bits ^ ((bits >> 31) & 0x7FFFFFFF)
        key = jnp.where(key == -1, 0, key)
        if diag:
            key = jnp.where(causal, key, INT_MIN)
        key_ref[c] = key

    def index_body(c, carry):
        index_chunk(c, False)
        return carry
    lax.fori_loop(0, i, index_body, 0)
    index_chunk(i, True)

    @pl.when(i * BLK < k_sel)
    def _():
        def body(c, carry):
            madd_ref[c] = jnp.zeros((BLK, BLK), F32)
            return carry
        lax.fori_loop(0, i, body, 0)
        madd_ref[i] = jnp.where(causal, 0.0, NEG_INF)

    @pl.when(i * BLK >= k_sel)
    def _():
        def count(pred):
            def body(c, cnt):
                return cnt + jnp.where(pred(key_ref[c]), 1.0, 0.0)
            cnt = lax.fori_loop(0, i + 1, body, jnp.zeros((BLK, BLK), F32))
            return jnp.sum(cnt, axis=-1, keepdims=True)

        def count_ge(cand):
            cb = jnp.broadcast_to(cand, (BLK, BLK))
            return count(lambda key: key >= cb)

        kf = float(k_sel)
        ans = jnp.where(count_ge(jnp.zeros((BLK, 1), I32)) >= kf, 0, INT_MIN)

        def bit_body(it, ans):
            cand = ans | jnp.left_shift(1, 30 - it)
            return jnp.where(count_ge(cand) >= kf, cand, ans)
        thr = lax.fori_loop(0, 31, bit_body, ans)
        thr_b = jnp.broadcast_to(thr, (BLK, BLK))
        n_gt = count(lambda key: key > thr_b)
        need_b = jnp.broadcast_to(kf - n_gt, (BLK, BLK))
        upper = jnp.where(row < col, 1.0, 0.0).astype(BF16)

        def mask_body(c, off):
            key = key_ref[c]
            eq = key == thr_b
            eqf = jnp.where(eq, 1.0, 0.0)
            before = _dot(eqf.astype(BF16), upper) + off
            sel = (key > thr_b) | (eq & (before < need_b))
            madd_ref[c] = jnp.where(sel, 0.0, NEG_INF)
            return off + jnp.sum(eqf, axis=-1, keepdims=True)
        lax.fori_loop(0, i + 1, mask_body, jnp.zeros((BLK, 1), F32))

    m_ref[...] = jnp.full(m_ref.shape, NEG_INF, F32)

    def logits_chunk(c, near):
        lg = _dot_nt(qs_ref[...], chunk(ckv_ref, c))
        madd = madd_ref[c]
        for h in range(N_HEADS):
            sl = slice(h * BLK, (h + 1) * BLK)
            v = lg[sl] * scale + madd
            if near is not None:
                v = v + btm_ref[near, h]
            l_ref[c, sl, :] = v
            m_ref[sl, :] = jnp.maximum(m_ref[sl, :], v)

    def far_body(c, carry):
        logits_chunk(c, None)
        return carry
    lax.fori_loop(0, i - 1, far_body, 0)

    @pl.when(i >= 1)
    def _():
        logits_chunk(i - 1, 1)
    logits_chunk(i, 0)

    for h in range(N_HEADS):
        sl = slice(h * BLK, (h + 1) * BLK)
        m_ref[sl, :] = jnp.broadcast_to(
            jnp.max(m_ref[sl, :], axis=-1, keepdims=True), (BLK, BLK))

    s_ref[...] = jnp.zeros(s_ref.shape, F32)
    acc_ref[...] = jnp.zeros(acc_ref.shape, F32)

    def pv_body(c, carry):
        for h in range(N_HEADS):
            sl = slice(h * BLK, (h + 1) * BLK)
            p = jnp.exp(l_ref[c, sl, :] - m_ref[sl, :])
            s_ref[sl, :] = s_ref[sl, :] + p
            pb_ref[sl, :] = p.astype(BF16)
        acc_ref[...] = acc_ref[...] + _dot(pb_ref[...], chunk(ckv_ref, c))
        return carry
    lax.fori_loop(0, i + 1, pv_body, 0)

    for hp in range(N_HEADS // 2):
        parts = []
        for h in (2 * hp, 2 * hp + 1):
            sl = slice(h * BLK, (h + 1) * BLK)
            den = jnp.sum(s_ref[sl, :], axis=-1, keepdims=True)
            o = (acc_ref[sl, :] / den).astype(BF16)
            parts.append(_dot(o, wuv_ref[h]))
        out_ref[:, hp * 2 * A_VHEAD:(hp + 1) * 2 * A_VHEAD] = (
            jnp.concatenate(parts, axis=1).astype(BF16))


def _dsa_attn(qlat, qidx, widx, kidx, ckv, wuv, btm, batch, seq):
    nblk = seq // BLK
    k_sel = min(TOPK_MAX, seq // 4)
    qrow = lambda b, i: (b * nblk + i, 0)
    brow = lambda b, i: (b, 0)
    hrows = N_HEADS * BLK
    return pl.pallas_call(
        functools.partial(_dsa_kernel, k_sel=k_sel),
        grid=(batch, nblk),
        in_specs=[pl.BlockSpec((BLK, A_QLAT), qrow),
                  pl.BlockSpec((BLK, A_QIDX), qrow),
                  pl.BlockSpec((BLK, IDX_HEADS), qrow),
                  pl.BlockSpec((seq, IDX_DIM), brow),
                  pl.BlockSpec((seq, A_LATENT), brow),
                  _resident((N_HEADS, A_LATENT, A_VHEAD)),
                  _resident((2, N_HEADS, BLK, BLK))],
        out_specs=pl.BlockSpec((BLK, N_HEADS * A_VHEAD), qrow),
        out_shape=jax.ShapeDtypeStruct((batch * seq, N_HEADS * A_VHEAD), BF16),
        scratch_shapes=[
            pltpu.VMEM((hrows, A_LATENT), BF16),
            pltpu.VMEM((IDX_HEADS * BLK, IDX_DIM), BF16),
            pltpu.VMEM((IDX_HEADS, BLK, BLK), F32),
            pltpu.VMEM((nblk, BLK, BLK), I32),
            pltpu.VMEM((nblk, BLK, BLK), F32),
            pltpu.VMEM((nblk, hrows, BLK), F32),
            pltpu.VMEM((hrows, BLK), F32),
            pltpu.VMEM((hrows, BLK), F32),
            pltpu.VMEM((hrows, A_LATENT), F32),
            pltpu.VMEM((hrows, BLK), BF16),
        ],
        compiler_params=_params(2),
        name="dsa_attn",
    )(qlat, qidx, widx, kidx, ckv, wuv, btm)


B_Q = N_HEADS * B_HEAD_DIM
B_KV = B_KV_HEADS * B_HEAD_DIM


def _b_proj_kernel(x_ref, g_ref, w_ref, b_ref, q_ref, k_ref, v_ref, h_ref):
    h_ref[...] = _rms(x_ref[...], g_ref[...]).astype(BF16)
    for c in range(B_Q // 512):
        sl = slice(c * 512, (c + 1) * 512)
        q_ref[:, sl] = (_dot(h_ref[...], w_ref[:, sl]) + b_ref[:, sl]).astype(BF16)
    t = _dot(h_ref[...], w_ref[:, B_Q:]) + b_ref[:, B_Q:]
    k_ref[...] = t[:, :B_KV].astype(BF16)
    v_ref[...] = t[:, B_KV:].astype(BF16)


def _b_proj(x2, g, w, b, tm=512):
    n, d = x2.shape
    row = lambda i: (i, 0)
    return pl.pallas_call(
        _b_proj_kernel,
        grid=(n // tm,),
        in_specs=[pl.BlockSpec((tm, d), row),
                  _resident((1, d)),
                  _resident((d, B_Q + 2 * B_KV)),
                  _resident((1, B_Q + 2 * B_KV))],
        out_specs=(pl.BlockSpec((tm, B_Q), row),
                   pl.BlockSpec((tm, B_KV), row),
                   pl.BlockSpec((tm, B_KV), row)),
        out_shape=(jax.ShapeDtypeStruct((n, B_Q), BF16),
                   jax.ShapeDtypeStruct((n, B_KV), BF16),
                   jax.ShapeDtypeStruct((n, B_KV), BF16)),
        scratch_shapes=[pltpu.VMEM((tm, d), BF16)],
        compiler_params=_params(1),
        name="b_proj",
    )(x2, g, w, b)


def _swa_kernel(sink_ref, q_ref, kp_ref, kc_ref, vp_ref, vc_ref, bt_ref, out_ref):
    n = pl.program_id(1)
    row = lax.broadcasted_iota(I32, (BLK, BLK), 0)
    col = lax.broadcasted_iota(I32, (BLK, BLK), 1)
    madd_cur = jnp.where(col <= row, 0.0, NEG_INF)
    madd_prev = jnp.where((col > row) & (n >= 1), 0.0, NEG_INF)
    scale = B_HEAD_DIM ** -0.5
    group = N_HEADS // B_KV_HEADS
    for hp in range(N_HEADS // 2):
        parts = []
        for h in (2 * hp, 2 * hp + 1):
            kv = h // group
            ksl = slice(kv * B_HEAD_DIM, (kv + 1) * B_HEAD_DIM)
            qh = q_ref[:, h * B_HEAD_DIM:(h + 1) * B_HEAD_DIM]
            lp = _dot_nt(qh, kp_ref[:, ksl]) * scale + bt_ref[1, h] + madd_prev
            lc = _dot_nt(qh, kc_ref[:, ksl]) * scale + bt_ref[0, h] + madd_cur
            sink = sink_ref[h]
            m = jnp.maximum(jnp.max(jnp.maximum(lp, lc), axis=-1, keepdims=True), sink)
            pp = jnp.exp(lp - m)
            pc = jnp.exp(lc - m)
            den = jnp.sum(pp + pc, axis=-1, keepdims=True) + jnp.exp(sink - m)
            o = _dot(pp.astype(BF16), vp_ref[:, ksl]) + _dot(pc.astype(BF16), vc_ref[:, ksl])
            parts.append(o / den)
        out_ref[:, hp * 2 * B_HEAD_DIM:(hp + 1) * 2 * B_HEAD_DIM] = (
            jnp.concatenate(parts, axis=1).astype(BF16))


def _swa_attn(q, k, v, sinks, bt, batch, seq):
    nb = seq // BLK
    cur = lambda b, n: (b * nb + n, 0)
    prev = lambda b, n: (b * nb + jnp.maximum(n - 1, 0), 0)
    return pl.pallas_call(
        _swa_kernel,
        grid=(batch, nb),
        in_specs=[pl.BlockSpec(memory_space=pltpu.SMEM),
                  pl.BlockSpec((BLK, B_Q), cur),
                  pl.BlockSpec((BLK, B_KV), prev),
                  pl.BlockSpec((BLK, B_KV), cur),
                  pl.BlockSpec((BLK, B_KV), prev),
                  pl.BlockSpec((BLK, B_KV), cur),
                  _resident((2, N_HEADS, BLK, BLK))],
        out_specs=pl.BlockSpec((BLK, B_Q), cur),
        out_shape=jax.ShapeDtypeStruct((batch * seq, B_Q), BF16),
        compiler_params=_params(2),
        name="swa_attn",
    )(sinks, q, k, k, v, v, bt)


FF_CHUNK = 256


def _ffn_kernel(x_ref, o_ref, wo_ref, bo_ref, g_ref, wgu_ref, wd_ref, fg_ref,
                out_ref, h_ref, act_ref, *, d_ff, final):
    x1 = x_ref[...] + _dot(o_ref[...], wo_ref[...]) + bo_ref[...]
    out_ref[...] = x1
    h_ref[...] = _rms(x1, g_ref[...]).astype(BF16)
    for c in range(d_ff // FF_CHUNK):
        gate = _dot(h_ref[...], wgu_ref[:, c * FF_CHUNK:(c + 1) * FF_CHUNK])
        up = _dot(h_ref[...], wgu_ref[:, d_ff + c * FF_CHUNK:d_ff + (c + 1) * FF_CHUNK])
        act_ref[:, c * FF_CHUNK:(c + 1) * FF_CHUNK] = (
            gate * jax.nn.sigmoid(gate) * up).astype(BF16)
    y = out_ref[...] + _dot(act_ref[...], wd_ref[...])
    if final:
        y = _rms(y, fg_ref[...])
    out_ref[...] = y


def _ffn(x2, o, wo, bo, g, wgu, wd, fg, final, tm=512):
    n, d = x2.shape
    d_ff = wd.shape[0]
    row = lambda i: (i, 0)
    return pl.pallas_call(
        functools.partial(_ffn_kernel, d_ff=d_ff, final=final),
        grid=(n // tm,),
        in_specs=[pl.BlockSpec((tm, d), row),
                  pl.BlockSpec((tm, o.shape[1]), row),
                  _resident(wo.shape),
                  _resident((1, d)),
                  _resident((1, d)),
                  _resident(wgu.shape),
                  _resident(wd.shape),
                  _resident((1, d))],
        out_specs=pl.BlockSpec((tm, d), row),
        out_shape=jax.ShapeDtypeStruct((n, d), F32),
        scratch_shapes=[pltpu.VMEM((tm, d), BF16),
                        pltpu.VMEM((tm, d_ff), BF16)],
        compiler_params=_params(1),
        name="ffn",
    )(x2, o, wo, bo, g, wgu, wd, fg)


def kernel(x, rel_bias, attn_norm, ffn_norm, final_norm, a_w_in, a_latent_norm, a_w_uv, a_w_out, b_w_in, b_b_in, b_sinks, b_w_out, b_b_out, ffn_w_gate_up, ffn_w_down):
    batch, seq, d = x.shape
    x2 = x.reshape(batch * seq, d)
    bt, btm = _bias_tiles(rel_bias)
    fg = final_norm.reshape(1, d)

    wa = jnp.pad(a_w_in[0], ((0, 0), (0, A_IN_PAD - a_w_in.shape[2]))).astype(BF16)
    qlat, ckv, qidx, kidx, widx = _a_proj(
        x2, attn_norm[0].reshape(1, d), wa, a_latent_norm[0].reshape(1, A_LATENT))
    o = _dsa_attn(qlat, qidx, widx, kidx, ckv, a_w_uv[0].astype(BF16), btm, batch, seq)
    x2 = _ffn(x2, o, a_w_out[0].astype(BF16), jnp.zeros((1, d), F32),
              ffn_norm[0].reshape(1, d), ffn_w_gate_up[0].astype(BF16),
              ffn_w_down[0].astype(BF16), fg, final=False)

    q, k, v = _b_proj(x2, attn_norm[1].reshape(1, d), b_w_in[0].astype(BF16),
                      b_b_in[0].reshape(1, -1))
    o = _swa_attn(q, k, v, b_sinks[0], bt, batch, seq)
    x2 = _ffn(x2, o, b_w_out[0].astype(BF16), b_b_out[0].reshape(1, d),
              ffn_norm[1].reshape(1, d), ffn_w_gate_up[1].astype(BF16),
              ffn_w_down[1].astype(BF16), fg, final=True)
    return x2.reshape(batch, seq, d)
```

```python
import functools
import math

import numpy as np
import jax
import jax.numpy as jnp
from jax import lax
from jax.experimental import pallas as pl
from jax.experimental.pallas import tpu as pltpu

F32 = jnp.float32
BF16 = jnp.bfloat16
I32 = jnp.int32

EPS = 1e-6
N_HEADS = 16
A_LATENT = 128
A_VHEAD = 64
IDX_HEADS = 8
IDX_DIM = 64
TOPK_MAX = 256
B_KV_HEADS = 2
B_HEAD_DIM = 64
REL_BUCKETS = 32
REL_MAX_DIST = 128

BLK = 128
SC = 256
INT_MIN = -(2 ** 31)
NEG_INF = float("-inf")
LOG2E = math.log2(math.e)
VMEM_LIMIT = 56 * 1024 * 1024


def _rms(xf, g):
    ms = jnp.mean(xf * xf, axis=-1, keepdims=True)
    return xf * lax.rsqrt(ms + EPS) * g


def _dot(a, b):
    return jnp.dot(a, b, preferred_element_type=F32)


def _dot_nt(a, b):
    return lax.dot_general(a, b, (((1,), (1,)), ((), ())), preferred_element_type=F32)


def _params(n_axes):
    return pltpu.CompilerParams(
        dimension_semantics=("arbitrary",) * n_axes, vmem_limit_bytes=VMEM_LIMIT)


def _resident(shape):
    zeros = (0,) * len(shape)
    return pl.BlockSpec(shape, lambda *_: zeros, pipeline_mode=pl.Buffered(1))


def _bucket(rel):
    n = np.maximum(rel, 0)
    max_exact = REL_BUCKETS // 2
    nf = np.maximum(n, max_exact).astype(np.float64)
    large = max_exact + (np.log(nf / max_exact) / math.log(REL_MAX_DIST / max_exact)
                         * (REL_BUCKETS - max_exact)).astype(np.int32)
    large = np.minimum(large, REL_BUCKETS - 1)
    return np.where(n < max_exact, n, large).astype(np.int32)


def _bucket_tables():
    r = np.arange(BLK)[:, None]
    c = np.arange(BLK)[None, :]
    rel = np.stack([r - c, BLK + r - c, c - r, BLK + c - r, np.full((BLK, BLK), 2 * BLK)])
    return _bucket(rel)


def _bias_kernel(rb_ref, bidx_ref, bt_ref, btmt_ref):
    for kind in range(5):
        bidx = bidx_ref[kind]
        for h in range(N_HEADS):
            def body(b, t, h=h, bidx=bidx):
                return jnp.where(bidx == b, rb_ref[b, h], t)
            t = lax.fori_loop(0, REL_BUCKETS, body, jnp.zeros((BLK, BLK), F32))
            if kind < 2:
                bt_ref[kind, h] = t
            else:
                btmt_ref[kind - 2, h] = (t - rb_ref[REL_BUCKETS - 1, h]) * LOG2E


def _bias_tiles(rel_bias):
    return pl.pallas_call(
        _bias_kernel,
        out_shape=(jax.ShapeDtypeStruct((2, N_HEADS, BLK, BLK), F32),
                   jax.ShapeDtypeStruct((3, N_HEADS, BLK, BLK), F32)),
        in_specs=[pl.BlockSpec(memory_space=pltpu.SMEM),
                  pl.BlockSpec(memory_space=pltpu.VMEM)],
        out_specs=(pl.BlockSpec(memory_space=pltpu.VMEM),
                   pl.BlockSpec(memory_space=pltpu.VMEM)),
        name="bias_tiles",
    )(rel_bias, jnp.asarray(_bucket_tables()))


A_QLAT = N_HEADS * A_LATENT
A_QIDX = IDX_HEADS * IDX_DIM
A_WT_ROWS = A_QLAT + A_QIDX + 16


def _a_proj_kernel(x_ref, g_ref, wt_ref, wn_ref, lg_ref,
                   qr_ref, qir_ref, wto_ref, kidx_ref, ckv_ref, ckvt_ref,
                   h_ref, ht_ref):
    tm = x_ref.shape[0]
    hf = _rms(x_ref[...], g_ref[...])
    h_ref[...] = hf.astype(BF16)
    ht_ref[...] = hf.T.astype(BF16)
    for c in range(A_QLAT // 512):
        r = _dot(wt_ref[c * 512:(c + 1) * 512, :], ht_ref[...]).astype(BF16)
        for hh in range(4):
            h = 4 * c + hh
            for blk in range(tm // BLK):
                qr_ref[blk, :, h * BLK:(h + 1) * BLK] = (
                    r[hh * A_LATENT:(hh + 1) * A_LATENT, blk * BLK:(blk + 1) * BLK])
    r = _dot(wt_ref[A_QLAT:A_QLAT + A_QIDX, :], ht_ref[...]).astype(BF16)
    for j in range(IDX_HEADS):
        for blk in range(tm // BLK):
            qir_ref[blk, :, j * BLK:(j + 1) * BLK] = (
                r[j * IDX_DIM:(j + 1) * IDX_DIM, blk * BLK:(blk + 1) * BLK])
    w = _dot(wt_ref[A_QLAT + A_QIDX:, :], ht_ref[...])
    wto_ref[...] = w[:IDX_HEADS] * (IDX_HEADS ** -0.5)
    t = _dot(h_ref[...], wn_ref[...])
    ckv = _rms(t[:, :A_LATENT], lg_ref[...])
    ckv_ref[...] = ckv.astype(BF16)
    kidx_ref[...] = t[:, A_LATENT:A_LATENT + IDX_DIM].astype(BF16)
    for b2 in range(tm // SC):
        ckvt_ref[b2] = ckv[b2 * SC:(b2 + 1) * SC, :].T.astype(BF16)


def _a_proj(x2, g, wt, wn, lg, tm=512):
    n, d = x2.shape
    row = lambda i: (i, 0)
    blk3 = lambda i: (i, 0, 0)
    return pl.pallas_call(
        _a_proj_kernel,
        grid=(n // tm,),
        in_specs=[pl.BlockSpec((tm, d), row),
                  _resident((1, d)),
                  _resident((A_WT_ROWS, d)),
                  _resident((d, 256)),
                  _resident((1, A_LATENT))],
        out_specs=(pl.BlockSpec((tm // BLK, A_LATENT, A_QLAT), blk3),
                   pl.BlockSpec((tm // BLK, IDX_DIM, IDX_HEADS * BLK), blk3),
                   pl.BlockSpec((IDX_HEADS, tm), lambda i: (0, i)),
                   pl.BlockSpec((tm, IDX_DIM), row),
                   pl.BlockSpec((tm, A_LATENT), row),
                   pl.BlockSpec((tm // SC, A_LATENT, SC), blk3)),
        out_shape=(jax.ShapeDtypeStruct((n // BLK, A_LATENT, A_QLAT), BF16),
                   jax.ShapeDtypeStruct((n // BLK, IDX_DIM, IDX_HEADS * BLK), BF16),
                   jax.ShapeDtypeStruct((IDX_HEADS, n), F32),
                   jax.ShapeDtypeStruct((n, IDX_DIM), BF16),
                   jax.ShapeDtypeStruct((n, A_LATENT), BF16),
                   jax.ShapeDtypeStruct((n // SC, A_LATENT, SC), BF16)),
        scratch_shapes=[pltpu.VMEM((tm, d), BF16), pltpu.VMEM((d, tm), BF16)],
        compiler_params=_params(1),
        name="a_proj",
    )(x2, g, wt, wn, lg)


def _dsa_kernel(qr_ref, qir_ref, wt_ref, kidx_ref, ckv_ref, ckvt_ref, wuvt_ref,
                btmt_ref, tri_ref, out_ref,
                key_ref, madd_ref, l_ref, acc_ref, pb_ref, *, k_sel, n_sc):
    i = pl.program_id(1)
    nsc = i // 2 + 1
    kf = float(k_sel)
    krow = lax.broadcasted_iota(I32, (SC, BLK), 0)
    qpos = i * BLK + lax.broadcasted_iota(I32, (SC, BLK), 1)

    def causal(sc):
        return sc * SC + krow <= qpos

    def key_rows(ref, sc):
        return ref[pl.ds(pl.multiple_of(sc * SC, SC), SC), :]

    def colsum(x):
        return jnp.sum(x.reshape(SC // 8, 8, BLK), axis=0)

    w2 = wt_ref[...] * (IDX_DIM ** -0.5)

    def index_sc(sc, last):
        d = _dot(key_rows(kidx_ref, sc), qir_ref[0])
        s = w2[0:1, :] * jnp.maximum(d[:, 0:BLK], 0.0)
        for j in range(1, IDX_HEADS):
            s = s + w2[j:j + 1, :] * jnp.maximum(d[:, j * BLK:(j + 1) * BLK], 0.0)
        bits = lax.bitcast_convert_type(s, I32)
        key = bits ^ ((bits >> 31) & 0x7FFFFFFF)
        key = jnp.where(key == -1, 0, key)
        if last:
            key = jnp.where(causal(sc), key, INT_MIN)
        key_ref[sc] = key

    def index_body(sc, carry):
        index_sc(sc, False)
        return carry
    lax.fori_loop(0, nsc - 1, index_body, 0)
    index_sc(nsc - 1, True)

    @pl.when((i + 1) * BLK <= k_sel)
    def _():
        def body(sc, carry):
            madd_ref[sc] = jnp.where(causal(sc), 0.0, NEG_INF)
            return carry
        lax.fori_loop(0, nsc, body, 0)

    def search(n):
        def count_ge(cand):
            part = jnp.zeros((8, BLK), F32)
            for sc in range(n):
                part = part + colsum(jnp.where(key_ref[sc] >= cand, 1.0, 0.0))
            return jnp.sum(part, axis=0, keepdims=True)

        ans = jnp.where(count_ge(jnp.zeros((1, BLK), I32)) >= kf, 0, INT_MIN)

        def bit_body(it, ans):
            cand = ans | jnp.left_shift(1, 30 - it)
            return jnp.where(count_ge(cand) >= kf, cand, ans)
        return lax.fori_loop(0, 31, bit_body, ans)

    @pl.when((i + 1) * BLK > k_sel)
    def _():
        for n in range(1, n_sc + 1):
            @pl.when(nsc == n)
            def _(n=n):
                key_ref[n_sc] = jnp.broadcast_to(search(n), (SC, BLK))
        thr = key_ref[n_sc][0:1, :]

        def gt_body(sc, part):
            return part + colsum(jnp.where(key_ref[sc] > thr, 1.0, 0.0))
        n_gt = jnp.sum(lax.fori_loop(0, nsc, gt_body, jnp.zeros((8, BLK), F32)),
                       axis=0, keepdims=True)
        need = kf - n_gt

        def mask_body(sc, off):
            key = key_ref[sc]
            eq = key == thr
            eqf = jnp.where(eq, 1.0, 0.0)
            before = _dot(tri_ref[...], eqf.astype(BF16)) + off
            sel = (key > thr) | (eq & (before < need))
            madd_ref[sc] = jnp.where(sel, 0.0, NEG_INF)
            return off + jnp.sum(eqf, axis=0, keepdims=True)
        lax.fori_loop(0, nsc, mask_body, jnp.zeros((1, BLK), F32))

    c1 = (A_LATENT ** -0.5) * LOG2E

    def logits_sc(sc, m, near):
        lt = _dot(key_rows(ckv_ref, sc), qr_ref[0])
        madd = madd_ref[sc]
        new_m = []
        for h in range(N_HEADS):
            hs = slice(h * BLK, (h + 1) * BLK)
            v = lt[:, hs] * c1 + madd
            if near:
                halves = []
                for half in range(2):
                    kind = jnp.clip(i - (2 * sc + half), 0, 2)
                    halves.append(v[half * BLK:(half + 1) * BLK] + btmt_ref[kind, h])
                v = jnp.concatenate(halves, axis=0)
            l_ref[sc, :, hs] = v
            new_m.append(jnp.maximum(m[:, hs], jnp.max(v, axis=0, keepdims=True)))
        return jnp.concatenate(new_m, axis=1)

    m = jnp.full((1, N_HEADS * BLK), NEG_INF, F32)
    m = lax.fori_loop(0, nsc - 2, lambda sc, m: logits_sc(sc, m, False), m)
    m_ref = acc_ref.at[0:8, :]
    m_ref[...] = jnp.broadcast_to(m, (8, N_HEADS * BLK))

    @pl.when(nsc >= 2)
    def _():
        m_ref[...] = jnp.broadcast_to(
            logits_sc(nsc - 2, m_ref[0:1, :], True), (8, N_HEADS * BLK))
    m = logits_sc(nsc - 1, m_ref[0:1, :], True)

    acc_ref[...] = jnp.zeros(acc_ref.shape, F32)

    def pv_body(sc, s):
        new_s = []
        for h in range(N_HEADS):
            hs = slice(h * BLK, (h + 1) * BLK)
            p = jnp.exp2(l_ref[sc, :, hs] - m[:, hs])
            new_s.append(s[:, hs] + colsum(p))
            pb_ref[:, hs] = p.astype(BF16)
        acc_ref[...] = acc_ref[...] + _dot(ckvt_ref[sc], pb_ref[...])
        return jnp.concatenate(new_s, axis=1)
    s = lax.fori_loop(0, nsc, pv_body, jnp.zeros((8, N_HEADS * BLK), F32))
    inv = 1.0 / jnp.sum(s, axis=0, keepdims=True)

    outs = []
    for h in range(N_HEADS):
        hs = slice(h * BLK, (h + 1) * BLK)
        o = (acc_ref[:, hs] * inv[:, hs]).astype(BF16)
        outs.append(_dot(wuvt_ref[h], o))
    out_ref[...] = jnp.concatenate(outs, axis=0).T.astype(BF16)


def _dsa_attn(qr, qir, wt, kidx, ckv, ckvt, wuvt, btmt, batch, seq):
    nblk = seq // BLK
    n_sc = seq // SC
    k_sel = min(TOPK_MAX, seq // 4)
    hcols = N_HEADS * BLK
    tri = jnp.asarray(np.tril(np.ones((SC, SC), np.float32), -1), BF16)
    return pl.pallas_call(
        functools.partial(_dsa_kernel, k_sel=k_sel, n_sc=n_sc),
        grid=(batch, nblk),
        in_specs=[pl.BlockSpec((1, A_LATENT, hcols), lambda b, i: (b * nblk + i, 0, 0)),
                  pl.BlockSpec((1, IDX_DIM, IDX_HEADS * BLK), lambda b, i: (b * nblk + i, 0, 0)),
                  pl.BlockSpec((IDX_HEADS, BLK), lambda b, i: (0, b * nblk + i)),
                  pl.BlockSpec((seq, IDX_DIM), lambda b, i: (b, 0)),
                  pl.BlockSpec((seq, A_LATENT), lambda b, i: (b, 0)),
                  pl.BlockSpec((n_sc, A_LATENT, SC), lambda b, i: (b, 0, 0)),
                  _resident((N_HEADS, A_VHEAD, A_LATENT)),
                  _resident((3, N_HEADS, BLK, BLK)),
                  _resident((SC, SC))],
        out_specs=pl.BlockSpec((BLK, N_HEADS * A_VHEAD), lambda b, i: (b * nblk + i, 0)),
        out_shape=jax.ShapeDtypeStruct((batch * seq, N_HEADS * A_VHEAD), BF16),
        scratch_shapes=[
            pltpu.VMEM((n_sc + 1, SC, BLK), I32),
            pltpu.VMEM((n_sc, SC, BLK), F32),
            pltpu.VMEM((n_sc, SC, hcols), F32),
            pltpu.VMEM((A_LATENT, hcols), F32),
            pltpu.VMEM((SC, hcols), BF16),
        ],
        compiler_params=_params(2),
        name="dsa_attn",
    )(qr, qir, wt, kidx, ckv, ckvt, wuvt, btmt, tri)


B_Q = N_HEADS * B_HEAD_DIM
B_KV = B_KV_HEADS * B_HEAD_DIM


def _b_proj_kernel(x_ref, g_ref, w_ref, b_ref, q_ref, k_ref, v_ref, h_ref):
    h_ref[...] = _rms(x_ref[...], g_ref[...]).astype(BF16)
    for c in range(B_Q // 512):
        sl = slice(c * 512, (c + 1) * 512)
        q_ref[:, sl] = (_dot(h_ref[...], w_ref[:, sl]) + b_ref[:, sl]).astype(BF16)
    t = _dot(h_ref[...], w_ref[:, B_Q:]) + b_ref[:, B_Q:]
    k_ref[...] = t[:, :B_KV].astype(BF16)
    v_ref[...] = t[:, B_KV:].astype(BF16)


def _b_proj(x2, g, w, b, tm=512):
    n, d = x2.shape
    row = lambda i: (i, 0)
    return pl.pallas_call(
        _b_proj_kernel,
        grid=(n // tm,),
        in_specs=[pl.BlockSpec((tm, d), row),
                  _resident((1, d)),
                  _resident((d, B_Q + 2 * B_KV)),
                  _resident((1, B_Q + 2 * B_KV))],
        out_specs=(pl.BlockSpec((tm, B_Q), row),
                   pl.BlockSpec((tm, B_KV), row),
                   pl.BlockSpec((tm, B_KV), row)),
        out_shape=(jax.ShapeDtypeStruct((n, B_Q), BF16),
                   jax.ShapeDtypeStruct((n, B_KV), BF16),
                   jax.ShapeDtypeStruct((n, B_KV), BF16)),
        scratch_shapes=[pltpu.VMEM((tm, d), BF16)],
        compiler_params=_params(1),
        name="b_proj",
    )(x2, g, w, b)


def _swa_kernel(sink_ref, q_ref, kp_ref, kc_ref, vp_ref, vc_ref, bt_ref, out_ref):
    n = pl.program_id(1)
    row = lax.broadcasted_iota(I32, (BLK, BLK), 0)
    col = lax.broadcasted_iota(I32, (BLK, BLK), 1)
    madd_cur = jnp.where(col <= row, 0.0, NEG_INF)
    madd_prev = jnp.where((col > row) & (n >= 1), 0.0, NEG_INF)
    scale = B_HEAD_DIM ** -0.5
    group = N_HEADS // B_KV_HEADS
    for hp in range(N_HEADS // 2):
        parts = []
        for h in (2 * hp, 2 * hp + 1):
            kv = h // group
            ksl = slice(kv * B_HEAD_DIM, (kv + 1) * B_HEAD_DIM)
            qh = q_ref[:, h * B_HEAD_DIM:(h + 1) * B_HEAD_DIM]
            lp = _dot_nt(qh, kp_ref[:, ksl]) * scale + bt_ref[1, h] + madd_prev
            lc = _dot_nt(qh, kc_ref[:, ksl]) * scale + bt_ref[0, h] + madd_cur
            sink = sink_ref[h]
            m = jnp.maximum(jnp.max(jnp.maximum(lp, lc), axis=-1, keepdims=True), sink)
            pp = jnp.exp(lp - m)
            pc = jnp.exp(lc - m)
            den = jnp.sum(pp + pc, axis=-1, keepdims=True) + jnp.exp(sink - m)
            o = _dot(pp.astype(BF16), vp_ref[:, ksl]) + _dot(pc.astype(BF16), vc_ref[:, ksl])
            parts.append(o / den)
        out_ref[:, hp * 2 * B_HEAD_DIM:(hp + 1) * 2 * B_HEAD_DIM] = (
            jnp.concatenate(parts, axis=1).astype(BF16))


def _swa_attn(q, k, v, sinks, bt, batch, seq):
    nb = seq // BLK
    cur = lambda b, n: (b * nb + n, 0)
    prev = lambda b, n: (b * nb + jnp.maximum(n - 1, 0), 0)
    return pl.pallas_call(
        _swa_kernel,
        grid=(batch, nb),
        in_specs=[pl.BlockSpec(memory_space=pltpu.SMEM),
                  pl.BlockSpec((BLK, B_Q), cur),
                  pl.BlockSpec((BLK, B_KV), prev),
                  pl.BlockSpec((BLK, B_KV), cur),
                  pl.BlockSpec((BLK, B_KV), prev),
                  pl.BlockSpec((BLK, B_KV), cur),
                  _resident((2, N_HEADS, BLK, BLK))],
        out_specs=pl.BlockSpec((BLK, B_Q), cur),
        out_shape=jax.ShapeDtypeStruct((batch * seq, B_Q), BF16),
        compiler_params=_params(2),
        name="swa_attn",
    )(sinks, q, k, k, v, v, bt)


FF_CHUNK = 256


def _ffn_kernel(x_ref, o_ref, wo_ref, bo_ref, g_ref, wgu_ref, wd_ref, fg_ref,
                out_ref, h_ref, act_ref, *, d_ff, final):
    x1 = x_ref[...] + _dot(o_ref[...], wo_ref[...]) + bo_ref[...]
    out_ref[...] = x1
    h_ref[...] = _rms(x1, g_ref[...]).astype(BF16)
    for c in range(d_ff // FF_CHUNK):
        gate = _dot(h_ref[...], wgu_ref[:, c * FF_CHUNK:(c + 1) * FF_CHUNK])
        up = _dot(h_ref[...], wgu_ref[:, d_ff + c * FF_CHUNK:d_ff + (c + 1) * FF_CHUNK])
        act_ref[:, c * FF_CHUNK:(c + 1) * FF_CHUNK] = (
            gate * jax.nn.sigmoid(gate) * up).astype(BF16)
    y = out_ref[...] + _dot(act_ref[...], wd_ref[...])
    if final:
        y = _rms(y, fg_ref[...])
    out_ref[...] = y


def _ffn(x2, o, wo, bo, g, wgu, wd, fg, final, tm=512):
    n, d = x2.shape
    d_ff = wd.shape[0]
    row = lambda i: (i, 0)
    return pl.pallas_call(
        functools.partial(_ffn_kernel, d_ff=d_ff, final=final),
        grid=(n // tm,),
        in_specs=[pl.BlockSpec((tm, d), row),
                  pl.BlockSpec((tm, o.shape[1]), row),
                  _resident(wo.shape),
                  _resident((1, d)),
                  _resident((1, d)),
                  _resident(wgu.shape),
                  _resident(wd.shape),
                  _resident((1, d))],
        out_specs=pl.BlockSpec((tm, d), row),
        out_shape=jax.ShapeDtypeStruct((n, d), F32),
        scratch_shapes=[pltpu.VMEM((tm, d), BF16),
                        pltpu.VMEM((tm, d_ff), BF16)],
        compiler_params=_params(1),
        name="ffn",
    )(x2, o, wo, bo, g, wgu, wd, fg)


def kernel(x, rel_bias, attn_norm, ffn_norm, final_norm, a_w_in, a_latent_norm, a_w_uv, a_w_out, b_w_in, b_b_in, b_sinks, b_w_out, b_b_out, ffn_w_gate_up, ffn_w_down):
    batch, seq, d = x.shape
    x2 = x.reshape(batch * seq, d)
    bt, btmt = _bias_tiles(rel_bias)
    fg = final_norm.reshape(1, d)

    wa = a_w_in[0]
    o_ckv, o_qidx, o_kidx, o_widx = A_QLAT, A_QLAT + A_LATENT, A_QLAT + A_LATENT + A_QIDX, \
        A_QLAT + A_LATENT + A_QIDX + IDX_DIM
    wt = jnp.concatenate(
        [wa[:, :o_ckv], wa[:, o_qidx:o_kidx], wa[:, o_widx:], jnp.zeros((d, 16 - IDX_HEADS), F32)],
        axis=1).T.astype(BF16)
    wn = jnp.concatenate(
        [wa[:, o_ckv:o_qidx], wa[:, o_kidx:o_widx], jnp.zeros((d, 256 - A_LATENT - IDX_DIM), F32)],
        axis=1).astype(BF16)
    qr, qir, wto, kidx, ckv, ckvt = _a_proj(
        x2, attn_norm[0].reshape(1, d), wt, wn, a_latent_norm[0].reshape(1, A_LATENT))
    wuvt = jnp.swapaxes(a_w_uv[0], 1, 2).astype(BF16)
    o = _dsa_attn(qr, qir, wto, kidx, ckv, ckvt, wuvt, btmt, batch, seq)
    x2 = _ffn(x2, o, a_w_out[0].astype(BF16), jnp.zeros((1, d), F32),
              ffn_norm[0].reshape(1, d), ffn_w_gate_up[0].astype(BF16),
              ffn_w_down[0].astype(BF16), fg, final=False)

    q, k, v = _b_proj(x2, attn_norm[1].reshape(1, d), b_w_in[0].astype(BF16),
                      b_b_in[0].reshape(1, -1))
    o = _swa_attn(q, k, v, b_sinks[0], bt, batch, seq)
    x2 = _ffn(x2, o, b_w_out[0].astype(BF16), b_b_out[0].reshape(1, d),
              ffn_norm[1].reshape(1, d), ffn_w_gate_up[1].astype(BF16),
              ffn_w_down[1].astype(BF16), fg, final=True)
    return x2.reshape(batch, seq, d)
```

```python
import functools
import math

import numpy as np
import jax
import jax.numpy as jnp
from jax import lax
from jax.experimental import pallas as pl
from jax.experimental.pallas import tpu as pltpu

F32 = jnp.float32
BF16 = jnp.bfloat16
I32 = jnp.int32
I16 = jnp.int16

EPS = 1e-6
N_HEADS = 16
A_LATENT = 128
A_VHEAD = 64
IDX_HEADS = 8
IDX_DIM = 64
TOPK_MAX = 256
B_KV_HEADS = 2
B_HEAD_DIM = 64
REL_BUCKETS = 32
REL_MAX_DIST = 128

BLK = 128
SC = 256
INT_MIN = -(2 ** 31)
NEG_INF = float("-inf")
LOG2E = math.log2(math.e)
VMEM_LIMIT = 56 * 1024 * 1024


def _rms(xf, g):
    ms = jnp.mean(xf * xf, axis=-1, keepdims=True)
    return xf * lax.rsqrt(ms + EPS) * g


def _dot(a, b):
    return jnp.dot(a, b, preferred_element_type=F32)


def _dot_nt(a, b):
    return lax.dot_general(a, b, (((1,), (1,)), ((), ())), preferred_element_type=F32)


def _params(n_axes):
    return pltpu.CompilerParams(
        dimension_semantics=("arbitrary",) * n_axes, vmem_limit_bytes=VMEM_LIMIT)


def _resident(shape):
    zeros = (0,) * len(shape)
    return pl.BlockSpec(shape, lambda *_: zeros, pipeline_mode=pl.Buffered(1))


def _bucket(rel):
    n = np.maximum(rel, 0)
    max_exact = REL_BUCKETS // 2
    nf = np.maximum(n, max_exact).astype(np.float64)
    large = max_exact + (np.log(nf / max_exact) / math.log(REL_MAX_DIST / max_exact)
                         * (REL_BUCKETS - max_exact)).astype(np.int32)
    large = np.minimum(large, REL_BUCKETS - 1)
    return np.where(n < max_exact, n, large).astype(np.int32)


def _bucket_tables():
    k = np.arange(BLK)[:, None]
    q = np.arange(BLK)[None, :]
    return _bucket(np.stack([q - k, BLK + q - k]))


def _bias_kernel(rb_ref, bidx_ref, bts_ref, btd_ref):
    k = lax.broadcasted_iota(I32, (BLK, BLK), 0)
    q = lax.broadcasted_iota(I32, (BLK, BLK), 1)
    window = (k <= q, k > q)
    for h in range(N_HEADS):
        for kind in range(2):
            bidx = bidx_ref[kind]

            def body(b, t, h=h, bidx=bidx):
                return jnp.where(bidx == b, rb_ref[b, h], t)
            t = lax.fori_loop(0, REL_BUCKETS, body, jnp.zeros((BLK, BLK), F32))
            bts_ref[kind, h] = jnp.where(window[kind], t * LOG2E, NEG_INF)
            btd_ref[kind, h] = (t - rb_ref[REL_BUCKETS - 1, h]) * LOG2E
        bts_ref[2, h] = jnp.full((BLK, BLK), NEG_INF, F32)
        btd_ref[2, h] = jnp.zeros((BLK, BLK), F32)


def _bias_tiles(rel_bias):
    return pl.pallas_call(
        _bias_kernel,
        out_shape=(jax.ShapeDtypeStruct((3, N_HEADS, BLK, BLK), F32),
                   jax.ShapeDtypeStruct((3, N_HEADS, BLK, BLK), F32)),
        in_specs=[pl.BlockSpec(memory_space=pltpu.SMEM),
                  pl.BlockSpec(memory_space=pltpu.VMEM)],
        out_specs=(pl.BlockSpec(memory_space=pltpu.VMEM),
                   pl.BlockSpec(memory_space=pltpu.VMEM)),
        name="bias_tiles",
    )(rel_bias, jnp.asarray(_bucket_tables()))


A_QLAT = N_HEADS * A_LATENT
A_QIDX = IDX_HEADS * IDX_DIM
A_WT_ROWS = A_QLAT + A_QIDX + 16


def _a_proj_kernel(x_ref, g_ref, wt_ref, wn_ref, lg_ref,
                   qr_ref, qir_ref, wto_ref, kidx_ref, ckv_ref, ckvt_ref,
                   h_ref, ht_ref):
    tm = x_ref.shape[0]
    hf = _rms(x_ref[...], g_ref[...])
    h_ref[...] = hf.astype(BF16)
    ht_ref[...] = hf.T.astype(BF16)
    for c in range(A_QLAT // 512):
        r = _dot(wt_ref[c * 512:(c + 1) * 512, :], ht_ref[...]).astype(BF16)
        for hh in range(4):
            h = 4 * c + hh
            for blk in range(tm // BLK):
                qr_ref[blk, :, h * BLK:(h + 1) * BLK] = (
                    r[hh * A_LATENT:(hh + 1) * A_LATENT, blk * BLK:(blk + 1) * BLK])
    r = _dot(wt_ref[A_QLAT:A_QLAT + A_QIDX, :], ht_ref[...]).astype(BF16)
    for j in range(IDX_HEADS):
        for blk in range(tm // BLK):
            qir_ref[blk, :, j * BLK:(j + 1) * BLK] = (
                r[j * IDX_DIM:(j + 1) * IDX_DIM, blk * BLK:(blk + 1) * BLK])
    w = _dot(wt_ref[A_QLAT + A_QIDX:, :], ht_ref[...])
    wto_ref[...] = w[:IDX_HEADS] * (IDX_HEADS ** -0.5)
    t = _dot(h_ref[...], wn_ref[...])
    ckv = _rms(t[:, :A_LATENT], lg_ref[...])
    ckv_ref[...] = ckv.astype(BF16)
    kidx_ref[...] = t[:, A_LATENT:A_LATENT + IDX_DIM].astype(BF16)
    for b2 in range(tm // SC):
        ckvt_ref[b2] = ckv[b2 * SC:(b2 + 1) * SC, :].T.astype(BF16)


def _a_proj(x2, g, wt, wn, lg, tm=512):
    n, d = x2.shape
    row = lambda i: (i, 0)
    blk3 = lambda i: (i, 0, 0)
    return pl.pallas_call(
        _a_proj_kernel,
        grid=(n // tm,),
        in_specs=[pl.BlockSpec((tm, d), row),
                  _resident((1, d)),
                  _resident((A_WT_ROWS, d)),
                  _resident((d, 256)),
                  _resident((1, A_LATENT))],
        out_specs=(pl.BlockSpec((tm // BLK, A_LATENT, A_QLAT), blk3),
                   pl.BlockSpec((tm // BLK, IDX_DIM, IDX_HEADS * BLK), blk3),
                   pl.BlockSpec((IDX_HEADS, tm), lambda i: (0, i)),
                   pl.BlockSpec((tm, IDX_DIM), row),
                   pl.BlockSpec((tm, A_LATENT), row),
                   pl.BlockSpec((tm // SC, A_LATENT, SC), blk3)),
        out_shape=(jax.ShapeDtypeStruct((n // BLK, A_LATENT, A_QLAT), BF16),
                   jax.ShapeDtypeStruct((n // BLK, IDX_DIM, IDX_HEADS * BLK), BF16),
                   jax.ShapeDtypeStruct((IDX_HEADS, n), F32),
                   jax.ShapeDtypeStruct((n, IDX_DIM), BF16),
                   jax.ShapeDtypeStruct((n, A_LATENT), BF16),
                   jax.ShapeDtypeStruct((n // SC, A_LATENT, SC), BF16)),
        scratch_shapes=[pltpu.VMEM((tm, d), BF16), pltpu.VMEM((d, tm), BF16)],
        compiler_params=_params(1),
        name="a_proj",
    )(x2, g, wt, wn, lg)


def _dsa_kernel(qr_ref, qir_ref, wt_ref, kidx_ref, ckv_ref, ckvt_ref, wuvt_ref,
                btd_ref, tri_ref, out_ref,
                key_ref, hi_ref, lo_ref, madd_ref, l_ref, acc_ref, pb_ref, *, k_sel, n_sc):
    i = pl.program_id(1)
    nsc = i // 2 + 1
    kf = float(k_sel)
    krow = lax.broadcasted_iota(I32, (SC, BLK), 0)
    qpos = i * BLK + lax.broadcasted_iota(I32, (SC, BLK), 1)

    def causal(sc):
        return sc * SC + krow <= qpos

    def key_rows(ref, sc):
        return ref[pl.ds(pl.multiple_of(sc * SC, SC), SC), :]

    def colsum(x):
        return jnp.sum(x.reshape(SC // 8, 8, BLK), axis=0)

    w2 = wt_ref[...] * (IDX_DIM ** -0.5)

    def index_sc(sc, last):
        d = _dot(key_rows(kidx_ref, sc), qir_ref[0])
        s = w2[0:1, :] * jnp.maximum(d[:, 0:BLK], 0.0)
        for j in range(1, IDX_HEADS):
            s = s + w2[j:j + 1, :] * jnp.maximum(d[:, j * BLK:(j + 1) * BLK], 0.0)
        bits = lax.bitcast_convert_type(s, I32)
        key = bits ^ ((bits >> 31) & 0x7FFFFFFF)
        key = jnp.where(key == -1, 0, key)
        if last:
            key = jnp.where(causal(sc), key, INT_MIN)
        key_ref[sc] = key
        hi_ref[sc] = (key >> 16).astype(I16)
        lo_ref[sc] = ((key & 0xFFFF) - 2 ** 15).astype(I16)

    def index_body(sc, carry):
        index_sc(sc, False)
        return carry
    lax.fori_loop(0, nsc - 1, index_body, 0)
    index_sc(nsc - 1, True)

    @pl.when((i + 1) * BLK <= k_sel)
    def _():
        def body(sc, carry):
            madd_ref[sc] = jnp.where(causal(sc), 0.0, NEG_INF)
            return carry
        lax.fori_loop(0, nsc, body, 0)

    def search16(ref, n):
        def count_ge(cand):
            c16 = jnp.broadcast_to(cand, (16, BLK)).astype(I16)
            parts = [jnp.zeros((16, BLK), I16)] * 8
            for sc in range(n):
                x = ref[sc]
                for t in range(SC // 16):
                    ge = x[t * 16:(t + 1) * 16] >= c16
                    parts[t % 8] = parts[t % 8] + jnp.where(ge, one16, zero16)
            part = sum(parts[1:], parts[0])
            return jnp.sum(part.astype(F32), axis=0, keepdims=True)

        one16, zero16 = jnp.ones((), I16), jnp.zeros((), I16)
        ans = jnp.where(count_ge(jnp.zeros((1, BLK), I32)) >= kf, 0, -(2 ** 15))

        def bit_body(it, ans):
            cand = ans | jnp.left_shift(1, 14 - it)
            return jnp.where(count_ge(cand) >= kf, cand, ans)
        return lax.fori_loop(0, 15, bit_body, ans)

    def search(n):
        hi_ans = search16(hi_ref, n)
        h16 = jnp.broadcast_to(hi_ans, (16, BLK)).astype(I16)
        top, bottom = jnp.full((), 2 ** 15 - 1, I16), jnp.full((), -(2 ** 15), I16)
        for sc in range(n):
            hi = hi_ref[sc].reshape(SC // 16, 16, BLK)
            lo = lo_ref[sc].reshape(SC // 16, 16, BLK)
            lo = jnp.where(hi == h16, lo, jnp.where(hi > h16, top, bottom))
            lo_ref[sc] = lo.reshape(SC, BLK)
        lo_ans = search16(lo_ref, n)
        return (hi_ans << 16) | (lo_ans + 2 ** 15)

    @pl.when((i + 1) * BLK > k_sel)
    def _():
        for n in range(1, n_sc + 1):
            @pl.when(nsc == n)
            def _(n=n):
                key_ref[n_sc] = jnp.broadcast_to(search(n), (SC, BLK))
        thr = key_ref[n_sc][0:1, :]

        def ge_body(sc, part):
            ge = key_ref[sc] >= thr
            madd_ref[sc] = jnp.where(ge, 0.0, NEG_INF)
            return part + colsum(jnp.where(ge, 1.0, 0.0))
        n_ge = jnp.sum(lax.fori_loop(0, nsc, ge_body, jnp.zeros((8, BLK), F32)),
                       axis=0, keepdims=True)

        @pl.when(jnp.max(n_ge) > kf)
        def _():
            def gt_body(sc, part):
                return part + colsum(jnp.where(key_ref[sc] > thr, 1.0, 0.0))
            n_gt = jnp.sum(lax.fori_loop(0, nsc, gt_body, jnp.zeros((8, BLK), F32)),
                           axis=0, keepdims=True)
            need = kf - n_gt

            def mask_body(sc, off):
                key = key_ref[sc]
                eq = key == thr
                eqf = jnp.where(eq, 1.0, 0.0)
                before = _dot(tri_ref[...], eqf.astype(BF16)) + off
                sel = (key > thr) | (eq & (before < need))
                madd_ref[sc] = jnp.where(sel, 0.0, NEG_INF)
                return off + jnp.sum(eqf, axis=0, keepdims=True)
            lax.fori_loop(0, nsc, mask_body, jnp.zeros((1, BLK), F32))

    c1 = (A_LATENT ** -0.5) * LOG2E

    def logits_sc(sc, m, near):
        lt = _dot(key_rows(ckv_ref, sc), qr_ref[0])
        madd = madd_ref[sc]
        new_m = []
        for h in range(N_HEADS):
            hs = slice(h * BLK, (h + 1) * BLK)
            v = lt[:, hs] * c1 + madd
            if near:
                halves = []
                for half in range(2):
                    kind = jnp.clip(i - (2 * sc + half), 0, 2)
                    halves.append(v[half * BLK:(half + 1) * BLK] + btd_ref[kind, h])
                v = jnp.concatenate(halves, axis=0)
            l_ref[sc, :, hs] = v
            new_m.append(jnp.maximum(m[:, hs], jnp.max(v, axis=0, keepdims=True)))
        return jnp.concatenate(new_m, axis=1)

    m = jnp.full((1, N_HEADS * BLK), NEG_INF, F32)
    m = lax.fori_loop(0, nsc - 2, lambda sc, m: logits_sc(sc, m, False), m)
    m_ref = acc_ref.at[0:8, :]
    m_ref[...] = jnp.broadcast_to(m, (8, N_HEADS * BLK))

    @pl.when(nsc >= 2)
    def _():
        m_ref[...] = jnp.broadcast_to(
            logits_sc(nsc - 2, m_ref[0:1, :], True), (8, N_HEADS * BLK))
    m = logits_sc(nsc - 1, m_ref[0:1, :], True)

    acc_ref[...] = jnp.zeros(acc_ref.shape, F32)

    def pv_body(sc, s):
        new_s = []
        for h in range(N_HEADS):
            hs = slice(h * BLK, (h + 1) * BLK)
            p = jnp.exp2(l_ref[sc, :, hs] - m[:, hs])
            new_s.append(s[:, hs] + colsum(p))
            pb_ref[:, hs] = p.astype(BF16)
        acc_ref[...] = acc_ref[...] + _dot(ckvt_ref[sc], pb_ref[...])
        return jnp.concatenate(new_s, axis=1)
    s = lax.fori_loop(0, nsc, pv_body, jnp.zeros((8, N_HEADS * BLK), F32))
    inv = 1.0 / jnp.sum(s, axis=0, keepdims=True)

    outs = []
    for h in range(N_HEADS):
        hs = slice(h * BLK, (h + 1) * BLK)
        o = (acc_ref[:, hs] * inv[:, hs]).astype(BF16)
        outs.append(_dot(wuvt_ref[h], o))
    out_ref[...] = jnp.concatenate(outs, axis=0).T.astype(BF16)


def _dsa_attn(qr, qir, wt, kidx, ckv, ckvt, wuvt, btd, batch, seq):
    nblk = seq // BLK
    n_sc = seq // SC
    k_sel = min(TOPK_MAX, seq // 4)
    hcols = N_HEADS * BLK
    tri = jnp.asarray(np.tril(np.ones((SC, SC), np.float32), -1), BF16)
    return pl.pallas_call(
        functools.partial(_dsa_kernel, k_sel=k_sel, n_sc=n_sc),
        grid=(batch, nblk),
        in_specs=[pl.BlockSpec((1, A_LATENT, hcols), lambda b, i: (b * nblk + i, 0, 0)),
                  pl.BlockSpec((1, IDX_DIM, IDX_HEADS * BLK), lambda b, i: (b * nblk + i, 0, 0)),
                  pl.BlockSpec((IDX_HEADS, BLK), lambda b, i: (0, b * nblk + i)),
                  pl.BlockSpec((seq, IDX_DIM), lambda b, i: (b, 0)),
                  pl.BlockSpec((seq, A_LATENT), lambda b, i: (b, 0)),
                  pl.BlockSpec((n_sc, A_LATENT, SC), lambda b, i: (b, 0, 0)),
                  _resident((N_HEADS, A_VHEAD, A_LATENT)),
                  _resident((3, N_HEADS, BLK, BLK)),
                  _resident((SC, SC))],
        out_specs=pl.BlockSpec((BLK, N_HEADS * A_VHEAD), lambda b, i: (b * nblk + i, 0)),
        out_shape=jax.ShapeDtypeStruct((batch * seq, N_HEADS * A_VHEAD), BF16),
        scratch_shapes=[
            pltpu.VMEM((n_sc + 1, SC, BLK), I32),
            pltpu.VMEM((n_sc, SC, BLK), I16),
            pltpu.VMEM((n_sc, SC, BLK), I16),
            pltpu.VMEM((n_sc, SC, BLK), F32),
            pltpu.VMEM((n_sc, SC, hcols), F32),
            pltpu.VMEM((A_LATENT, hcols), F32),
            pltpu.VMEM((SC, hcols), BF16),
        ],
        compiler_params=_params(2),
        name="dsa_attn",
    )(qr, qir, wt, kidx, ckv, ckvt, wuvt, btd, tri)


B_Q = N_HEADS * B_HEAD_DIM
B_KV = B_KV_HEADS * B_HEAD_DIM


def _b_proj_kernel(x_ref, g_ref, wt_ref, bcol_ref, wk_ref, bk_ref,
                   qr_ref, kk_ref, vt_ref, h_ref, ht_ref):
    tm = x_ref.shape[0]
    hf = _rms(x_ref[...], g_ref[...])
    h_ref[...] = hf.astype(BF16)
    ht_ref[...] = hf.T.astype(BF16)
    heads_per_dot = 512 // B_HEAD_DIM
    for c in range(B_Q // 512):
        rows = slice(c * 512, (c + 1) * 512)
        r = (_dot(wt_ref[rows, :], ht_ref[...]) + bcol_ref[rows, :]).astype(BF16)
        for hh in range(heads_per_dot):
            h = c * heads_per_dot + hh
            for blk in range(tm // BLK):
                qr_ref[blk, :, h * BLK:(h + 1) * BLK] = (
                    r[hh * B_HEAD_DIM:(hh + 1) * B_HEAD_DIM, blk * BLK:(blk + 1) * BLK])
    vt_ref[...] = (_dot(wt_ref[B_Q:, :], ht_ref[...]) + bcol_ref[B_Q:, :]).astype(BF16)
    t = _dot(h_ref[...], wk_ref[...]) + bk_ref[...]
    for kv in range(B_KV_HEADS):
        kk_ref[kv] = t[:, kv * B_HEAD_DIM:(kv + 1) * B_HEAD_DIM].astype(BF16)


def _b_proj(x2, g, wt, bcol, wk, bk, tm=512):
    n, d = x2.shape
    return pl.pallas_call(
        _b_proj_kernel,
        grid=(n // tm,),
        in_specs=[pl.BlockSpec((tm, d), lambda i: (i, 0)),
                  _resident((1, d)),
                  _resident((B_Q + B_KV, d)),
                  _resident((B_Q + B_KV, 1)),
                  _resident((d, B_KV)),
                  _resident((1, B_KV))],
        out_specs=(pl.BlockSpec((tm // BLK, B_HEAD_DIM, N_HEADS * BLK), lambda i: (i, 0, 0)),
                   pl.BlockSpec((B_KV_HEADS, tm, B_HEAD_DIM), lambda i: (0, i, 0)),
                   pl.BlockSpec((B_KV, tm), lambda i: (0, i))),
        out_shape=(jax.ShapeDtypeStruct((n // BLK, B_HEAD_DIM, N_HEADS * BLK), BF16),
                   jax.ShapeDtypeStruct((B_KV_HEADS, n, B_HEAD_DIM), BF16),
                   jax.ShapeDtypeStruct((B_KV, n), BF16)),
        scratch_shapes=[pltpu.VMEM((tm, d), BF16), pltpu.VMEM((d, tm), BF16)],
        compiler_params=_params(1),
        name="b_proj",
    )(x2, g, wt, bcol, wk, bk)


def _swa_kernel(sink_ref, qr_ref, kp_ref, kc_ref, vp_ref, vc_ref, bts_ref, out_ref):
    n = pl.program_id(1)
    prev_kind = jnp.where(n >= 1, 1, 2)
    c1 = (B_HEAD_DIM ** -0.5) * LOG2E
    group = N_HEADS // B_KV_HEADS
    outs = []
    for kv in range(B_KV_HEADS):
        cols = slice(kv * group * BLK, (kv + 1) * group * BLK)
        q = qr_ref[0, :, cols]
        lp = _dot(kp_ref[kv], q)
        lc = _dot(kc_ref[kv], q)
        lps, lcs, ms = [], [], []
        for g in range(group):
            h = kv * group + g
            hs = slice(g * BLK, (g + 1) * BLK)
            a = lp[:, hs] * c1 + bts_ref[prev_kind, h]
            b = lc[:, hs] * c1 + bts_ref[0, h]
            lps.append(a)
            lcs.append(b)
            ms.append(jnp.max(jnp.maximum(a, b), axis=0, keepdims=True))
        sink = sink_ref[:, cols] * LOG2E
        m = jnp.maximum(jnp.concatenate(ms, axis=1), sink)
        pps, pcs, dens = [], [], []
        for g in range(group):
            hs = slice(g * BLK, (g + 1) * BLK)
            pp = jnp.exp2(lps[g] - m[:, hs])
            pc = jnp.exp2(lcs[g] - m[:, hs])
            dens.append(jnp.sum(pp + pc, axis=0, keepdims=True))
            pps.append(pp.astype(BF16))
            pcs.append(pc.astype(BF16))
        den = jnp.concatenate(dens, axis=1) + jnp.exp2(sink - m)
        vrows = slice(kv * B_HEAD_DIM, (kv + 1) * B_HEAD_DIM)
        ot = (_dot(vp_ref[vrows, :], jnp.concatenate(pps, axis=1))
              + _dot(vc_ref[vrows, :], jnp.concatenate(pcs, axis=1)))
        ot = ot * (1.0 / den)
        for g in range(group):
            outs.append(ot[:, g * BLK:(g + 1) * BLK])
    out_ref[...] = jnp.concatenate(outs, axis=0).T.astype(BF16)


def _swa_attn(qr, kk, vt, sinks_row, bts, batch, seq):
    nb = seq // BLK
    cur = lambda b, n: b * nb + n
    prev = lambda b, n: b * nb + jnp.maximum(n - 1, 0)
    kblock = (B_KV_HEADS, BLK, B_HEAD_DIM)
    return pl.pallas_call(
        _swa_kernel,
        grid=(batch, nb),
        in_specs=[_resident((1, N_HEADS * BLK)),
                  pl.BlockSpec((1, B_HEAD_DIM, N_HEADS * BLK), lambda b, n: (cur(b, n), 0, 0)),
                  pl.BlockSpec(kblock, lambda b, n: (0, prev(b, n), 0)),
                  pl.BlockSpec(kblock, lambda b, n: (0, cur(b, n), 0)),
                  pl.BlockSpec((B_KV, BLK), lambda b, n: (0, prev(b, n))),
                  pl.BlockSpec((B_KV, BLK), lambda b, n: (0, cur(b, n))),
                  _resident((3, N_HEADS, BLK, BLK))],
        out_specs=pl.BlockSpec((BLK, B_Q), lambda b, n: (cur(b, n), 0)),
        out_shape=jax.ShapeDtypeStruct((batch * seq, B_Q), BF16),
        compiler_params=_params(2),
        name="swa_attn",
    )(sinks_row, qr, kk, kk, vt, vt, bts)


FF_CHUNK = 256


def _ffn_kernel(x_ref, o_ref, wo_ref, bo_ref, g_ref, wgu_ref, wd_ref, fg_ref,
                out_ref, h_ref, act_ref, *, d_ff, final):
    x1 = x_ref[...] + _dot(o_ref[...], wo_ref[...]) + bo_ref[...]
    out_ref[...] = x1
    h_ref[...] = _rms(x1, g_ref[...]).astype(BF16)
    for c in range(d_ff // FF_CHUNK):
        gate = _dot(h_ref[...], wgu_ref[:, c * FF_CHUNK:(c + 1) * FF_CHUNK])
        up = _dot(h_ref[...], wgu_ref[:, d_ff + c * FF_CHUNK:d_ff + (c + 1) * FF_CHUNK])
        act_ref[:, c * FF_CHUNK:(c + 1) * FF_CHUNK] = (
            gate * jax.nn.sigmoid(gate) * up).astype(BF16)
    y = out_ref[...] + _dot(act_ref[...], wd_ref[...])
    if final:
        y = _rms(y, fg_ref[...])
    out_ref[...] = y


def _ffn(x2, o, wo, bo, g, wgu, wd, fg, final, tm=512):
    n, d = x2.shape
    d_ff = wd.shape[0]
    row = lambda i: (i, 0)
    return pl.pallas_call(
        functools.partial(_ffn_kernel, d_ff=d_ff, final=final),
        grid=(n // tm,),
        in_specs=[pl.BlockSpec((tm, d), row),
                  pl.BlockSpec((tm, o.shape[1]), row),
                  _resident(wo.shape),
                  _resident((1, d)),
                  _resident((1, d)),
                  _resident(wgu.shape),
                  _resident(wd.shape),
                  _resident((1, d))],
        out_specs=pl.BlockSpec((tm, d), row),
        out_shape=jax.ShapeDtypeStruct((n, d), F32),
        scratch_shapes=[pltpu.VMEM((tm, d), BF16),
                        pltpu.VMEM((tm, d_ff), BF16)],
        compiler_params=_params(1),
        name="ffn",
    )(x2, o, wo, bo, g, wgu, wd, fg)


def kernel(x, rel_bias, attn_norm, ffn_norm, final_norm, a_w_in, a_latent_norm, a_w_uv, a_w_out, b_w_in, b_b_in, b_sinks, b_w_out, b_b_out, ffn_w_gate_up, ffn_w_down):
    batch, seq, d = x.shape
    x2 = x.reshape(batch * seq, d)
    bts, btd = _bias_tiles(rel_bias)
    fg = final_norm.reshape(1, d)

    wa = a_w_in[0]
    o_ckv, o_qidx, o_kidx, o_widx = A_QLAT, A_QLAT + A_LATENT, A_QLAT + A_LATENT + A_QIDX, \
        A_QLAT + A_LATENT + A_QIDX + IDX_DIM
    wt = jnp.concatenate(
        [wa[:, :o_ckv], wa[:, o_qidx:o_kidx], wa[:, o_widx:], jnp.zeros((d, 16 - IDX_HEADS), F32)],
        axis=1).T.astype(BF16)
    wn = jnp.concatenate(
        [wa[:, o_ckv:o_qidx], wa[:, o_kidx:o_widx], jnp.zeros((d, 256 - A_LATENT - IDX_DIM), F32)],
        axis=1).astype(BF16)
    qr, qir, wto, kidx, ckv, ckvt = _a_proj(
        x2, attn_norm[0].reshape(1, d), wt, wn, a_latent_norm[0].reshape(1, A_LATENT))
    wuvt = jnp.swapaxes(a_w_uv[0], 1, 2).astype(BF16)
    o = _dsa_attn(qr, qir, wto, kidx, ckv, ckvt, wuvt, btd, batch, seq)
    x2 = _ffn(x2, o, a_w_out[0].astype(BF16), jnp.zeros((1, d), F32),
              ffn_norm[0].reshape(1, d), ffn_w_gate_up[0].astype(BF16),
              ffn_w_down[0].astype(BF16), fg, final=False)

    wb, bb = b_w_in[0], b_b_in[0]
    qv_cols = np.r_[0:B_Q, B_Q + B_KV:B_Q + 2 * B_KV]
    qr, kk, vt = _b_proj(
        x2, attn_norm[1].reshape(1, d), wb[:, qv_cols].T.astype(BF16),
        bb[qv_cols].reshape(-1, 1), wb[:, B_Q:B_Q + B_KV].astype(BF16),
        bb[B_Q:B_Q + B_KV].reshape(1, B_KV))
    sinks_row = jnp.repeat(b_sinks[0], BLK).reshape(1, N_HEADS * BLK)
    o = _swa_attn(qr, kk, vt, sinks_row, bts, batch, seq)
    x2 = _ffn(x2, o, b_w_out[0].astype(BF16), b_b_out[0].reshape(1, d),
              ffn_norm[1].reshape(1, d), ffn_w_gate_up[1].astype(BF16),
              ffn_w_down[1].astype(BF16), fg, final=True)
    return x2.reshape(batch, seq, d)
```

```python
import functools
import math

import numpy as np
import jax
import jax.numpy as jnp
from jax import lax
from jax.experimental import pallas as pl
from jax.experimental.pallas import tpu as pltpu

F32 = jnp.float32
BF16 = jnp.bfloat16
I32 = jnp.int32

EPS = 1e-6
N_HEADS = 16
A_LATENT = 128
A_VHEAD = 64
IDX_HEADS = 8
IDX_DIM = 64
TOPK_MAX = 256
B_KV_HEADS = 2
B_HEAD_DIM = 64
REL_BUCKETS = 32
REL_MAX_DIST = 128

BLK = 128
SC = 256
INT_MIN = -(2 ** 31)
NEG_INF = float("-inf")
LOG2E = math.log2(math.e)
VMEM_LIMIT = 56 * 1024 * 1024


def _rms(xf, g):
    ms = jnp.mean(xf * xf, axis=-1, keepdims=True)
    return xf * lax.rsqrt(ms + EPS) * g


def _dot(a, b):
    return jnp.dot(a, b, preferred_element_type=F32)


def _dot_nt(a, b):
    return lax.dot_general(a, b, (((1,), (1,)), ((), ())), preferred_element_type=F32)


def _params(n_axes):
    return pltpu.CompilerParams(
        dimension_semantics=("arbitrary",) * n_axes, vmem_limit_bytes=VMEM_LIMIT)


def _resident(shape):
    zeros = (0,) * len(shape)
    return pl.BlockSpec(shape, lambda *_: zeros, pipeline_mode=pl.Buffered(1))


def _bucket(rel):
    n = np.maximum(rel, 0)
    max_exact = REL_BUCKETS // 2
    nf = np.maximum(n, max_exact).astype(np.float64)
    large = max_exact + (np.log(nf / max_exact) / math.log(REL_MAX_DIST / max_exact)
                         * (REL_BUCKETS - max_exact)).astype(np.int32)
    large = np.minimum(large, REL_BUCKETS - 1)
    return np.where(n < max_exact, n, large).astype(np.int32)


def _bucket_tables():
    k = np.arange(BLK)[:, None]
    q = np.arange(BLK)[None, :]
    return _bucket(np.stack([q - k, BLK + q - k]))


def _bias_kernel(rb_ref, bidx_ref, bts_ref, btd_ref):
    k = lax.broadcasted_iota(I32, (BLK, BLK), 0)
    q = lax.broadcasted_iota(I32, (BLK, BLK), 1)
    window = (k <= q, k > q)
    for h in range(N_HEADS):
        for kind in range(2):
            bidx = bidx_ref[kind]

            def body(b, t, h=h, bidx=bidx):
                return jnp.where(bidx == b, rb_ref[b, h], t)
            t = lax.fori_loop(0, REL_BUCKETS, body, jnp.zeros((BLK, BLK), F32))
            bts_ref[kind, h] = jnp.where(window[kind], t * LOG2E, NEG_INF)
            btd_ref[kind, h] = (t - rb_ref[REL_BUCKETS - 1, h]) * LOG2E
        bts_ref[2, h] = jnp.full((BLK, BLK), NEG_INF, F32)
        btd_ref[2, h] = jnp.zeros((BLK, BLK), F32)


def _bias_tiles(rel_bias):
    return pl.pallas_call(
        _bias_kernel,
        out_shape=(jax.ShapeDtypeStruct((3, N_HEADS, BLK, BLK), F32),
                   jax.ShapeDtypeStruct((3, N_HEADS, BLK, BLK), F32)),
        in_specs=[pl.BlockSpec(memory_space=pltpu.SMEM),
                  pl.BlockSpec(memory_space=pltpu.VMEM)],
        out_specs=(pl.BlockSpec(memory_space=pltpu.VMEM),
                   pl.BlockSpec(memory_space=pltpu.VMEM)),
        name="bias_tiles",
    )(rel_bias, jnp.asarray(_bucket_tables()))


A_QLAT = N_HEADS * A_LATENT
A_QIDX = IDX_HEADS * IDX_DIM
A_WT_ROWS = A_QLAT + A_QIDX + 16


def _a_proj_kernel(x_ref, g_ref, wt_ref, wn_ref, lg_ref,
                   qr_ref, qir_ref, wto_ref, kidx_ref, ckv_ref, ckvt_ref,
                   h_ref, ht_ref):
    tm = x_ref.shape[0]
    hf = _rms(x_ref[...], g_ref[...])
    h_ref[...] = hf.astype(BF16)
    ht_ref[...] = hf.T.astype(BF16)
    for c in range(A_QLAT // 512):
        r = _dot(wt_ref[c * 512:(c + 1) * 512, :], ht_ref[...]).astype(BF16)
        for hh in range(4):
            h = 4 * c + hh
            for blk in range(tm // BLK):
                qr_ref[blk, :, h * BLK:(h + 1) * BLK] = (
                    r[hh * A_LATENT:(hh + 1) * A_LATENT, blk * BLK:(blk + 1) * BLK])
    r = _dot(wt_ref[A_QLAT:A_QLAT + A_QIDX, :], ht_ref[...]).astype(BF16)
    for j in range(IDX_HEADS):
        for blk in range(tm // BLK):
            qir_ref[blk, :, j * BLK:(j + 1) * BLK] = (
                r[j * IDX_DIM:(j + 1) * IDX_DIM, blk * BLK:(blk + 1) * BLK])
    w = _dot(wt_ref[A_QLAT + A_QIDX:, :], ht_ref[...])
    wto_ref[...] = w[:IDX_HEADS] * (IDX_HEADS ** -0.5)
    t = _dot(h_ref[...], wn_ref[...])
    ckv = _rms(t[:, :A_LATENT], lg_ref[...])
    ckv_ref[...] = ckv.astype(BF16)
    kidx_ref[...] = t[:, A_LATENT:A_LATENT + IDX_DIM].astype(BF16)
    for b2 in range(tm // SC):
        ckvt_ref[b2] = ckv[b2 * SC:(b2 + 1) * SC, :].T.astype(BF16)


def _a_proj(x2, g, wt, wn, lg, tm=512):
    n, d = x2.shape
    row = lambda i: (i, 0)
    blk3 = lambda i: (i, 0, 0)
    return pl.pallas_call(
        _a_proj_kernel,
        grid=(n // tm,),
        in_specs=[pl.BlockSpec((tm, d), row),
                  _resident((1, d)),
                  _resident((A_WT_ROWS, d)),
                  _resident((d, 256)),
                  _resident((1, A_LATENT))],
        out_specs=(pl.BlockSpec((tm // BLK, A_LATENT, A_QLAT), blk3),
                   pl.BlockSpec((tm // BLK, IDX_DIM, IDX_HEADS * BLK), blk3),
                   pl.BlockSpec((IDX_HEADS, tm), lambda i: (0, i)),
                   pl.BlockSpec((tm, IDX_DIM), row),
                   pl.BlockSpec((tm, A_LATENT), row),
                   pl.BlockSpec((tm // SC, A_LATENT, SC), blk3)),
        out_shape=(jax.ShapeDtypeStruct((n // BLK, A_LATENT, A_QLAT), BF16),
                   jax.ShapeDtypeStruct((n // BLK, IDX_DIM, IDX_HEADS * BLK), BF16),
                   jax.ShapeDtypeStruct((IDX_HEADS, n), F32),
                   jax.ShapeDtypeStruct((n, IDX_DIM), BF16),
                   jax.ShapeDtypeStruct((n, A_LATENT), BF16),
                   jax.ShapeDtypeStruct((n // SC, A_LATENT, SC), BF16)),
        scratch_shapes=[pltpu.VMEM((tm, d), BF16), pltpu.VMEM((d, tm), BF16)],
        compiler_params=_params(1),
        name="a_proj",
    )(x2, g, wt, wn, lg)


def _dsa_kernel(qr_ref, qir_ref, wt_ref, kidx_ref, ckv_ref, ckvt_ref, wuvt_ref,
                btd_ref, tri_ref, out_ref,
                key_ref, madd_ref, acc_ref, pb_ref, l_ref, *, k_sel, n_sc):
    i = pl.program_id(1)
    nsc = i // 2 + 1
    kf = float(k_sel)
    krow = lax.broadcasted_iota(I32, (SC, BLK), 0)
    qpos = i * BLK + lax.broadcasted_iota(I32, (SC, BLK), 1)

    def causal(sc):
        return sc * SC + krow <= qpos

    def key_rows(ref, sc):
        return ref[pl.ds(pl.multiple_of(sc * SC, SC), SC), :]

    def colsum(x):
        return jnp.sum(x.reshape(SC // 8, 8, BLK), axis=0)

    w2 = wt_ref[...] * (IDX_DIM ** -0.5)

    def index_sc(sc, last):
        d = _dot(key_rows(kidx_ref, sc), qir_ref[0])
        s = w2[0:1, :] * jnp.maximum(d[:, 0:BLK], 0.0)
        for j in range(1, IDX_HEADS):
            s = s + w2[j:j + 1, :] * jnp.maximum(d[:, j * BLK:(j + 1) * BLK], 0.0)
        bits = lax.bitcast_convert_type(s, I32)
        key = bits ^ ((bits >> 31) & 0x7FFFFFFF)
        key = jnp.where(key == -1, 0, key)
        if last:
            key = jnp.where(causal(sc), key, INT_MIN)
        key_ref[sc] = key

    def index_body(sc, carry):
        index_sc(sc, False)
        return carry
    lax.fori_loop(0, nsc - 1, index_body, 0)
    index_sc(nsc - 1, True)

    @pl.when((i + 1) * BLK <= k_sel)
    def _():
        def body(sc, carry):
            madd_ref[sc] = jnp.where(causal(sc), 0.0, NEG_INF)
            return carry
        lax.fori_loop(0, nsc, body, 0)

    def search(nblk):
        def count_ge(cand):
            cb = jnp.broadcast_to(cand, (8, BLK))
            parts = [jnp.zeros((8, BLK), F32)] * 8
            for t in range(nblk * BLK // 8):
                sc, r = divmod(t * 8, SC)
                ge = key_ref[sc, r:r + 8, :] >= cb
                parts[t % 8] = parts[t % 8] + jnp.where(ge, 1.0, 0.0)
            part = sum(parts[1:], parts[0])
            return jnp.sum(part, axis=0, keepdims=True)

        ans = jnp.where(count_ge(jnp.zeros((1, BLK), I32)) >= kf, 0, INT_MIN)

        def bit_body(it, ans):
            cand = ans | jnp.left_shift(1, 30 - it)
            return jnp.where(count_ge(cand) >= kf, cand, ans)
        return lax.fori_loop(0, 31, bit_body, ans)

    @pl.when((i + 1) * BLK > k_sel)
    def _():
        for nblk in range(k_sel // BLK + 1, 2 * n_sc + 1):
            @pl.when(i + 1 == nblk)
            def _(nblk=nblk):
                key_ref[n_sc] = jnp.broadcast_to(search(nblk), (SC, BLK))
        thr = key_ref[n_sc][0:1, :]

        def ge_body(sc, part):
            ge = key_ref[sc] >= thr
            madd_ref[sc] = jnp.where(ge, 0.0, NEG_INF)
            return part + colsum(jnp.where(ge, 1.0, 0.0))
        n_ge = jnp.sum(lax.fori_loop(0, nsc, ge_body, jnp.zeros((8, BLK), F32)),
                       axis=0, keepdims=True)

        @pl.when(jnp.max(n_ge) > kf)
        def _():
            def gt_body(sc, part):
                return part + colsum(jnp.where(key_ref[sc] > thr, 1.0, 0.0))
            n_gt = jnp.sum(lax.fori_loop(0, nsc, gt_body, jnp.zeros((8, BLK), F32)),
                           axis=0, keepdims=True)
            need = kf - n_gt

            def mask_body(sc, off):
                key = key_ref[sc]
                eq = key == thr
                eqf = jnp.where(eq, 1.0, 0.0)
                before = _dot(tri_ref[...], eqf.astype(BF16)) + off
                sel = (key > thr) | (eq & (before < need))
                madd_ref[sc] = jnp.where(sel, 0.0, NEG_INF)
                return off + jnp.sum(eqf, axis=0, keepdims=True)
            lax.fori_loop(0, nsc, mask_body, jnp.zeros((1, BLK), F32))

    c1 = (A_LATENT ** -0.5) * LOG2E
    hcols = N_HEADS * BLK

    def logits_sc(sc, m, near, live=None):
        lt = _dot(key_rows(ckv_ref, sc), qr_ref[0])
        madd = madd_ref[sc]
        if live is not None:
            madd = madd + jnp.where(live, 0.0, NEG_INF)
        new_m = []
        for h in range(N_HEADS):
            hs = slice(h * BLK, (h + 1) * BLK)
            v = lt[:, hs] * c1 + madd
            if near:
                halves = []
                for half in range(2):
                    kind = jnp.clip(i - (2 * sc + half), 0, 2)
                    halves.append(v[half * BLK:(half + 1) * BLK] + btd_ref[kind, h])
                v = jnp.concatenate(halves, axis=0)
            l_ref[sc, :, hs] = v
            new_m.append(jnp.maximum(m[:, hs], jnp.max(v, axis=0, keepdims=True)))
        return jnp.concatenate(new_m, axis=1)

    m = jnp.full((1, hcols), NEG_INF, F32)
    m = lax.fori_loop(0, nsc - 2, lambda sc, m: logits_sc(sc, m, False), m)
    m = logits_sc(jnp.maximum(nsc - 2, 0), m, True, live=nsc >= 2)
    m = logits_sc(nsc - 1, m, True)

    def pv_body(sc, s):
        new_s = []
        for h in range(N_HEADS):
            hs = slice(h * BLK, (h + 1) * BLK)
            p = jnp.exp2(l_ref[sc, :, hs] - m[:, hs])
            new_s.append(s[:, hs] + colsum(p))
            pb_ref[:, hs] = p.astype(BF16)
        acc_ref[...] = acc_ref[...] + _dot(ckvt_ref[sc], pb_ref[...])
        return jnp.concatenate(new_s, axis=1)

    acc_ref[...] = jnp.zeros(acc_ref.shape, F32)
    s = lax.fori_loop(0, nsc, pv_body, jnp.zeros((8, hcols), F32))
    inv = 1.0 / jnp.sum(s, axis=0, keepdims=True)

    outs = []
    for h in range(N_HEADS):
        hs = slice(h * BLK, (h + 1) * BLK)
        o = (acc_ref[:, hs] * inv[:, hs]).astype(BF16)
        outs.append(_dot(wuvt_ref[h], o))
    out_ref[...] = jnp.concatenate(outs, axis=0).T.astype(BF16)


def _dsa_attn(qr, qir, wt, kidx, ckv, ckvt, wuvt, btd, batch, seq):
    nblk = seq // BLK
    n_sc = seq // SC
    k_sel = min(TOPK_MAX, seq // 4)
    hcols = N_HEADS * BLK
    tri = jnp.asarray(np.tril(np.ones((SC, SC), np.float32), -1), BF16)
    return pl.pallas_call(
        functools.partial(_dsa_kernel, k_sel=k_sel, n_sc=n_sc),
        grid=(batch, nblk),
        in_specs=[pl.BlockSpec((1, A_LATENT, hcols), lambda b, i: (b * nblk + i, 0, 0)),
                  pl.BlockSpec((1, IDX_DIM, IDX_HEADS * BLK), lambda b, i: (b * nblk + i, 0, 0)),
                  pl.BlockSpec((IDX_HEADS, BLK), lambda b, i: (0, b * nblk + i)),
                  pl.BlockSpec((seq, IDX_DIM), lambda b, i: (b, 0)),
                  pl.BlockSpec((seq, A_LATENT), lambda b, i: (b, 0)),
                  pl.BlockSpec((n_sc, A_LATENT, SC), lambda b, i: (b, 0, 0)),
                  _resident((N_HEADS, A_VHEAD, A_LATENT)),
                  _resident((3, N_HEADS, BLK, BLK)),
                  _resident((SC, SC))],
        out_specs=pl.BlockSpec((BLK, N_HEADS * A_VHEAD), lambda b, i: (b * nblk + i, 0)),
        out_shape=jax.ShapeDtypeStruct((batch * seq, N_HEADS * A_VHEAD), BF16),
        scratch_shapes=[
            pltpu.VMEM((n_sc + 1, SC, BLK), I32),
            pltpu.VMEM((n_sc, SC, BLK), F32),
            pltpu.VMEM((A_LATENT, hcols), F32),
            pltpu.VMEM((SC, hcols), BF16),
            pltpu.VMEM((n_sc, SC, hcols), F32),
        ],
        compiler_params=_params(2),
        name="dsa_attn",
    )(qr, qir, wt, kidx, ckv, ckvt, wuvt, btd, tri)


B_Q = N_HEADS * B_HEAD_DIM
B_KV = B_KV_HEADS * B_HEAD_DIM


def _b_proj_kernel(x_ref, g_ref, wt_ref, bcol_ref, wk_ref, bk_ref,
                   qr_ref, kk_ref, vt_ref, h_ref, ht_ref):
    tm = x_ref.shape[0]
    hf = _rms(x_ref[...], g_ref[...])
    h_ref[...] = hf.astype(BF16)
    ht_ref[...] = hf.T.astype(BF16)
    heads_per_dot = 512 // B_HEAD_DIM
    for c in range(B_Q // 512):
        rows = slice(c * 512, (c + 1) * 512)
        r = (_dot(wt_ref[rows, :], ht_ref[...]) + bcol_ref[rows, :]).astype(BF16)
        for hh in range(heads_per_dot):
            h = c * heads_per_dot + hh
            for blk in range(tm // BLK):
                qr_ref[blk, :, h * BLK:(h + 1) * BLK] = (
                    r[hh * B_HEAD_DIM:(hh + 1) * B_HEAD_DIM, blk * BLK:(blk + 1) * BLK])
    vt_ref[...] = (_dot(wt_ref[B_Q:, :], ht_ref[...]) + bcol_ref[B_Q:, :]).astype(BF16)
    t = _dot(h_ref[...], wk_ref[...]) + bk_ref[...]
    for kv in range(B_KV_HEADS):
        kk_ref[kv] = t[:, kv * B_HEAD_DIM:(kv + 1) * B_HEAD_DIM].astype(BF16)


def _b_proj(x2, g, wt, bcol, wk, bk, tm=512):
    n, d = x2.shape
    return pl.pallas_call(
        _b_proj_kernel,
        grid=(n // tm,),
        in_specs=[pl.BlockSpec((tm, d), lambda i: (i, 0)),
                  _resident((1, d)),
                  _resident((B_Q + B_KV, d)),
                  _resident((B_Q + B_KV, 1)),
                  _resident((d, B_KV)),
                  _resident((1, B_KV))],
        out_specs=(pl.BlockSpec((tm // BLK, B_HEAD_DIM, N_HEADS * BLK), lambda i: (i, 0, 0)),
                   pl.BlockSpec((B_KV_HEADS, tm, B_HEAD_DIM), lambda i: (0, i, 0)),
                   pl.BlockSpec((B_KV, tm), lambda i: (0, i))),
        out_shape=(jax.ShapeDtypeStruct((n // BLK, B_HEAD_DIM, N_HEADS * BLK), BF16),
                   jax.ShapeDtypeStruct((B_KV_HEADS, n, B_HEAD_DIM), BF16),
                   jax.ShapeDtypeStruct((B_KV, n), BF16)),
        scratch_shapes=[pltpu.VMEM((tm, d), BF16), pltpu.VMEM((d, tm), BF16)],
        compiler_params=_params(1),
        name="b_proj",
    )(x2, g, wt, bcol, wk, bk)


def _swa_kernel(sink_ref, qr_ref, kp_ref, kc_ref, vp_ref, vc_ref, bts_ref, out_ref):
    n = pl.program_id(1)
    prev_kind = jnp.where(n >= 1, 1, 2)
    c1 = (B_HEAD_DIM ** -0.5) * LOG2E
    group = N_HEADS // B_KV_HEADS
    outs = []
    for kv in range(B_KV_HEADS):
        cols = slice(kv * group * BLK, (kv + 1) * group * BLK)
        q = qr_ref[0, :, cols]
        lp = _dot(kp_ref[kv], q)
        lc = _dot(kc_ref[kv], q)
        lps, lcs, ms = [], [], []
        for g in range(group):
            h = kv * group + g
            hs = slice(g * BLK, (g + 1) * BLK)
            a = lp[:, hs] * c1 + bts_ref[prev_kind, h]
            b = lc[:, hs] * c1 + bts_ref[0, h]
            lps.append(a)
            lcs.append(b)
            ms.append(jnp.max(jnp.maximum(a, b), axis=0, keepdims=True))
        sink = sink_ref[:, cols] * LOG2E
        m = jnp.maximum(jnp.concatenate(ms, axis=1), sink)
        pps, pcs, dens = [], [], []
        for g in range(group):
            hs = slice(g * BLK, (g + 1) * BLK)
            pp = jnp.exp2(lps[g] - m[:, hs])
            pc = jnp.exp2(lcs[g] - m[:, hs])
            dens.append(jnp.sum(pp + pc, axis=0, keepdims=True))
            pps.append(pp.astype(BF16))
            pcs.append(pc.astype(BF16))
        den = jnp.concatenate(dens, axis=1) + jnp.exp2(sink - m)
        vrows = slice(kv * B_HEAD_DIM, (kv + 1) * B_HEAD_DIM)
        ot = (_dot(vp_ref[vrows, :], jnp.concatenate(pps, axis=1))
              + _dot(vc_ref[vrows, :], jnp.concatenate(pcs, axis=1)))
        ot = ot * (1.0 / den)
        for g in range(group):
            outs.append(ot[:, g * BLK:(g + 1) * BLK])
    out_ref[...] = jnp.concatenate(outs, axis=0).T.astype(BF16)


def _swa_attn(qr, kk, vt, sinks_row, bts, batch, seq):
    nb = seq // BLK
    cur = lambda b, n: b * nb + n
    prev = lambda b, n: b * nb + jnp.maximum(n - 1, 0)
    kblock = (B_KV_HEADS, BLK, B_HEAD_DIM)
    return pl.pallas_call(
        _swa_kernel,
        grid=(batch, nb),
        in_specs=[_resident((1, N_HEADS * BLK)),
                  pl.BlockSpec((1, B_HEAD_DIM, N_HEADS * BLK), lambda b, n: (cur(b, n), 0, 0)),
                  pl.BlockSpec(kblock, lambda b, n: (0, prev(b, n), 0)),
                  pl.BlockSpec(kblock, lambda b, n: (0, cur(b, n), 0)),
                  pl.BlockSpec((B_KV, BLK), lambda b, n: (0, prev(b, n))),
                  pl.BlockSpec((B_KV, BLK), lambda b, n: (0, cur(b, n))),
                  _resident((3, N_HEADS, BLK, BLK))],
        out_specs=pl.BlockSpec((BLK, B_Q), lambda b, n: (cur(b, n), 0)),
        out_shape=jax.ShapeDtypeStruct((batch * seq, B_Q), BF16),
        compiler_params=_params(2),
        name="swa_attn",
    )(sinks_row, qr, kk, kk, vt, vt, bts)


FF_CHUNK = 256


def _ffn_kernel(x_ref, o_ref, wo_ref, bo_ref, g_ref, wgu_ref, wd_ref, fg_ref,
                out_ref, h_ref, act_ref, *, d_ff, final):
    x1 = x_ref[...] + _dot(o_ref[...], wo_ref[...]) + bo_ref[...]
    out_ref[...] = x1
    h_ref[...] = _rms(x1, g_ref[...]).astype(BF16)
    for c in range(d_ff // FF_CHUNK):
        gate = _dot(h_ref[...], wgu_ref[:, c * FF_CHUNK:(c + 1) * FF_CHUNK])
        up = _dot(h_ref[...], wgu_ref[:, d_ff + c * FF_CHUNK:d_ff + (c + 1) * FF_CHUNK])
        act_ref[:, c * FF_CHUNK:(c + 1) * FF_CHUNK] = (
            gate * jax.nn.sigmoid(gate) * up).astype(BF16)
    y = out_ref[...] + _dot(act_ref[...], wd_ref[...])
    if final:
        y = _rms(y, fg_ref[...])
    out_ref[...] = y


def _ffn(x2, o, wo, bo, g, wgu, wd, fg, final, tm=512):
    n, d = x2.shape
    d_ff = wd.shape[0]
    row = lambda i: (i, 0)
    return pl.pallas_call(
        functools.partial(_ffn_kernel, d_ff=d_ff, final=final),
        grid=(n // tm,),
        in_specs=[pl.BlockSpec((tm, d), row),
                  pl.BlockSpec((tm, o.shape[1]), row),
                  _resident(wo.shape),
                  _resident((1, d)),
                  _resident((1, d)),
                  _resident(wgu.shape),
                  _resident(wd.shape),
                  _resident((1, d))],
        out_specs=pl.BlockSpec((tm, d), row),
        out_shape=jax.ShapeDtypeStruct((n, d), F32),
        scratch_shapes=[pltpu.VMEM((tm, d), BF16),
                        pltpu.VMEM((tm, d_ff), BF16)],
        compiler_params=_params(1),
        name="ffn",
    )(x2, o, wo, bo, g, wgu, wd, fg)


def kernel(x, rel_bias, attn_norm, ffn_norm, final_norm, a_w_in, a_latent_norm, a_w_uv, a_w_out, b_w_in, b_b_in, b_sinks, b_w_out, b_b_out, ffn_w_gate_up, ffn_w_down):
    batch, seq, d = x.shape
    x2 = x.reshape(batch * seq, d)
    bts, btd = _bias_tiles(rel_bias)
    fg = final_norm.reshape(1, d)

    wa = a_w_in[0]
    o_ckv, o_qidx, o_kidx, o_widx = A_QLAT, A_QLAT + A_LATENT, A_QLAT + A_LATENT + A_QIDX, \
        A_QLAT + A_LATENT + A_QIDX + IDX_DIM
    wt = jnp.concatenate(
        [wa[:, :o_ckv], wa[:, o_qidx:o_kidx], wa[:, o_widx:], jnp.zeros((d, 16 - IDX_HEADS), F32)],
        axis=1).T.astype(BF16)
    wn = jnp.concatenate(
        [wa[:, o_ckv:o_qidx], wa[:, o_kidx:o_widx], jnp.zeros((d, 256 - A_LATENT - IDX_DIM), F32)],
        axis=1).astype(BF16)
    qr, qir, wto, kidx, ckv, ckvt = _a_proj(
        x2, attn_norm[0].reshape(1, d), wt, wn, a_latent_norm[0].reshape(1, A_LATENT))
    wuvt = jnp.swapaxes(a_w_uv[0], 1, 2).astype(BF16)
    o = _dsa_attn(qr, qir, wto, kidx, ckv, ckvt, wuvt, btd, batch, seq)
    x2 = _ffn(x2, o, a_w_out[0].astype(BF16), jnp.zeros((1, d), F32),
              ffn_norm[0].reshape(1, d), ffn_w_gate_up[0].astype(BF16),
              ffn_w_down[0].astype(BF16), fg, final=False)

    wb, bb = b_w_in[0], b_b_in[0]
    qv_cols = np.r_[0:B_Q, B_Q + B_KV:B_Q + 2 * B_KV]
    qr, kk, vt = _b_proj(
        x2, attn_norm[1].reshape(1, d), wb[:, qv_cols].T.astype(BF16),
        bb[qv_cols].reshape(-1, 1), wb[:, B_Q:B_Q + B_KV].astype(BF16),
        bb[B_Q:B_Q + B_KV].reshape(1, B_KV))
    sinks_row = jnp.repeat(b_sinks[0], BLK).reshape(1, N_HEADS * BLK)
    o = _swa_attn(qr, kk, vt, sinks_row, bts, batch, seq)
    x2 = _ffn(x2, o, b_w_out[0].astype(BF16), b_b_out[0].reshape(1, d),
              ffn_norm[1].reshape(1, d), ffn_w_gate_up[1].astype(BF16),
              ffn_w_down[1].astype(BF16), fg, final=True)
    return x2.reshape(batch, seq, d)
```

```python
import functools
import math

import numpy as np
import jax
import jax.numpy as jnp
from jax import lax
from jax.experimental import pallas as pl
from jax.experimental.pallas import tpu as pltpu

F32 = jnp.float32
BF16 = jnp.bfloat16
I32 = jnp.int32

EPS = 1e-6
N_HEADS = 16
A_LATENT = 128
A_VHEAD = 64
IDX_HEADS = 8
IDX_DIM = 64
TOPK_MAX = 256
B_KV_HEADS = 2
B_HEAD_DIM = 64
REL_BUCKETS = 32
REL_MAX_DIST = 128

BLK = 128
SC = 256
INT_MIN = -(2 ** 31)
NEG_INF = float("-inf")
LOG2E = math.log2(math.e)
VMEM_LIMIT = 56 * 1024 * 1024


def _rms(xf, g):
    ms = jnp.mean(xf * xf, axis=-1, keepdims=True)
    return xf * lax.rsqrt(ms + EPS) * g


def _dot(a, b):
    return jnp.dot(a, b, preferred_element_type=F32)


def _dot_nt(a, b):
    return lax.dot_general(a, b, (((1,), (1,)), ((), ())), preferred_element_type=F32)


def _params(n_axes):
    return pltpu.CompilerParams(
        dimension_semantics=("arbitrary",) * n_axes, vmem_limit_bytes=VMEM_LIMIT)


def _resident(shape):
    zeros = (0,) * len(shape)
    return pl.BlockSpec(shape, lambda *_: zeros, pipeline_mode=pl.Buffered(1))


def _bucket(rel):
    n = np.maximum(rel, 0)
    max_exact = REL_BUCKETS // 2
    nf = np.maximum(n, max_exact).astype(np.float64)
    large = max_exact + (np.log(nf / max_exact) / math.log(REL_MAX_DIST / max_exact)
                         * (REL_BUCKETS - max_exact)).astype(np.int32)
    large = np.minimum(large, REL_BUCKETS - 1)
    return np.where(n < max_exact, n, large).astype(np.int32)


def _bucket_tables():
    k = np.arange(BLK)[:, None]
    q = np.arange(BLK)[None, :]
    return _bucket(np.stack([q - k, BLK + q - k]))


def _bias_kernel(rb_ref, bidx_ref, bts_ref, btd_ref):
    k = lax.broadcasted_iota(I32, (BLK, BLK), 0)
    q = lax.broadcasted_iota(I32, (BLK, BLK), 1)
    window = (k <= q, k > q)
    for h in range(N_HEADS):
        for kind in range(2):
            bidx = bidx_ref[kind]

            def body(b, t, h=h, bidx=bidx):
                return jnp.where(bidx == b, rb_ref[b, h], t)
            t = lax.fori_loop(0, REL_BUCKETS, body, jnp.zeros((BLK, BLK), F32))
            bts_ref[kind, h] = jnp.where(window[kind], t * LOG2E, NEG_INF)
            btd_ref[kind, h] = (t - rb_ref[REL_BUCKETS - 1, h]) * LOG2E
        bts_ref[2, h] = jnp.full((BLK, BLK), NEG_INF, F32)
        btd_ref[2, h] = jnp.zeros((BLK, BLK), F32)


def _bias_tiles(rel_bias):
    return pl.pallas_call(
        _bias_kernel,
        out_shape=(jax.ShapeDtypeStruct((3, N_HEADS, BLK, BLK), F32),
                   jax.ShapeDtypeStruct((3, N_HEADS, BLK, BLK), F32)),
        in_specs=[pl.BlockSpec(memory_space=pltpu.SMEM),
                  pl.BlockSpec(memory_space=pltpu.VMEM)],
        out_specs=(pl.BlockSpec(memory_space=pltpu.VMEM),
                   pl.BlockSpec(memory_space=pltpu.VMEM)),
        name="bias_tiles",
    )(rel_bias, jnp.asarray(_bucket_tables()))


A_QLAT = N_HEADS * A_LATENT
A_QIDX = IDX_HEADS * IDX_DIM
A_WT_ROWS = A_QLAT + A_QIDX + 16


def _a_proj_kernel(x_ref, g_ref, wt_ref, wn_ref, lg_ref,
                   qr_ref, qir_ref, wto_ref, kidx_ref, ckv_ref, ckvt_ref,
                   h_ref, ht_ref):
    tm = x_ref.shape[0]
    hf = _rms(x_ref[...], g_ref[...])
    h_ref[...] = hf.astype(BF16)
    ht_ref[...] = hf.T.astype(BF16)
    for c in range(A_QLAT // 512):
        r = _dot(wt_ref[c * 512:(c + 1) * 512, :], ht_ref[...]).astype(BF16)
        for hh in range(4):
            h = 4 * c + hh
            for blk in range(tm // BLK):
                qr_ref[blk, :, h * BLK:(h + 1) * BLK] = (
                    r[hh * A_LATENT:(hh + 1) * A_LATENT, blk * BLK:(blk + 1) * BLK])
    r = _dot(wt_ref[A_QLAT:A_QLAT + A_QIDX, :], ht_ref[...]).astype(BF16)
    for j in range(IDX_HEADS):
        for blk in range(tm // BLK):
            qir_ref[blk, :, j * BLK:(j + 1) * BLK] = (
                r[j * IDX_DIM:(j + 1) * IDX_DIM, blk * BLK:(blk + 1) * BLK])
    w = _dot(wt_ref[A_QLAT + A_QIDX:, :], ht_ref[...])
    wto_ref[...] = w[:IDX_HEADS] * (IDX_HEADS ** -0.5)
    t = _dot(h_ref[...], wn_ref[...])
    ckv = _rms(t[:, :A_LATENT], lg_ref[...])
    ckv_ref[...] = ckv.astype(BF16)
    kidx_ref[...] = t[:, A_LATENT:A_LATENT + IDX_DIM].astype(BF16)
    for b2 in range(tm // SC):
        ckvt_ref[b2] = ckv[b2 * SC:(b2 + 1) * SC, :].T.astype(BF16)


def _a_proj(x2, g, wt, wn, lg, tm=512):
    n, d = x2.shape
    row = lambda i: (i, 0)
    blk3 = lambda i: (i, 0, 0)
    return pl.pallas_call(
        _a_proj_kernel,
        grid=(n // tm,),
        in_specs=[pl.BlockSpec((tm, d), row),
                  _resident((1, d)),
                  _resident((A_WT_ROWS, d)),
                  _resident((d, 256)),
                  _resident((1, A_LATENT))],
        out_specs=(pl.BlockSpec((tm // BLK, A_LATENT, A_QLAT), blk3),
                   pl.BlockSpec((tm // BLK, IDX_DIM, IDX_HEADS * BLK), blk3),
                   pl.BlockSpec((IDX_HEADS, tm), lambda i: (0, i)),
                   pl.BlockSpec((tm, IDX_DIM), row),
                   pl.BlockSpec((tm, A_LATENT), row),
                   pl.BlockSpec((tm // SC, A_LATENT, SC), blk3)),
        out_shape=(jax.ShapeDtypeStruct((n // BLK, A_LATENT, A_QLAT), BF16),
                   jax.ShapeDtypeStruct((n // BLK, IDX_DIM, IDX_HEADS * BLK), BF16),
                   jax.ShapeDtypeStruct((IDX_HEADS, n), F32),
                   jax.ShapeDtypeStruct((n, IDX_DIM), BF16),
                   jax.ShapeDtypeStruct((n, A_LATENT), BF16),
                   jax.ShapeDtypeStruct((n // SC, A_LATENT, SC), BF16)),
        scratch_shapes=[pltpu.VMEM((tm, d), BF16), pltpu.VMEM((d, tm), BF16)],
        compiler_params=_params(1),
        name="a_proj",
    )(x2, g, wt, wn, lg)


def _dsa_kernel(qr_ref, qir_ref, wt_ref, kidx_ref, ckv_ref, ckvt_ref, wuvt_ref,
                btd_ref, tri_ref, out_ref,
                key_ref, madd_ref, acc_ref, pb_ref, l_ref, *, k_sel, n_sc):
    i = pl.program_id(1)
    nsc = i // 2 + 1
    kf = float(k_sel)
    krow = lax.broadcasted_iota(I32, (SC, BLK), 0)
    qpos = i * BLK + lax.broadcasted_iota(I32, (SC, BLK), 1)

    def causal(sc):
        return sc * SC + krow <= qpos

    def key_rows(ref, sc, count=1):
        return ref[pl.ds(pl.multiple_of(sc * SC, SC), count * SC), :]

    def colsum(x):
        return jnp.sum(x.reshape(SC // 8, 8, BLK), axis=0)

    w2 = wt_ref[...] * (IDX_DIM ** -0.5)

    def index_group(sc0, count, last=False):
        d = _dot(key_rows(kidx_ref, sc0, count), qir_ref[0])
        for c in range(count):
            rows = slice(c * SC, (c + 1) * SC)
            s = w2[0:1, :] * jnp.maximum(d[rows, 0:BLK], 0.0)
            for j in range(1, IDX_HEADS):
                s = s + w2[j:j + 1, :] * jnp.maximum(d[rows, j * BLK:(j + 1) * BLK], 0.0)
            bits = lax.bitcast_convert_type(s, I32)
            key = bits ^ ((bits >> 31) & 0x7FFFFFFF)
            key = jnp.where(key == -1, 0, key)
            if last:
                key = jnp.where(causal(sc0 + c), key, INT_MIN)
            key_ref[sc0 + c] = key

    def index_body(t, carry):
        index_group(2 * t, 2)
        return carry
    lax.fori_loop(0, (nsc - 1) // 2, index_body, 0)

    @pl.when((nsc - 1) % 2 == 1)
    def _():
        index_group(nsc - 2, 1)
    index_group(nsc - 1, 1, last=True)

    @pl.when((i + 1) * BLK <= k_sel)
    def _():
        def body(sc, carry):
            madd_ref[sc] = jnp.where(causal(sc), 0.0, NEG_INF)
            return carry
        lax.fori_loop(0, nsc, body, 0)

    def search(nblk):
        def count_ge(cand):
            cb = jnp.broadcast_to(cand, (8, BLK))
            parts = [jnp.zeros((8, BLK), F32)] * 8
            for t in range(nblk * BLK // 8):
                sc, r = divmod(t * 8, SC)
                ge = key_ref[sc, r:r + 8, :] >= cb
                parts[t % 8] = parts[t % 8] + jnp.where(ge, 1.0, 0.0)
            part = sum(parts[1:], parts[0])
            return jnp.sum(part, axis=0, keepdims=True)

        ans = jnp.where(count_ge(jnp.zeros((1, BLK), I32)) >= kf, 0, INT_MIN)

        def bit_body(it, ans):
            cand = ans | jnp.left_shift(1, 30 - it)
            return jnp.where(count_ge(cand) >= kf, cand, ans)
        return lax.fori_loop(0, 31, bit_body, ans)

    @pl.when((i + 1) * BLK > k_sel)
    def _():
        for nblk in range(k_sel // BLK + 1, 2 * n_sc + 1):
            @pl.when(i + 1 == nblk)
            def _(nblk=nblk):
                key_ref[n_sc] = jnp.broadcast_to(search(nblk), (SC, BLK))
        thr = key_ref[n_sc][0:1, :]

        def ge_body(sc, part):
            ge = key_ref[sc] >= thr
            madd_ref[sc] = jnp.where(ge, 0.0, NEG_INF)
            return part + colsum(jnp.where(ge, 1.0, 0.0))
        n_ge = jnp.sum(lax.fori_loop(0, nsc, ge_body, jnp.zeros((8, BLK), F32)),
                       axis=0, keepdims=True)

        @pl.when(jnp.max(n_ge) > kf)
        def _():
            def gt_body(sc, part):
                return part + colsum(jnp.where(key_ref[sc] > thr, 1.0, 0.0))
            n_gt = jnp.sum(lax.fori_loop(0, nsc, gt_body, jnp.zeros((8, BLK), F32)),
                           axis=0, keepdims=True)
            need = kf - n_gt

            def mask_body(sc, off):
                key = key_ref[sc]
                eq = key == thr
                eqf = jnp.where(eq, 1.0, 0.0)
                before = _dot(tri_ref[...], eqf.astype(BF16)) + off
                sel = (key > thr) | (eq & (before < need))
                madd_ref[sc] = jnp.where(sel, 0.0, NEG_INF)
                return off + jnp.sum(eqf, axis=0, keepdims=True)
            lax.fori_loop(0, nsc, mask_body, jnp.zeros((1, BLK), F32))

    c1 = (A_LATENT ** -0.5) * LOG2E
    hcols = N_HEADS * BLK

    def logits_group(sc0, m, count, near):
        lt = _dot(key_rows(ckv_ref, sc0, count), qr_ref[0])
        madds = [madd_ref[sc0 + c] for c in range(count)]
        new_m = []
        for h in range(N_HEADS):
            hs = slice(h * BLK, (h + 1) * BLK)
            mh = m[:, hs]
            for c in range(count):
                v = lt[c * SC:(c + 1) * SC, hs] * c1 + madds[c]
                if near:
                    halves = []
                    for half in range(2):
                        kind = jnp.clip(i - (2 * (sc0 + c) + half), 0, 2)
                        halves.append(v[half * BLK:(half + 1) * BLK] + btd_ref[kind, h])
                    v = jnp.concatenate(halves, axis=0)
                l_ref[sc0 + c, h] = v
                mh = jnp.maximum(mh, jnp.max(v, axis=0, keepdims=True))
            new_m.append(mh)
        return jnp.concatenate(new_m, axis=1)

    n_far = nsc - 2
    m = jnp.full((1, hcols), NEG_INF, F32)
    m = lax.fori_loop(0, n_far // 2, lambda t, m: logits_group(2 * t, m, 2, False), m)
    m = lax.cond((n_far > 0) & (n_far % 2 == 1),
                 lambda m: logits_group(n_far - 1, m, 1, False), lambda m: m, m)
    m = lax.cond(nsc >= 2, lambda m: logits_group(nsc - 2, m, 2, True),
                 lambda m: logits_group(0, m, 1, True), m)

    def pv_group(sc0, s, count):
        vt = jnp.concatenate([ckvt_ref[sc0 + c] for c in range(count)], axis=1)
        new_s = []
        for hp in range(N_HEADS // 2):
            cols = []
            for h in (2 * hp, 2 * hp + 1):
                hs = slice(h * BLK, (h + 1) * BLK)
                sh = s[:, hs]
                ps = []
                for c in range(count):
                    p = jnp.exp2(l_ref[sc0 + c, h] - m[:, hs])
                    sh = sh + colsum(p)
                    ps.append(p.astype(BF16))
                new_s.append(sh)
                cols.append(jnp.concatenate(ps, axis=0))
            pair = jnp.concatenate(cols, axis=1)
            acc_ref[hp] = acc_ref[hp] + _dot(vt, pair)
        return jnp.concatenate(new_s, axis=1)

    acc_ref[...] = jnp.zeros(acc_ref.shape, F32)
    s = lax.fori_loop(0, nsc // 2, lambda t, s: pv_group(2 * t, s, 2),
                      jnp.zeros((8, hcols), F32))
    s = lax.cond(nsc % 2 == 1, lambda s: pv_group(nsc - 1, s, 1), lambda s: s, s)
    inv = 1.0 / jnp.sum(s, axis=0, keepdims=True)

    outs = []
    for h in range(N_HEADS):
        hs = slice(h * BLK, (h + 1) * BLK)
        a = acc_ref[h // 2, :, (h % 2) * BLK:(h % 2 + 1) * BLK]
        o = (a * inv[:, hs]).astype(BF16)
        outs.append(_dot(wuvt_ref[h], o))
    out_ref[...] = jnp.concatenate(outs, axis=0).T.astype(BF16)


def _dsa_attn(qr, qir, wt, kidx, ckv, ckvt, wuvt, btd, batch, seq):
    nblk = seq // BLK
    n_sc = seq // SC
    k_sel = min(TOPK_MAX, seq // 4)
    hcols = N_HEADS * BLK
    tri = jnp.asarray(np.tril(np.ones((SC, SC), np.float32), -1), BF16)
    return pl.pallas_call(
        functools.partial(_dsa_kernel, k_sel=k_sel, n_sc=n_sc),
        grid=(batch, nblk),
        in_specs=[pl.BlockSpec((1, A_LATENT, hcols), lambda b, i: (b * nblk + i, 0, 0)),
                  pl.BlockSpec((1, IDX_DIM, IDX_HEADS * BLK), lambda b, i: (b * nblk + i, 0, 0)),
                  pl.BlockSpec((IDX_HEADS, BLK), lambda b, i: (0, b * nblk + i)),
                  pl.BlockSpec((seq, IDX_DIM), lambda b, i: (b, 0)),
                  pl.BlockSpec((seq, A_LATENT), lambda b, i: (b, 0)),
                  pl.BlockSpec((n_sc, A_LATENT, SC), lambda b, i: (b, 0, 0)),
                  _resident((N_HEADS, A_VHEAD, A_LATENT)),
                  _resident((3, N_HEADS, BLK, BLK)),
                  _resident((SC, SC))],
        out_specs=pl.BlockSpec((BLK, N_HEADS * A_VHEAD), lambda b, i: (b * nblk + i, 0)),
        out_shape=jax.ShapeDtypeStruct((batch * seq, N_HEADS * A_VHEAD), BF16),
        scratch_shapes=[
            pltpu.VMEM((n_sc + 1, SC, BLK), I32),
            pltpu.VMEM((n_sc, SC, BLK), F32),
            pltpu.VMEM((N_HEADS // 2, A_LATENT, 2 * BLK), F32),
            pltpu.VMEM((N_HEADS // 2, 2 * SC, 2 * BLK), BF16),
            pltpu.VMEM((n_sc, N_HEADS, SC, BLK), F32),
        ],
        compiler_params=_params(2),
        name="dsa_attn",
    )(qr, qir, wt, kidx, ckv, ckvt, wuvt, btd, tri)


B_Q = N_HEADS * B_HEAD_DIM
B_KV = B_KV_HEADS * B_HEAD_DIM


def _b_proj_kernel(x_ref, g_ref, wt_ref, bcol_ref, wk_ref, bk_ref,
                   qr_ref, kk_ref, vt_ref, h_ref, ht_ref):
    tm = x_ref.shape[0]
    hf = _rms(x_ref[...], g_ref[...])
    h_ref[...] = hf.astype(BF16)
    ht_ref[...] = hf.T.astype(BF16)
    heads_per_dot = 512 // B_HEAD_DIM
    for c in range(B_Q // 512):
        rows = slice(c * 512, (c + 1) * 512)
        r = (_dot(wt_ref[rows, :], ht_ref[...]) + bcol_ref[rows, :]).astype(BF16)
        for hh in range(heads_per_dot):
            h = c * heads_per_dot + hh
            for blk in range(tm // BLK):
                qr_ref[blk, :, h * BLK:(h + 1) * BLK] = (
                    r[hh * B_HEAD_DIM:(hh + 1) * B_HEAD_DIM, blk * BLK:(blk + 1) * BLK])
    vt_ref[...] = (_dot(wt_ref[B_Q:, :], ht_ref[...]) + bcol_ref[B_Q:, :]).astype(BF16)
    t = _dot(h_ref[...], wk_ref[...]) + bk_ref[...]
    for kv in range(B_KV_HEADS):
        kk_ref[kv] = t[:, kv * B_HEAD_DIM:(kv + 1) * B_HEAD_DIM].astype(BF16)


def _b_proj(x2, g, wt, bcol, wk, bk, tm=512):
    n, d = x2.shape
    return pl.pallas_call(
        _b_proj_kernel,
        grid=(n // tm,),
        in_specs=[pl.BlockSpec((tm, d), lambda i: (i, 0)),
                  _resident((1, d)),
                  _resident((B_Q + B_KV, d)),
                  _resident((B_Q + B_KV, 1)),
                  _resident((d, B_KV)),
                  _resident((1, B_KV))],
        out_specs=(pl.BlockSpec((tm // BLK, B_HEAD_DIM, N_HEADS * BLK), lambda i: (i, 0, 0)),
                   pl.BlockSpec((B_KV_HEADS, tm, B_HEAD_DIM), lambda i: (0, i, 0)),
                   pl.BlockSpec((B_KV, tm), lambda i: (0, i))),
        out_shape=(jax.ShapeDtypeStruct((n // BLK, B_HEAD_DIM, N_HEADS * BLK), BF16),
                   jax.ShapeDtypeStruct((B_KV_HEADS, n, B_HEAD_DIM), BF16),
                   jax.ShapeDtypeStruct((B_KV, n), BF16)),
        scratch_shapes=[pltpu.VMEM((tm, d), BF16), pltpu.VMEM((d, tm), BF16)],
        compiler_params=_params(1),
        name="b_proj",
    )(x2, g, wt, bcol, wk, bk)


def _swa_kernel(sink_ref, qr_ref, kp_ref, kc_ref, vp_ref, vc_ref, bts_ref, out_ref):
    n = pl.program_id(1)
    prev_kind = jnp.where(n >= 1, 1, 2)
    c1 = (B_HEAD_DIM ** -0.5) * LOG2E
    group = N_HEADS // B_KV_HEADS
    outs = []
    for kv in range(B_KV_HEADS):
        cols = slice(kv * group * BLK, (kv + 1) * group * BLK)
        q = qr_ref[0, :, cols]
        lp = _dot(kp_ref[kv], q)
        lc = _dot(kc_ref[kv], q)
        lps, lcs, ms = [], [], []
        for g in range(group):
            h = kv * group + g
            hs = slice(g * BLK, (g + 1) * BLK)
            a = lp[:, hs] * c1 + bts_ref[prev_kind, h]
            b = lc[:, hs] * c1 + bts_ref[0, h]
            lps.append(a)
            lcs.append(b)
            ms.append(jnp.max(jnp.maximum(a, b), axis=0, keepdims=True))
        sink = sink_ref[:, cols] * LOG2E
        m = jnp.maximum(jnp.concatenate(ms, axis=1), sink)
        pps, pcs, dens = [], [], []
        for g in range(group):
            hs = slice(g * BLK, (g + 1) * BLK)
            pp = jnp.exp2(lps[g] - m[:, hs])
            pc = jnp.exp2(lcs[g] - m[:, hs])
            dens.append(jnp.sum(pp + pc, axis=0, keepdims=True))
            pps.append(pp.astype(BF16))
            pcs.append(pc.astype(BF16))
        den = jnp.concatenate(dens, axis=1) + jnp.exp2(sink - m)
        vrows = slice(kv * B_HEAD_DIM, (kv + 1) * B_HEAD_DIM)
        ot = (_dot(vp_ref[vrows, :], jnp.concatenate(pps, axis=1))
              + _dot(vc_ref[vrows, :], jnp.concatenate(pcs, axis=1)))
        ot = ot * (1.0 / den)
        for g in range(group):
            outs.append(ot[:, g * BLK:(g + 1) * BLK])
    out_ref[...] = jnp.concatenate(outs, axis=0).T.astype(BF16)


def _swa_attn(qr, kk, vt, sinks_row, bts, batch, seq):
    nb = seq // BLK
    cur = lambda b, n: b * nb + n
    prev = lambda b, n: b * nb + jnp.maximum(n - 1, 0)
    kblock = (B_KV_HEADS, BLK, B_HEAD_DIM)
    return pl.pallas_call(
        _swa_kernel,
        grid=(batch, nb),
        in_specs=[_resident((1, N_HEADS * BLK)),
                  pl.BlockSpec((1, B_HEAD_DIM, N_HEADS * BLK), lambda b, n: (cur(b, n), 0, 0)),
                  pl.BlockSpec(kblock, lambda b, n: (0, prev(b, n), 0)),
                  pl.BlockSpec(kblock, lambda b, n: (0, cur(b, n), 0)),
                  pl.BlockSpec((B_KV, BLK), lambda b, n: (0, prev(b, n))),
                  pl.BlockSpec((B_KV, BLK), lambda b, n: (0, cur(b, n))),
                  _resident((3, N_HEADS, BLK, BLK))],
        out_specs=pl.BlockSpec((BLK, B_Q), lambda b, n: (cur(b, n), 0)),
        out_shape=jax.ShapeDtypeStruct((batch * seq, B_Q), BF16),
        compiler_params=_params(2),
        name="swa_attn",
    )(sinks_row, qr, kk, kk, vt, vt, bts)


FF_CHUNK = 256


def _ffn_kernel(x_ref, o_ref, wo_ref, bo_ref, g_ref, wgu_ref, wd_ref, fg_ref,
                out_ref, h_ref, act_ref, *, d_ff, final):
    x1 = x_ref[...] + _dot(o_ref[...], wo_ref[...]) + bo_ref[...]
    out_ref[...] = x1
    h_ref[...] = _rms(x1, g_ref[...]).astype(BF16)
    for c in range(d_ff // FF_CHUNK):
        gate = _dot(h_ref[...], wgu_ref[:, c * FF_CHUNK:(c + 1) * FF_CHUNK])
        up = _dot(h_ref[...], wgu_ref[:, d_ff + c * FF_CHUNK:d_ff + (c + 1) * FF_CHUNK])
        act_ref[:, c * FF_CHUNK:(c + 1) * FF_CHUNK] = (
            gate * jax.nn.sigmoid(gate) * up).astype(BF16)
    y = out_ref[...] + _dot(act_ref[...], wd_ref[...])
    if final:
        y = _rms(y, fg_ref[...])
    out_ref[...] = y


def _ffn(x2, o, wo, bo, g, wgu_all, wd_all, layer, fg, final, tm=512):
    n, d = x2.shape
    d_ff = wd_all.shape[1]
    row = lambda i: (i, 0)
    pick = lambda i: (layer, 0, 0)
    return pl.pallas_call(
        functools.partial(_ffn_kernel, d_ff=d_ff, final=final),
        grid=(n // tm,),
        in_specs=[pl.BlockSpec((tm, d), row),
                  pl.BlockSpec((tm, o.shape[1]), row),
                  _resident(wo.shape),
                  _resident((1, d)),
                  _resident((1, d)),
                  pl.BlockSpec((None,) + wgu_all.shape[1:], pick, pipeline_mode=pl.Buffered(1)),
                  pl.BlockSpec((None,) + wd_all.shape[1:], pick, pipeline_mode=pl.Buffered(1)),
                  _resident((1, d))],
        out_specs=pl.BlockSpec((tm, d), row),
        out_shape=jax.ShapeDtypeStruct((n, d), F32),
        scratch_shapes=[pltpu.VMEM((tm, d), BF16),
                        pltpu.VMEM((tm, d_ff), BF16)],
        compiler_params=_params(1),
        name="ffn",
    )(x2, o, wo, bo, g, wgu_all, wd_all, fg)


def kernel(x, rel_bias, attn_norm, ffn_norm, final_norm, a_w_in, a_latent_norm, a_w_uv, a_w_out, b_w_in, b_b_in, b_sinks, b_w_out, b_b_out, ffn_w_gate_up, ffn_w_down):
    batch, seq, d = x.shape
    x2 = x.reshape(batch * seq, d)
    bts, btd = _bias_tiles(rel_bias)
    fg = final_norm.reshape(1, d)
    wgu_all = ffn_w_gate_up.astype(BF16)
    wd_all = ffn_w_down.astype(BF16)

    wa = a_w_in[0]
    o_ckv, o_qidx, o_kidx, o_widx = A_QLAT, A_QLAT + A_LATENT, A_QLAT + A_LATENT + A_QIDX, \
        A_QLAT + A_LATENT + A_QIDX + IDX_DIM
    wt = jnp.concatenate(
        [wa[:, :o_ckv], wa[:, o_qidx:o_kidx], wa[:, o_widx:], jnp.zeros((d, 16 - IDX_HEADS), F32)],
        axis=1).T.astype(BF16)
    wn = jnp.concatenate(
        [wa[:, o_ckv:o_qidx], wa[:, o_kidx:o_widx], jnp.zeros((d, 256 - A_LATENT - IDX_DIM), F32)],
        axis=1).astype(BF16)
    qr, qir, wto, kidx, ckv, ckvt = _a_proj(
        x2, attn_norm[0].reshape(1, d), wt, wn, a_latent_norm[0].reshape(1, A_LATENT))
    wuvt = jnp.swapaxes(a_w_uv[0], 1, 2).astype(BF16)
    o = _dsa_attn(qr, qir, wto, kidx, ckv, ckvt, wuvt, btd, batch, seq)
    x2 = _ffn(x2, o, a_w_out[0].astype(BF16), jnp.zeros((1, d), F32),
              ffn_norm[0].reshape(1, d), wgu_all, wd_all, 0, fg, final=False)

    wb, bb = b_w_in[0], b_b_in[0]
    qv_cols = np.r_[0:B_Q, B_Q + B_KV:B_Q + 2 * B_KV]
    qr, kk, vt = _b_proj(
        x2, attn_norm[1].reshape(1, d), wb[:, qv_cols].T.astype(BF16),
        bb[qv_cols].reshape(-1, 1), wb[:, B_Q:B_Q + B_KV].astype(BF16),
        bb[B_Q:B_Q + B_KV].reshape(1, B_KV))
    sinks_row = jnp.repeat(b_sinks[0], BLK).reshape(1, N_HEADS * BLK)
    o = _swa_attn(qr, kk, vt, sinks_row, bts, batch, seq)
    x2 = _ffn(x2, o, b_w_out[0].astype(BF16), b_b_out[0].reshape(1, d),
              ffn_norm[1].reshape(1, d), wgu_all, wd_all, 1, fg, final=True)
    return x2.reshape(batch, seq, d)
```

```python
import functools
import math

import numpy as np
import jax
import jax.numpy as jnp
from jax import lax
from jax.experimental import pallas as pl
from jax.experimental.pallas import tpu as pltpu

F32 = jnp.float32
BF16 = jnp.bfloat16
I32 = jnp.int32

EPS = 1e-6
N_HEADS = 16
A_LATENT = 128
A_VHEAD = 64
IDX_HEADS = 8
IDX_DIM = 64
TOPK_MAX = 256
B_KV_HEADS = 2
B_HEAD_DIM = 64
REL_BUCKETS = 32
REL_MAX_DIST = 128

BLK = 128
SC = 256
INT_MIN = -(2 ** 31)
NEG_INF = float("-inf")
LOG2E = math.log2(math.e)
VMEM_LIMIT = 56 * 1024 * 1024


def _rms(xf, g):
    ms = jnp.mean(xf * xf, axis=-1, keepdims=True)
    return xf * lax.rsqrt(ms + EPS) * g


def _dot(a, b):
    return jnp.dot(a, b, preferred_element_type=F32)


def _dot_nt(a, b):
    return lax.dot_general(a, b, (((1,), (1,)), ((), ())), preferred_element_type=F32)


def _params(n_axes):
    return pltpu.CompilerParams(
        dimension_semantics=("arbitrary",) * n_axes, vmem_limit_bytes=VMEM_LIMIT)


def _resident(shape):
    zeros = (0,) * len(shape)
    return pl.BlockSpec(shape, lambda *_: zeros, pipeline_mode=pl.Buffered(1))


def _bucket(rel):
    n = np.maximum(rel, 0)
    max_exact = REL_BUCKETS // 2
    nf = np.maximum(n, max_exact).astype(np.float64)
    large = max_exact + (np.log(nf / max_exact) / math.log(REL_MAX_DIST / max_exact)
                         * (REL_BUCKETS - max_exact)).astype(np.int32)
    large = np.minimum(large, REL_BUCKETS - 1)
    return np.where(n < max_exact, n, large).astype(np.int32)


def _bucket_tables():
    k = np.arange(BLK)[:, None]
    q = np.arange(BLK)[None, :]
    return _bucket(np.stack([q - k, BLK + q - k]))


def _bias_kernel(rb_ref, bidx_ref, bts_ref, btd_ref):
    k = lax.broadcasted_iota(I32, (BLK, BLK), 0)
    q = lax.broadcasted_iota(I32, (BLK, BLK), 1)
    window = (k <= q, k > q)
    for h in range(N_HEADS):
        for kind in range(2):
            bidx = bidx_ref[kind]

            t = jnp.zeros((BLK, BLK), F32)
            for b in range(REL_BUCKETS):
                t = jnp.where(bidx == b, rb_ref[b, h], t)
            bts_ref[kind, h] = jnp.where(window[kind], t * LOG2E, NEG_INF)
            btd_ref[kind, h] = (t - rb_ref[REL_BUCKETS - 1, h]) * LOG2E
        bts_ref[2, h] = jnp.full((BLK, BLK), NEG_INF, F32)
        btd_ref[2, h] = jnp.zeros((BLK, BLK), F32)


def _bias_tiles(rel_bias):
    return pl.pallas_call(
        _bias_kernel,
        out_shape=(jax.ShapeDtypeStruct((3, N_HEADS, BLK, BLK), F32),
                   jax.ShapeDtypeStruct((3, N_HEADS, BLK, BLK), F32)),
        in_specs=[pl.BlockSpec(memory_space=pltpu.SMEM),
                  pl.BlockSpec(memory_space=pltpu.VMEM)],
        out_specs=(pl.BlockSpec(memory_space=pltpu.VMEM),
                   pl.BlockSpec(memory_space=pltpu.VMEM)),
        name="bias_tiles",
    )(rel_bias, jnp.asarray(_bucket_tables()))


A_QLAT = N_HEADS * A_LATENT
A_QIDX = IDX_HEADS * IDX_DIM
A_WT_ROWS = A_QLAT + A_QIDX + 16


def _a_proj_kernel(x_ref, g_ref, wt_ref, wn_ref, lg_ref,
                   qr_ref, qir_ref, wto_ref, kidx_ref, ckv_ref, ckvt_ref,
                   h_ref, ht_ref):
    tm = x_ref.shape[0]
    hf = _rms(x_ref[...], g_ref[...])
    h_ref[...] = hf.astype(BF16)
    ht_ref[...] = hf.T.astype(BF16)
    for c in range(A_QLAT // 512):
        r = _dot(wt_ref[c * 512:(c + 1) * 512, :], ht_ref[...]).astype(BF16)
        for hh in range(4):
            h = 4 * c + hh
            for blk in range(tm // BLK):
                qr_ref[blk, :, h * BLK:(h + 1) * BLK] = (
                    r[hh * A_LATENT:(hh + 1) * A_LATENT, blk * BLK:(blk + 1) * BLK])
    r = _dot(wt_ref[A_QLAT:A_QLAT + A_QIDX, :], ht_ref[...]).astype(BF16)
    for j in range(IDX_HEADS):
        for blk in range(tm // BLK):
            qir_ref[blk, :, j * BLK:(j + 1) * BLK] = (
                r[j * IDX_DIM:(j + 1) * IDX_DIM, blk * BLK:(blk + 1) * BLK])
    w = _dot(wt_ref[A_QLAT + A_QIDX:, :], ht_ref[...])
    wto_ref[...] = w[:IDX_HEADS] * (IDX_HEADS ** -0.5)
    t = _dot(h_ref[...], wn_ref[...])
    ckv = _rms(t[:, :A_LATENT], lg_ref[...])
    ckv_ref[...] = ckv.astype(BF16)
    kidx_ref[...] = t[:, A_LATENT:A_LATENT + IDX_DIM].astype(BF16)
    for b2 in range(tm // SC):
        ckvt_ref[b2] = ckv[b2 * SC:(b2 + 1) * SC, :].T.astype(BF16)


def _a_proj(x2, g, wt, wn, lg, tm=512):
    n, d = x2.shape
    row = lambda i: (i, 0)
    blk3 = lambda i: (i, 0, 0)
    return pl.pallas_call(
        _a_proj_kernel,
        grid=(n // tm,),
        in_specs=[pl.BlockSpec((tm, d), row),
                  _resident((1, d)),
                  _resident((A_WT_ROWS, d)),
                  _resident((d, 256)),
                  _resident((1, A_LATENT))],
        out_specs=(pl.BlockSpec((tm // BLK, A_LATENT, A_QLAT), blk3),
                   pl.BlockSpec((tm // BLK, IDX_DIM, IDX_HEADS * BLK), blk3),
                   pl.BlockSpec((IDX_HEADS, tm), lambda i: (0, i)),
                   pl.BlockSpec((tm, IDX_DIM), row),
                   pl.BlockSpec((tm, A_LATENT), row),
                   pl.BlockSpec((tm // SC, A_LATENT, SC), blk3)),
        out_shape=(jax.ShapeDtypeStruct((n // BLK, A_LATENT, A_QLAT), BF16),
                   jax.ShapeDtypeStruct((n // BLK, IDX_DIM, IDX_HEADS * BLK), BF16),
                   jax.ShapeDtypeStruct((IDX_HEADS, n), F32),
                   jax.ShapeDtypeStruct((n, IDX_DIM), BF16),
                   jax.ShapeDtypeStruct((n, A_LATENT), BF16),
                   jax.ShapeDtypeStruct((n // SC, A_LATENT, SC), BF16)),
        scratch_shapes=[pltpu.VMEM((tm, d), BF16), pltpu.VMEM((d, tm), BF16)],
        compiler_params=_params(1),
        name="a_proj",
    )(x2, g, wt, wn, lg)


def _dsa_kernel(qr_ref, qir_ref, wt_ref, kidx_ref, ckv_ref, ckvt_ref, wuvt_ref,
                btd_ref, tri_ref, out_ref,
                key_ref, madd_ref, acc_ref, pb_ref, l_ref, *, k_sel, n_sc):
    i = pl.program_id(1)
    nsc = i // 2 + 1
    kf = float(k_sel)
    krow = lax.broadcasted_iota(I32, (SC, BLK), 0)
    qpos = i * BLK + lax.broadcasted_iota(I32, (SC, BLK), 1)

    def causal(sc):
        return sc * SC + krow <= qpos

    def key_rows(ref, sc, count=1):
        return ref[pl.ds(pl.multiple_of(sc * SC, SC), count * SC), :]

    def colsum(x):
        return jnp.sum(x.reshape(SC // 8, 8, BLK), axis=0)

    w2 = wt_ref[...] * (IDX_DIM ** -0.5)

    def index_group(sc0, count, last=False):
        d = _dot(key_rows(kidx_ref, sc0, count), qir_ref[0])
        for c in range(count):
            rows = slice(c * SC, (c + 1) * SC)
            s = w2[0:1, :] * jnp.maximum(d[rows, 0:BLK], 0.0)
            for j in range(1, IDX_HEADS):
                s = s + w2[j:j + 1, :] * jnp.maximum(d[rows, j * BLK:(j + 1) * BLK], 0.0)
            bits = lax.bitcast_convert_type(s, I32)
            key = bits ^ ((bits >> 31) & 0x7FFFFFFF)
            key = jnp.where(key == -1, 0, key)
            if last:
                key = jnp.where(causal(sc0 + c), key, INT_MIN)
            key_ref[sc0 + c] = key

    def index_body(t, carry):
        index_group(2 * t, 2)
        return carry
    lax.fori_loop(0, (nsc - 1) // 2, index_body, 0)

    @pl.when((nsc - 1) % 2 == 1)
    def _():
        index_group(nsc - 2, 1)
    index_group(nsc - 1, 1, last=True)

    @pl.when((i + 1) * BLK <= k_sel)
    def _():
        def body(sc, carry):
            madd_ref[sc] = jnp.where(causal(sc), 0.0, NEG_INF)
            return carry
        lax.fori_loop(0, nsc, body, 0)

    def search(nblk):
        def count_ge(cand):
            cb = jnp.broadcast_to(cand, (8, BLK))
            parts = [jnp.zeros((8, BLK), F32)] * 8
            for t in range(nblk * BLK // 8):
                sc, r = divmod(t * 8, SC)
                ge = key_ref[sc, r:r + 8, :] >= cb
                parts[t % 8] = parts[t % 8] + jnp.where(ge, 1.0, 0.0)
            part = sum(parts[1:], parts[0])
            return jnp.sum(part, axis=0, keepdims=True)

        ans = jnp.where(count_ge(jnp.zeros((1, BLK), I32)) >= kf, 0, INT_MIN)

        def bit_body(it, ans):
            cand = ans | jnp.left_shift(1, 30 - it)
            return jnp.where(count_ge(cand) >= kf, cand, ans)
        return lax.fori_loop(0, 31, bit_body, ans)

    @pl.when((i + 1) * BLK > k_sel)
    def _():
        for nblk in range(k_sel // BLK + 1, 2 * n_sc + 1):
            @pl.when(i + 1 == nblk)
            def _(nblk=nblk):
                key_ref[n_sc] = jnp.broadcast_to(search(nblk), (SC, BLK))
        thr = key_ref[n_sc][0:1, :]

        def ge_body(sc, part):
            ge = key_ref[sc] >= thr
            madd_ref[sc] = jnp.where(ge, 0.0, NEG_INF)
            return part + colsum(jnp.where(ge, 1.0, 0.0))
        n_ge = jnp.sum(lax.fori_loop(0, nsc, ge_body, jnp.zeros((8, BLK), F32)),
                       axis=0, keepdims=True)

        @pl.when(jnp.max(n_ge) > kf)
        def _():
            def gt_body(sc, part):
                return part + colsum(jnp.where(key_ref[sc] > thr, 1.0, 0.0))
            n_gt = jnp.sum(lax.fori_loop(0, nsc, gt_body, jnp.zeros((8, BLK), F32)),
                           axis=0, keepdims=True)
            need = kf - n_gt

            def mask_body(sc, off):
                key = key_ref[sc]
                eq = key == thr
                eqf = jnp.where(eq, 1.0, 0.0)
                before = _dot(tri_ref[...], eqf.astype(BF16)) + off
                sel = (key > thr) | (eq & (before < need))
                madd_ref[sc] = jnp.where(sel, 0.0, NEG_INF)
                return off + jnp.sum(eqf, axis=0, keepdims=True)
            lax.fori_loop(0, nsc, mask_body, jnp.zeros((1, BLK), F32))

    c1 = (A_LATENT ** -0.5) * LOG2E
    hcols = N_HEADS * BLK

    def logits_group(sc0, m, count, near):
        lt = _dot(key_rows(ckv_ref, sc0, count), qr_ref[0])
        madds = [madd_ref[sc0 + c] for c in range(count)]
        new_m = []
        for h in range(N_HEADS):
            hs = slice(h * BLK, (h + 1) * BLK)
            mh = m[:, hs]
            for c in range(count):
                v = lt[c * SC:(c + 1) * SC, hs] * c1 + madds[c]
                if near:
                    halves = []
                    for half in range(2):
                        kind = jnp.clip(i - (2 * (sc0 + c) + half), 0, 2)
                        halves.append(v[half * BLK:(half + 1) * BLK] + btd_ref[kind, h])
                    v = jnp.concatenate(halves, axis=0)
                l_ref[sc0 + c, h] = v
                mh = jnp.maximum(mh, jnp.max(v, axis=0, keepdims=True))
            new_m.append(mh)
        return jnp.concatenate(new_m, axis=1)

    n_far = nsc - 2
    m = jnp.full((1, hcols), NEG_INF, F32)
    m = lax.fori_loop(0, n_far // 2, lambda t, m: logits_group(2 * t, m, 2, False), m)
    m = lax.cond((n_far > 0) & (n_far % 2 == 1),
                 lambda m: logits_group(n_far - 1, m, 1, False), lambda m: m, m)
    m = lax.cond(nsc >= 2, lambda m: logits_group(nsc - 2, m, 2, True),
                 lambda m: logits_group(0, m, 1, True), m)

    def pv_group(sc0, s, count):
        vt = jnp.concatenate([ckvt_ref[sc0 + c] for c in range(count)], axis=1)
        new_s = []
        for hp in range(N_HEADS // 2):
            cols = []
            for h in (2 * hp, 2 * hp + 1):
                hs = slice(h * BLK, (h + 1) * BLK)
                sh = s[:, hs]
                ps = []
                for c in range(count):
                    p = jnp.exp2(l_ref[sc0 + c, h] - m[:, hs])
                    sh = sh + colsum(p)
                    ps.append(p.astype(BF16))
                new_s.append(sh)
                cols.append(jnp.concatenate(ps, axis=0))
            pair = jnp.concatenate(cols, axis=1)
            acc_ref[hp] = acc_ref[hp] + _dot(vt, pair)
        return jnp.concatenate(new_s, axis=1)

    acc_ref[...] = jnp.zeros(acc_ref.shape, F32)
    s = lax.fori_loop(0, nsc // 2, lambda t, s: pv_group(2 * t, s, 2),
                      jnp.zeros((8, hcols), F32))
    s = lax.cond(nsc % 2 == 1, lambda s: pv_group(nsc - 1, s, 1), lambda s: s, s)
    inv = 1.0 / jnp.sum(s, axis=0, keepdims=True)

    outs = []
    for h in range(N_HEADS):
        hs = slice(h * BLK, (h + 1) * BLK)
        a = acc_ref[h // 2, :, (h % 2) * BLK:(h % 2 + 1) * BLK]
        o = (a * inv[:, hs]).astype(BF16)
        outs.append(_dot(wuvt_ref[h], o))
    out_ref[...] = jnp.concatenate(outs, axis=0).T.astype(BF16)


def _dsa_attn(qr, qir, wt, kidx, ckv, ckvt, wuvt, btd, batch, seq):
    nblk = seq // BLK
    n_sc = seq // SC
    k_sel = min(TOPK_MAX, seq // 4)
    hcols = N_HEADS * BLK
    tri = jnp.asarray(np.tril(np.ones((SC, SC), np.float32), -1), BF16)
    return pl.pallas_call(
        functools.partial(_dsa_kernel, k_sel=k_sel, n_sc=n_sc),
        grid=(batch, nblk),
        in_specs=[pl.BlockSpec((1, A_LATENT, hcols), lambda b, i: (b * nblk + i, 0, 0)),
                  pl.BlockSpec((1, IDX_DIM, IDX_HEADS * BLK), lambda b, i: (b * nblk + i, 0, 0)),
                  pl.BlockSpec((IDX_HEADS, BLK), lambda b, i: (0, b * nblk + i)),
                  pl.BlockSpec((seq, IDX_DIM), lambda b, i: (b, 0)),
                  pl.BlockSpec((seq, A_LATENT), lambda b, i: (b, 0)),
                  pl.BlockSpec((n_sc, A_LATENT, SC), lambda b, i: (b, 0, 0)),
                  _resident((N_HEADS, A_VHEAD, A_LATENT)),
                  _resident((3, N_HEADS, BLK, BLK)),
                  _resident((SC, SC))],
        out_specs=pl.BlockSpec((BLK, N_HEADS * A_VHEAD), lambda b, i: (b * nblk + i, 0)),
        out_shape=jax.ShapeDtypeStruct((batch * seq, N_HEADS * A_VHEAD), BF16),
        scratch_shapes=[
            pltpu.VMEM((n_sc + 1, SC, BLK), I32),
            pltpu.VMEM((n_sc, SC, BLK), F32),
            pltpu.VMEM((N_HEADS // 2, A_LATENT, 2 * BLK), F32),
            pltpu.VMEM((N_HEADS // 2, 2 * SC, 2 * BLK), BF16),
            pltpu.VMEM((n_sc, N_HEADS, SC, BLK), F32),
        ],
        compiler_params=_params(2),
        name="dsa_attn",
    )(qr, qir, wt, kidx, ckv, ckvt, wuvt, btd, tri)


B_Q = N_HEADS * B_HEAD_DIM
B_KV = B_KV_HEADS * B_HEAD_DIM


def _b_proj_kernel(x_ref, g_ref, wt_ref, bcol_ref, wk_ref, bk_ref,
                   qr_ref, kk_ref, vt_ref, h_ref, ht_ref):
    tm = x_ref.shape[0]
    hf = _rms(x_ref[...], g_ref[...])
    h_ref[...] = hf.astype(BF16)
    ht_ref[...] = hf.T.astype(BF16)
    heads_per_dot = 512 // B_HEAD_DIM
    for c in range(B_Q // 512):
        rows = slice(c * 512, (c + 1) * 512)
        r = (_dot(wt_ref[rows, :], ht_ref[...]) + bcol_ref[rows, :]).astype(BF16)
        for hh in range(heads_per_dot):
            h = c * heads_per_dot + hh
            for blk in range(tm // BLK):
                qr_ref[blk, :, h * BLK:(h + 1) * BLK] = (
                    r[hh * B_HEAD_DIM:(hh + 1) * B_HEAD_DIM, blk * BLK:(blk + 1) * BLK])
    vt_ref[...] = (_dot(wt_ref[B_Q:, :], ht_ref[...]) + bcol_ref[B_Q:, :]).astype(BF16)
    t = _dot(h_ref[...], wk_ref[...]) + bk_ref[...]
    for kv in range(B_KV_HEADS):
        kk_ref[kv] = t[:, kv * B_HEAD_DIM:(kv + 1) * B_HEAD_DIM].astype(BF16)


def _b_proj(x2, g, wt, bcol, wk, bk, tm=512):
    n, d = x2.shape
    return pl.pallas_call(
        _b_proj_kernel,
        grid=(n // tm,),
        in_specs=[pl.BlockSpec((tm, d), lambda i: (i, 0)),
                  _resident((1, d)),
                  _resident((B_Q + B_KV, d)),
                  _resident((B_Q + B_KV, 1)),
                  _resident((d, B_KV)),
                  _resident((1, B_KV))],
        out_specs=(pl.BlockSpec((tm // BLK, B_HEAD_DIM, N_HEADS * BLK), lambda i: (i, 0, 0)),
                   pl.BlockSpec((B_KV_HEADS, tm, B_HEAD_DIM), lambda i: (0, i, 0)),
                   pl.BlockSpec((B_KV, tm), lambda i: (0, i))),
        out_shape=(jax.ShapeDtypeStruct((n // BLK, B_HEAD_DIM, N_HEADS * BLK), BF16),
                   jax.ShapeDtypeStruct((B_KV_HEADS, n, B_HEAD_DIM), BF16),
                   jax.ShapeDtypeStruct((B_KV, n), BF16)),
        scratch_shapes=[pltpu.VMEM((tm, d), BF16), pltpu.VMEM((d, tm), BF16)],
        compiler_params=_params(1),
        name="b_proj",
    )(x2, g, wt, bcol, wk, bk)


def _swa_kernel(sink_ref, qr_ref, kp_ref, kc_ref, vp_ref, vc_ref, bts_ref, out_ref):
    n = pl.program_id(1)
    first_kind = jnp.where(n >= 1, 1, 2)
    for qb in range(2):
        if qb == 0:
            _swa_block(sink_ref, qr_ref.at[0], kp_ref, kc_ref.at[:, 0:BLK, :],
                       vp_ref, vc_ref.at[:, 0:BLK], bts_ref, out_ref.at[0:BLK, :], first_kind)
        else:
            _swa_block(sink_ref, qr_ref.at[1], kc_ref.at[:, 0:BLK, :], kc_ref.at[:, BLK:2 * BLK, :],
                       vc_ref.at[:, 0:BLK], vc_ref.at[:, BLK:2 * BLK], bts_ref,
                       out_ref.at[BLK:2 * BLK, :], 1)


def _swa_block(sink_ref, qr_ref, kp_ref, kc_ref, vp_ref, vc_ref, bts_ref, out_ref, prev_kind):
    c1 = (B_HEAD_DIM ** -0.5) * LOG2E
    group = N_HEADS // B_KV_HEADS
    outs = []
    for kv in range(B_KV_HEADS):
        cols = slice(kv * group * BLK, (kv + 1) * group * BLK)
        q = qr_ref[:, cols]
        lp = _dot(kp_ref[kv], q)
        lc = _dot(kc_ref[kv], q)
        sink = sink_ref[:, cols] * LOG2E
        vrows = slice(kv * B_HEAD_DIM, (kv + 1) * B_HEAD_DIM)
        for g2 in range(group // 2):
            pps, pcs, invs = [], [], []
            for g in (2 * g2, 2 * g2 + 1):
                h = kv * group + g
                hs = slice(g * BLK, (g + 1) * BLK)
                a = lp[:, hs] * c1 + bts_ref[prev_kind, h]
                b = lc[:, hs] * c1 + bts_ref[0, h]
                m = jnp.maximum(jnp.max(jnp.maximum(a, b), axis=0, keepdims=True), sink[:, hs])
                pp = jnp.exp2(a - m)
                pc = jnp.exp2(b - m)
                den = jnp.sum(pp + pc, axis=0, keepdims=True) + jnp.exp2(sink[:, hs] - m)
                invs.append(1.0 / den)
                pps.append(pp.astype(BF16))
                pcs.append(pc.astype(BF16))
            ot = (_dot(vp_ref[vrows, :], jnp.concatenate(pps, axis=1))
                  + _dot(vc_ref[vrows, :], jnp.concatenate(pcs, axis=1)))
            for j in range(2):
                outs.append(ot[:, j * BLK:(j + 1) * BLK] * invs[j])
    out_ref[...] = jnp.concatenate(outs, axis=0).T.astype(BF16)


def _swa_attn(qr, kk, vt, sinks_row, bts, batch, seq):
    nb2 = seq // (2 * BLK)
    cur = lambda b, n: b * nb2 + n
    prev = lambda b, n: 2 * (b * nb2 + n) - jnp.minimum(n, 1)
    return pl.pallas_call(
        _swa_kernel,
        grid=(batch, nb2),
        in_specs=[_resident((1, N_HEADS * BLK)),
                  pl.BlockSpec((2, B_HEAD_DIM, N_HEADS * BLK), lambda b, n: (cur(b, n), 0, 0)),
                  pl.BlockSpec((B_KV_HEADS, BLK, B_HEAD_DIM), lambda b, n: (0, prev(b, n), 0)),
                  pl.BlockSpec((B_KV_HEADS, 2 * BLK, B_HEAD_DIM), lambda b, n: (0, cur(b, n), 0)),
                  pl.BlockSpec((B_KV, BLK), lambda b, n: (0, prev(b, n))),
                  pl.BlockSpec((B_KV, 2 * BLK), lambda b, n: (0, cur(b, n))),
                  _resident((3, N_HEADS, BLK, BLK))],
        out_specs=pl.BlockSpec((2 * BLK, B_Q), lambda b, n: (cur(b, n), 0)),
        out_shape=jax.ShapeDtypeStruct((batch * seq, B_Q), BF16),
        compiler_params=_params(2),
        name="swa_attn",
    )(sinks_row, qr, kk, kk, vt, vt, bts)


FF_CHUNK = 256


def _ffn_kernel(x_ref, o_ref, wo_ref, bo_ref, g_ref, wgu_ref, wd_ref, fg_ref,
                out_ref, h_ref, act_ref, *, d_ff, final):
    x1 = x_ref[...] + _dot(o_ref[...], wo_ref[...]) + bo_ref[...]
    out_ref[...] = x1
    h_ref[...] = _rms(x1, g_ref[...]).astype(BF16)
    for c in range(d_ff // FF_CHUNK):
        gate = _dot(h_ref[...], wgu_ref[:, c * FF_CHUNK:(c + 1) * FF_CHUNK])
        up = _dot(h_ref[...], wgu_ref[:, d_ff + c * FF_CHUNK:d_ff + (c + 1) * FF_CHUNK])
        act_ref[:, c * FF_CHUNK:(c + 1) * FF_CHUNK] = (
            gate * jax.nn.sigmoid(gate) * up).astype(BF16)
    y = out_ref[...] + _dot(act_ref[...], wd_ref[...])
    if final:
        y = _rms(y, fg_ref[...])
    out_ref[...] = y


def _ffn(x2, o, wo, bo, g, wgu_all, wd_all, layer, fg, final, tm=1024):
    n, d = x2.shape
    d_ff = wd_all.shape[1]
    row = lambda i: (i, 0)
    pick = lambda i: (layer, 0, 0)
    return pl.pallas_call(
        functools.partial(_ffn_kernel, d_ff=d_ff, final=final),
        grid=(n // tm,),
        in_specs=[pl.BlockSpec((tm, d), row),
                  pl.BlockSpec((tm, o.shape[1]), row),
                  _resident(wo.shape),
                  _resident((1, d)),
                  _resident((1, d)),
                  pl.BlockSpec((None,) + wgu_all.shape[1:], pick, pipeline_mode=pl.Buffered(1)),
                  pl.BlockSpec((None,) + wd_all.shape[1:], pick, pipeline_mode=pl.Buffered(1)),
                  _resident((1, d))],
        out_specs=pl.BlockSpec((tm, d), row),
        out_shape=jax.ShapeDtypeStruct((n, d), F32),
        scratch_shapes=[pltpu.VMEM((tm, d), BF16),
                        pltpu.VMEM((tm, d_ff), BF16)],
        compiler_params=_params(1),
        name="ffn",
    )(x2, o, wo, bo, g, wgu_all, wd_all, fg)


def kernel(x, rel_bias, attn_norm, ffn_norm, final_norm, a_w_in, a_latent_norm, a_w_uv, a_w_out, b_w_in, b_b_in, b_sinks, b_w_out, b_b_out, ffn_w_gate_up, ffn_w_down):
    batch, seq, d = x.shape
    x2 = x.reshape(batch * seq, d)
    bts, btd = _bias_tiles(rel_bias)
    fg = final_norm.reshape(1, d)
    wgu_all = ffn_w_gate_up.astype(BF16)
    wd_all = ffn_w_down.astype(BF16)

    wa = a_w_in[0].astype(BF16)
    o_ckv, o_qidx, o_kidx, o_widx = A_QLAT, A_QLAT + A_LATENT, A_QLAT + A_LATENT + A_QIDX, \
        A_QLAT + A_LATENT + A_QIDX + IDX_DIM
    wt = jnp.concatenate(
        [wa[:, :o_ckv], wa[:, o_qidx:o_kidx], wa[:, o_widx:], jnp.zeros((d, 16 - IDX_HEADS), BF16)],
        axis=1).T
    wn = jnp.concatenate(
        [wa[:, o_ckv:o_qidx], wa[:, o_kidx:o_widx], jnp.zeros((d, 256 - A_LATENT - IDX_DIM), BF16)],
        axis=1)
    qr, qir, wto, kidx, ckv, ckvt = _a_proj(
        x2, attn_norm[0].reshape(1, d), wt, wn, a_latent_norm[0].reshape(1, A_LATENT))
    wuvt = jnp.swapaxes(a_w_uv[0], 1, 2).astype(BF16)
    o = _dsa_attn(qr, qir, wto, kidx, ckv, ckvt, wuvt, btd, batch, seq)
    x2 = _ffn(x2, o, a_w_out[0].astype(BF16), jnp.zeros((1, d), F32),
              ffn_norm[0].reshape(1, d), wgu_all, wd_all, 0, fg, final=False)

    wb, bb = b_w_in[0].astype(BF16), b_b_in[0]
    wqv = jnp.concatenate([wb[:, :B_Q], wb[:, B_Q + B_KV:]], axis=1).T
    bqv = jnp.concatenate([bb[:B_Q], bb[B_Q + B_KV:]]).reshape(-1, 1)
    qr, kk, vt = _b_proj(
        x2, attn_norm[1].reshape(1, d), wqv, bqv, wb[:, B_Q:B_Q + B_KV],
        bb[B_Q:B_Q + B_KV].reshape(1, B_KV))
    sinks_row = jnp.repeat(b_sinks[0], BLK).reshape(1, N_HEADS * BLK)
    o = _swa_attn(qr, kk, vt, sinks_row, bts, batch, seq)
    x2 = _ffn(x2, o, b_w_out[0].astype(BF16), b_b_out[0].reshape(1, d),
              ffn_norm[1].reshape(1, d), wgu_all, wd_all, 1, fg, final=True)
    return x2.reshape(batch, seq, d)
```

```python
import functools
import math

import numpy as np
import jax
import jax.numpy as jnp
from jax import lax
from jax.experimental import pallas as pl
from jax.experimental.pallas import tpu as pltpu

F32 = jnp.float32
BF16 = jnp.bfloat16
I32 = jnp.int32

EPS = 1e-6
N_HEADS = 16
A_LATENT = 128
A_VHEAD = 64
IDX_HEADS = 8
IDX_DIM = 64
TOPK_MAX = 256
B_KV_HEADS = 2
B_HEAD_DIM = 64
REL_BUCKETS = 32
REL_MAX_DIST = 128

BLK = 128
SC = 256
INT_MIN = -(2 ** 31)
NEG_INF = float("-inf")
LOG2E = math.log2(math.e)
VMEM_LIMIT = 56 * 1024 * 1024


def _rms(xf, g):
    ms = jnp.mean(xf * xf, axis=-1, keepdims=True)
    return xf * lax.rsqrt(ms + EPS) * g


def _dot(a, b):
    return jnp.dot(a, b, preferred_element_type=F32)


def _dot_nt(a, b):
    return lax.dot_general(a, b, (((1,), (1,)), ((), ())), preferred_element_type=F32)


def _params(n_axes):
    return pltpu.CompilerParams(
        dimension_semantics=("arbitrary",) * n_axes, vmem_limit_bytes=VMEM_LIMIT)


def _resident(shape):
    zeros = (0,) * len(shape)
    return pl.BlockSpec(shape, lambda *_: zeros, pipeline_mode=pl.Buffered(1))


def _bucket(rel):
    n = np.maximum(rel, 0)
    max_exact = REL_BUCKETS // 2
    nf = np.maximum(n, max_exact).astype(np.float64)
    large = max_exact + (np.log(nf / max_exact) / math.log(REL_MAX_DIST / max_exact)
                         * (REL_BUCKETS - max_exact)).astype(np.int32)
    large = np.minimum(large, REL_BUCKETS - 1)
    return np.where(n < max_exact, n, large).astype(np.int32)


def _bucket_tables():
    k = np.arange(BLK)[:, None]
    q = np.arange(BLK)[None, :]
    return _bucket(np.stack([q - k, BLK + q - k]))


def _bias_kernel(rb_ref, bidx_ref, bts_ref, btd_ref):
    k = lax.broadcasted_iota(I32, (BLK, BLK), 0)
    q = lax.broadcasted_iota(I32, (BLK, BLK), 1)
    window = (k <= q, k > q)
    for h in range(N_HEADS):
        for kind in range(2):
            bidx = bidx_ref[kind]

            t = jnp.zeros((BLK, BLK), F32)
            for b in range(REL_BUCKETS):
                t = jnp.where(bidx == b, rb_ref[b, h], t)
            bts_ref[kind, h] = jnp.where(window[kind], t * LOG2E, NEG_INF)
            btd_ref[kind, h] = (t - rb_ref[REL_BUCKETS - 1, h]) * LOG2E
        bts_ref[2, h] = jnp.full((BLK, BLK), NEG_INF, F32)
        btd_ref[2, h] = jnp.zeros((BLK, BLK), F32)


def _bias_tiles(rel_bias):
    return pl.pallas_call(
        _bias_kernel,
        out_shape=(jax.ShapeDtypeStruct((3, N_HEADS, BLK, BLK), F32),
                   jax.ShapeDtypeStruct((3, N_HEADS, BLK, BLK), F32)),
        in_specs=[pl.BlockSpec(memory_space=pltpu.SMEM),
                  pl.BlockSpec(memory_space=pltpu.VMEM)],
        out_specs=(pl.BlockSpec(memory_space=pltpu.VMEM),
                   pl.BlockSpec(memory_space=pltpu.VMEM)),
        name="bias_tiles",
    )(rel_bias, jnp.asarray(_bucket_tables()))


A_QLAT = N_HEADS * A_LATENT
A_QIDX = IDX_HEADS * IDX_DIM
A_WT_ROWS = A_QLAT + A_QIDX + 16


def _a_proj_kernel(x_ref, g_ref, wt_ref, wn_ref, lg_ref,
                   qr_ref, qir_ref, wto_ref, kidx_ref, ckv_ref, ckvt_ref,
                   h_ref, ht_ref):
    tm = x_ref.shape[0]
    hf = _rms(x_ref[...], g_ref[...])
    h_ref[...] = hf.astype(BF16)
    ht_ref[...] = hf.T.astype(BF16)
    for c in range(A_QLAT // 512):
        r = _dot(wt_ref[c * 512:(c + 1) * 512, :], ht_ref[...]).astype(BF16)
        for hh in range(4):
            h = 4 * c + hh
            for blk in range(tm // BLK):
                qr_ref[blk, :, h * BLK:(h + 1) * BLK] = (
                    r[hh * A_LATENT:(hh + 1) * A_LATENT, blk * BLK:(blk + 1) * BLK])
    r = _dot(wt_ref[A_QLAT:A_QLAT + A_QIDX, :], ht_ref[...]).astype(BF16)
    for j in range(IDX_HEADS):
        for blk in range(tm // BLK):
            qir_ref[blk, :, j * BLK:(j + 1) * BLK] = (
                r[j * IDX_DIM:(j + 1) * IDX_DIM, blk * BLK:(blk + 1) * BLK])
    w = _dot(wt_ref[A_QLAT + A_QIDX:, :], ht_ref[...])
    wto_ref[...] = w[:IDX_HEADS] * (IDX_HEADS ** -0.5)
    t = _dot(h_ref[...], wn_ref[...])
    ckv = _rms(t[:, :A_LATENT], lg_ref[...])
    ckv_ref[...] = ckv.astype(BF16)
    kidx_ref[...] = t[:, A_LATENT:A_LATENT + IDX_DIM].astype(BF16)
    for b2 in range(tm // SC):
        ckvt_ref[b2] = ckv[b2 * SC:(b2 + 1) * SC, :].T.astype(BF16)


def _a_proj(x2, g, wt, wn, lg, tm=512):
    n, d = x2.shape
    row = lambda i: (i, 0)
    blk3 = lambda i: (i, 0, 0)
    return pl.pallas_call(
        _a_proj_kernel,
        grid=(n // tm,),
        in_specs=[pl.BlockSpec((tm, d), row),
                  _resident((1, d)),
                  _resident((A_WT_ROWS, d)),
                  _resident((d, 256)),
                  _resident((1, A_LATENT))],
        out_specs=(pl.BlockSpec((tm // BLK, A_LATENT, A_QLAT), blk3),
                   pl.BlockSpec((tm // BLK, IDX_DIM, IDX_HEADS * BLK), blk3),
                   pl.BlockSpec((IDX_HEADS, tm), lambda i: (0, i)),
                   pl.BlockSpec((tm, IDX_DIM), row),
                   pl.BlockSpec((tm, A_LATENT), row),
                   pl.BlockSpec((tm // SC, A_LATENT, SC), blk3)),
        out_shape=(jax.ShapeDtypeStruct((n // BLK, A_LATENT, A_QLAT), BF16),
                   jax.ShapeDtypeStruct((n // BLK, IDX_DIM, IDX_HEADS * BLK), BF16),
                   jax.ShapeDtypeStruct((IDX_HEADS, n), F32),
                   jax.ShapeDtypeStruct((n, IDX_DIM), BF16),
                   jax.ShapeDtypeStruct((n, A_LATENT), BF16),
                   jax.ShapeDtypeStruct((n // SC, A_LATENT, SC), BF16)),
        scratch_shapes=[pltpu.VMEM((tm, d), BF16), pltpu.VMEM((d, tm), BF16)],
        compiler_params=_params(1),
        name="a_proj",
    )(x2, g, wt, wn, lg)


def _dsa_kernel(qr_ref, qir_ref, wt_ref, kidx_ref, ckv_ref, ckvt_ref, wuvt_ref,
                btd_ref, tri_ref, out_ref,
                score_ref, madd_ref, acc_ref, l_ref, *, k_sel, n_sc):
    i = pl.program_id(1)
    nsc = i // 2 + 1
    kf = float(k_sel)
    krow = lax.broadcasted_iota(I32, (SC, BLK), 0)
    qpos = i * BLK + lax.broadcasted_iota(I32, (SC, BLK), 1)

    def causal(sc):
        return sc * SC + krow <= qpos

    def key_rows(ref, sc, count=1):
        return ref[pl.ds(pl.multiple_of(sc * SC, SC), count * SC), :]

    def colsum(x):
        return jnp.sum(x.reshape(SC // 8, 8, BLK), axis=0)

    w2 = wt_ref[...] * (IDX_DIM ** -0.5)

    def index_group(sc0, count, last=False):
        d = _dot(key_rows(kidx_ref, sc0, count), qir_ref[0])
        for c in range(count):
            rows = slice(c * SC, (c + 1) * SC)
            s = w2[0:1, :] * jnp.maximum(d[rows, 0:BLK], 0.0)
            for j in range(1, IDX_HEADS):
                s = s + w2[j:j + 1, :] * jnp.maximum(d[rows, j * BLK:(j + 1) * BLK], 0.0)
            if last:
                s = jnp.where(causal(sc0 + c), s, NEG_INF)
            score_ref[sc0 + c] = s

    def index_body(t, carry):
        index_group(2 * t, 2)
        return carry
    lax.fori_loop(0, (nsc - 1) // 2, index_body, 0)

    @pl.when((nsc - 1) % 2 == 1)
    def _():
        index_group(nsc - 2, 1)
    index_group(nsc - 1, 1, last=True)

    @pl.when((i + 1) * BLK <= k_sel)
    def _():
        def body(sc, carry):
            madd_ref[sc] = jnp.where(causal(sc), 0.0, NEG_INF)
            return carry
        lax.fori_loop(0, nsc, body, 0)

    def key_to_float(key):
        return lax.bitcast_convert_type(key ^ ((key >> 31) & 0x7FFFFFFF), F32)

    def search(nblk):
        def count_ge(cand_key):
            cb = jnp.broadcast_to(key_to_float(cand_key), (8, BLK))
            parts = [jnp.zeros((8, BLK), F32)] * 8
            for t in range(nblk * BLK // 8):
                sc, r = divmod(t * 8, SC)
                ge = score_ref[sc, r:r + 8, :] >= cb
                parts[t % 8] = parts[t % 8] + jnp.where(ge, 1.0, 0.0)
            part = sum(parts[1:], parts[0])
            return jnp.sum(part, axis=0, keepdims=True)

        ans = jnp.where(count_ge(jnp.zeros((1, BLK), I32)) >= kf, 0, INT_MIN)

        def bit_body(it, ans):
            cand = ans | jnp.left_shift(1, 30 - it)
            return jnp.where(count_ge(cand) >= kf, cand, ans)
        return key_to_float(lax.fori_loop(0, 31, bit_body, ans))

    @pl.when((i + 1) * BLK > k_sel)
    def _():
        for nblk in range(k_sel // BLK + 1, 2 * n_sc + 1):
            @pl.when(i + 1 == nblk)
            def _(nblk=nblk):
                score_ref[n_sc] = jnp.broadcast_to(search(nblk), (SC, BLK))
        thr = score_ref[n_sc][0:1, :]

        def ge_body(sc, part):
            ge = score_ref[sc] >= thr
            madd_ref[sc] = jnp.where(ge, 0.0, NEG_INF)
            return part + colsum(jnp.where(ge, 1.0, 0.0))
        n_ge = jnp.sum(lax.fori_loop(0, nsc, ge_body, jnp.zeros((8, BLK), F32)),
                       axis=0, keepdims=True)

        @pl.when(jnp.max(n_ge) > kf)
        def _():
            def gt_body(sc, part):
                return part + colsum(jnp.where(score_ref[sc] > thr, 1.0, 0.0))
            n_gt = jnp.sum(lax.fori_loop(0, nsc, gt_body, jnp.zeros((8, BLK), F32)),
                           axis=0, keepdims=True)
            need = kf - n_gt

            def mask_body(sc, off):
                score = score_ref[sc]
                eq = score == thr
                eqf = jnp.where(eq, 1.0, 0.0)
                before = _dot(tri_ref[...], eqf.astype(BF16)) + off
                sel = (score > thr) | (eq & (before < need))
                madd_ref[sc] = jnp.where(sel, 0.0, NEG_INF)
                return off + jnp.sum(eqf, axis=0, keepdims=True)
            lax.fori_loop(0, nsc, mask_body, jnp.zeros((1, BLK), F32))

    c1 = (A_LATENT ** -0.5) * LOG2E
    hcols = N_HEADS * BLK

    def logits_group(sc0, m, count, near):
        lt = _dot(key_rows(ckv_ref, sc0, count), qr_ref[0])
        madds = [madd_ref[sc0 + c] for c in range(count)]
        new_m = []
        for h in range(N_HEADS):
            hs = slice(h * BLK, (h + 1) * BLK)
            mh = m[:, hs]
            for c in range(count):
                v = lt[c * SC:(c + 1) * SC, hs] * c1 + madds[c]
                if near:
                    halves = []
                    for half in range(2):
                        kind = jnp.clip(i - (2 * (sc0 + c) + half), 0, 2)
                        halves.append(v[half * BLK:(half + 1) * BLK] + btd_ref[kind, h])
                    v = jnp.concatenate(halves, axis=0)
                l_ref[sc0 + c, h] = v
                mh = jnp.maximum(mh, jnp.max(v, axis=0, keepdims=True))
            new_m.append(mh)
        return jnp.concatenate(new_m, axis=1)

    n_far = nsc - 2
    m = jnp.full((1, hcols), NEG_INF, F32)
    m = lax.fori_loop(0, n_far // 2, lambda t, m: logits_group(2 * t, m, 2, False), m)
    m = lax.cond((n_far > 0) & (n_far % 2 == 1),
                 lambda m: logits_group(n_far - 1, m, 1, False), lambda m: m, m)
    m = lax.cond(nsc >= 2, lambda m: logits_group(nsc - 2, m, 2, True),
                 lambda m: logits_group(0, m, 1, True), m)

    def pv_group(sc0, s, count):
        vt = jnp.concatenate([ckvt_ref[sc0 + c] for c in range(count)], axis=1)
        new_s = []
        for hp in range(N_HEADS // 2):
            cols = []
            for h in (2 * hp, 2 * hp + 1):
                hs = slice(h * BLK, (h + 1) * BLK)
                sh = s[:, hs]
                ps = []
                for c in range(count):
                    p = jnp.exp2(l_ref[sc0 + c, h] - m[:, hs])
                    sh = sh + colsum(p)
                    ps.append(p.astype(BF16))
                new_s.append(sh)
                cols.append(jnp.concatenate(ps, axis=0))
            pair = jnp.concatenate(cols, axis=1)
            acc_ref[hp] = acc_ref[hp] + _dot(vt, pair)
        return jnp.concatenate(new_s, axis=1)

    acc_ref[...] = jnp.zeros(acc_ref.shape, F32)
    s = lax.fori_loop(0, nsc // 2, lambda t, s: pv_group(2 * t, s, 2),
                      jnp.zeros((8, hcols), F32))
    s = lax.cond(nsc % 2 == 1, lambda s: pv_group(nsc - 1, s, 1), lambda s: s, s)
    inv = 1.0 / jnp.sum(s, axis=0, keepdims=True)

    outs = []
    for h in range(N_HEADS):
        hs = slice(h * BLK, (h + 1) * BLK)
        a = acc_ref[h // 2, :, (h % 2) * BLK:(h % 2 + 1) * BLK]
        o = (a * inv[:, hs]).astype(BF16)
        outs.append(_dot(wuvt_ref[h], o))
    out_ref[...] = jnp.concatenate(outs, axis=0).T.astype(BF16)


def _dsa_attn(qr, qir, wt, kidx, ckv, ckvt, wuvt, btd, batch, seq):
    nblk = seq // BLK
    n_sc = seq // SC
    k_sel = min(TOPK_MAX, seq // 4)
    hcols = N_HEADS * BLK
    tri = jnp.asarray(np.tril(np.ones((SC, SC), np.float32), -1), BF16)
    return pl.pallas_call(
        functools.partial(_dsa_kernel, k_sel=k_sel, n_sc=n_sc),
        grid=(batch, nblk),
        in_specs=[pl.BlockSpec((1, A_LATENT, hcols), lambda b, i: (b * nblk + i, 0, 0)),
                  pl.BlockSpec((1, IDX_DIM, IDX_HEADS * BLK), lambda b, i: (b * nblk + i, 0, 0)),
                  pl.BlockSpec((IDX_HEADS, BLK), lambda b, i: (0, b * nblk + i)),
                  pl.BlockSpec((seq, IDX_DIM), lambda b, i: (b, 0)),
                  pl.BlockSpec((seq, A_LATENT), lambda b, i: (b, 0)),
                  pl.BlockSpec((n_sc, A_LATENT, SC), lambda b, i: (b, 0, 0)),
                  _resident((N_HEADS, A_VHEAD, A_LATENT)),
                  _resident((3, N_HEADS, BLK, BLK)),
                  _resident((SC, SC))],
        out_specs=pl.BlockSpec((BLK, N_HEADS * A_VHEAD), lambda b, i: (b * nblk + i, 0)),
        out_shape=jax.ShapeDtypeStruct((batch * seq, N_HEADS * A_VHEAD), BF16),
        scratch_shapes=[
            pltpu.VMEM((n_sc + 1, SC, BLK), F32),
            pltpu.VMEM((n_sc, SC, BLK), F32),
            pltpu.VMEM((N_HEADS // 2, A_LATENT, 2 * BLK), F32),
            pltpu.VMEM((n_sc, N_HEADS, SC, BLK), F32),
        ],
        compiler_params=_params(2),
        name="dsa_attn",
    )(qr, qir, wt, kidx, ckv, ckvt, wuvt, btd, tri)


B_Q = N_HEADS * B_HEAD_DIM
B_KV = B_KV_HEADS * B_HEAD_DIM


def _b_proj_kernel(x_ref, g_ref, wt_ref, bcol_ref, wk_ref, bk_ref,
                   qr_ref, kk_ref, vt_ref, h_ref, ht_ref):
    tm = x_ref.shape[0]
    hf = _rms(x_ref[...], g_ref[...])
    h_ref[...] = hf.astype(BF16)
    ht_ref[...] = hf.T.astype(BF16)
    heads_per_dot = 512 // B_HEAD_DIM
    for c in range(B_Q // 512):
        rows = slice(c * 512, (c + 1) * 512)
        r = (_dot(wt_ref[rows, :], ht_ref[...]) + bcol_ref[rows, :]).astype(BF16)
        for hh in range(heads_per_dot):
            h = c * heads_per_dot + hh
            for blk in range(tm // BLK):
                qr_ref[blk, :, h * BLK:(h + 1) * BLK] = (
                    r[hh * B_HEAD_DIM:(hh + 1) * B_HEAD_DIM, blk * BLK:(blk + 1) * BLK])
    vt_ref[...] = (_dot(wt_ref[B_Q:, :], ht_ref[...]) + bcol_ref[B_Q:, :]).astype(BF16)
    t = _dot(h_ref[...], wk_ref[...]) + bk_ref[...]
    for kv in range(B_KV_HEADS):
        kk_ref[kv] = t[:, kv * B_HEAD_DIM:(kv + 1) * B_HEAD_DIM].astype(BF16)


def _b_proj(x2, g, wt, bcol, wk, bk, tm=512):
    n, d = x2.shape
    return pl.pallas_call(
        _b_proj_kernel,
        grid=(n // tm,),
        in_specs=[pl.BlockSpec((tm, d), lambda i: (i, 0)),
                  _resident((1, d)),
                  _resident((B_Q + B_KV, d)),
                  _resident((B_Q + B_KV, 1)),
                  _resident((d, B_KV)),
                  _resident((1, B_KV))],
        out_specs=(pl.BlockSpec((tm // BLK, B_HEAD_DIM, N_HEADS * BLK), lambda i: (i, 0, 0)),
                   pl.BlockSpec((B_KV_HEADS, tm, B_HEAD_DIM), lambda i: (0, i, 0)),
                   pl.BlockSpec((B_KV, tm), lambda i: (0, i))),
        out_shape=(jax.ShapeDtypeStruct((n // BLK, B_HEAD_DIM, N_HEADS * BLK), BF16),
                   jax.ShapeDtypeStruct((B_KV_HEADS, n, B_HEAD_DIM), BF16),
                   jax.ShapeDtypeStruct((B_KV, n), BF16)),
        scratch_shapes=[pltpu.VMEM((tm, d), BF16), pltpu.VMEM((d, tm), BF16)],
        compiler_params=_params(1),
        name="b_proj",
    )(x2, g, wt, bcol, wk, bk)


def _swa_kernel(sink_ref, qr_ref, kp_ref, kc_ref, vp_ref, vc_ref, bts_ref, out_ref):
    n = pl.program_id(1)
    c1 = (B_HEAD_DIM ** -0.5) * LOG2E
    group = N_HEADS // B_KV_HEADS
    blocks = (
        (qr_ref.at[0], kp_ref, kc_ref.at[:, 0:BLK, :], vp_ref, vc_ref.at[:, 0:BLK],
         jnp.where(n >= 1, 1, 2)),
        (qr_ref.at[1], kc_ref.at[:, 0:BLK, :], kc_ref.at[:, BLK:2 * BLK, :],
         vc_ref.at[:, 0:BLK], vc_ref.at[:, BLK:2 * BLK], 1))
    tasks = [(qb, kv, g2) for qb in range(2) for kv in range(B_KV_HEADS)
             for g2 in range(group // 2)]

    def scores(task):
        qb, kv, g2 = task
        q_ref, kp, kc = blocks[qb][0:3]
        h0 = kv * group + 2 * g2
        q = q_ref[:, h0 * BLK:(h0 + 2) * BLK]
        return _dot(kp[kv], q), _dot(kc[kv], q)

    def attend(task, lp, lc):
        qb, kv, g2 = task
        vp, vc, prev_kind = blocks[qb][3:6]
        vrows = slice(kv * B_HEAD_DIM, (kv + 1) * B_HEAD_DIM)
        pps, pcs, invs = [], [], []
        for j in range(2):
            h = kv * group + 2 * g2 + j
            lanes = slice(j * BLK, (j + 1) * BLK)
            sink = sink_ref[:, h * BLK:(h + 1) * BLK] * LOG2E
            a = lp[:, lanes] * c1 + bts_ref[prev_kind, h]
            b = lc[:, lanes] * c1 + bts_ref[0, h]
            m = jnp.maximum(jnp.max(jnp.maximum(a, b), axis=0, keepdims=True), sink)
            pp = jnp.exp2(a - m)
            pc = jnp.exp2(b - m)
            den = jnp.sum(pp + pc, axis=0, keepdims=True) + jnp.exp2(sink - m)
            invs.append(1.0 / den)
            pps.append(pp.astype(BF16))
            pcs.append(pc.astype(BF16))
        ot = (_dot(vp[vrows, :], jnp.concatenate(pps, axis=1))
              + _dot(vc[vrows, :], jnp.concatenate(pcs, axis=1)))
        return [ot[:, j * BLK:(j + 1) * BLK] * invs[j] for j in range(2)]

    outs = [[], []]
    ahead = 3
    queue = [scores(task) for task in tasks[:ahead]]
    for t, task in enumerate(tasks):
        if t + ahead < len(tasks):
            queue.append(scores(tasks[t + ahead]))
        outs[task[0]] += attend(task, *queue.pop(0))
    for qb in range(2):
        out_ref[qb * BLK:(qb + 1) * BLK, :] = (
            jnp.concatenate(outs[qb], axis=0).T.astype(BF16))


def _swa_attn(qr, kk, vt, sinks_row, bts, batch, seq):
    nb2 = seq // (2 * BLK)
    cur = lambda b, n: b * nb2 + n
    prev = lambda b, n: 2 * (b * nb2 + n) - jnp.minimum(n, 1)
    return pl.pallas_call(
        _swa_kernel,
        grid=(batch, nb2),
        in_specs=[_resident((1, N_HEADS * BLK)),
                  pl.BlockSpec((2, B_HEAD_DIM, N_HEADS * BLK), lambda b, n: (cur(b, n), 0, 0)),
                  pl.BlockSpec((B_KV_HEADS, BLK, B_HEAD_DIM), lambda b, n: (0, prev(b, n), 0)),
                  pl.BlockSpec((B_KV_HEADS, 2 * BLK, B_HEAD_DIM), lambda b, n: (0, cur(b, n), 0)),
                  pl.BlockSpec((B_KV, BLK), lambda b, n: (0, prev(b, n))),
                  pl.BlockSpec((B_KV, 2 * BLK), lambda b, n: (0, cur(b, n))),
                  _resident((3, N_HEADS, BLK, BLK))],
        out_specs=pl.BlockSpec((2 * BLK, B_Q), lambda b, n: (cur(b, n), 0)),
        out_shape=jax.ShapeDtypeStruct((batch * seq, B_Q), BF16),
        compiler_params=_params(2),
        name="swa_attn",
    )(sinks_row, qr, kk, kk, vt, vt, bts)


FF_CHUNK = 256


def _ffn_kernel(x_ref, o_ref, wo_ref, bo_ref, g_ref, wgu_ref, wd_ref, fg_ref,
                out_ref, h_ref, act_ref, *, d_ff, final):
    x1 = x_ref[...] + _dot(o_ref[...], wo_ref[...]) + bo_ref[...]
    out_ref[...] = x1
    h_ref[...] = _rms(x1, g_ref[...]).astype(BF16)
    for c in range(d_ff // FF_CHUNK):
        gate = _dot(h_ref[...], wgu_ref[:, c * FF_CHUNK:(c + 1) * FF_CHUNK])
        up = _dot(h_ref[...], wgu_ref[:, d_ff + c * FF_CHUNK:d_ff + (c + 1) * FF_CHUNK])
        act_ref[:, c * FF_CHUNK:(c + 1) * FF_CHUNK] = (
            gate * jax.nn.sigmoid(gate) * up).astype(BF16)
    y = out_ref[...] + _dot(act_ref[...], wd_ref[...])
    if final:
        y = _rms(y, fg_ref[...])
    out_ref[...] = y


def _ffn(x2, o, wo, bo, g, wgu_all, wd_all, layer, fg, final, tm=1024):
    n, d = x2.shape
    d_ff = wd_all.shape[1]
    row = lambda i: (i, 0)
    pick = lambda i: (layer, 0, 0)
    return pl.pallas_call(
        functools.partial(_ffn_kernel, d_ff=d_ff, final=final),
        grid=(n // tm,),
        in_specs=[pl.BlockSpec((tm, d), row),
                  pl.BlockSpec((tm, o.shape[1]), row),
                  _resident(wo.shape),
                  _resident((1, d)),
                  _resident((1, d)),
                  pl.BlockSpec((None,) + wgu_all.shape[1:], pick, pipeline_mode=pl.Buffered(1)),
                  pl.BlockSpec((None,) + wd_all.shape[1:], pick, pipeline_mode=pl.Buffered(1)),
                  _resident((1, d))],
        out_specs=pl.BlockSpec((tm, d), row),
        out_shape=jax.ShapeDtypeStruct((n, d), F32),
        scratch_shapes=[pltpu.VMEM((tm, d), BF16),
                        pltpu.VMEM((tm, d_ff), BF16)],
        compiler_params=_params(1),
        name="ffn",
    )(x2, o, wo, bo, g, wgu_all, wd_all, fg)


def kernel(x, rel_bias, attn_norm, ffn_norm, final_norm, a_w_in, a_latent_norm, a_w_uv, a_w_out, b_w_in, b_b_in, b_sinks, b_w_out, b_b_out, ffn_w_gate_up, ffn_w_down):
    batch, seq, d = x.shape
    x2 = x.reshape(batch * seq, d)
    bts, btd = _bias_tiles(rel_bias)
    fg = final_norm.reshape(1, d)
    wgu_all = ffn_w_gate_up.astype(BF16)
    wd_all = ffn_w_down.astype(BF16)

    wa = a_w_in[0].astype(BF16)
    o_ckv, o_qidx, o_kidx, o_widx = A_QLAT, A_QLAT + A_LATENT, A_QLAT + A_LATENT + A_QIDX, \
        A_QLAT + A_LATENT + A_QIDX + IDX_DIM
    wt = jnp.concatenate(
        [wa[:, :o_ckv], wa[:, o_qidx:o_kidx], wa[:, o_widx:], jnp.zeros((d, 16 - IDX_HEADS), BF16)],
        axis=1).T
    wn = jnp.concatenate(
        [wa[:, o_ckv:o_qidx], wa[:, o_kidx:o_widx], jnp.zeros((d, 256 - A_LATENT - IDX_DIM), BF16)],
        axis=1)
    qr, qir, wto, kidx, ckv, ckvt = _a_proj(
        x2, attn_norm[0].reshape(1, d), wt, wn, a_latent_norm[0].reshape(1, A_LATENT))
    wuvt = jnp.swapaxes(a_w_uv[0], 1, 2).astype(BF16)
    o = _dsa_attn(qr, qir, wto, kidx, ckv, ckvt, wuvt, btd, batch, seq)
    x2 = _ffn(x2, o, a_w_out[0].astype(BF16), jnp.zeros((1, d), F32),
              ffn_norm[0].reshape(1, d), wgu_all, wd_all, 0, fg, final=False)

    wb, bb = b_w_in[0].astype(BF16), b_b_in[0]
    wqv = jnp.concatenate([wb[:, :B_Q], wb[:, B_Q + B_KV:]], axis=1).T
    bqv = jnp.concatenate([bb[:B_Q], bb[B_Q + B_KV:]]).reshape(-1, 1)
    qr, kk, vt = _b_proj(
        x2, attn_norm[1].reshape(1, d), wqv, bqv, wb[:, B_Q:B_Q + B_KV],
        bb[B_Q:B_Q + B_KV].reshape(1, B_KV))
    sinks_row = jnp.repeat(b_sinks[0], BLK).reshape(1, N_HEADS * BLK)
    o = _swa_attn(qr, kk, vt, sinks_row, bts, batch, seq)
    x2 = _ffn(x2, o, b_w_out[0].astype(BF16), b_b_out[0].reshape(1, d),
              ffn_norm[1].reshape(1, d), wgu_all, wd_all, 1, fg, final=True)
    return x2.reshape(batch, seq, d)
```

```python
import functools
import math

import numpy as np
import jax
import jax.numpy as jnp
from jax import lax
from jax.experimental import pallas as pl
from jax.experimental.pallas import tpu as pltpu

F32 = jnp.float32
BF16 = jnp.bfloat16
I32 = jnp.int32

EPS = 1e-6
N_HEADS = 16
A_LATENT = 128
A_VHEAD = 64
IDX_HEADS = 8
IDX_DIM = 64
TOPK_MAX = 256
B_KV_HEADS = 2
B_HEAD_DIM = 64
REL_BUCKETS = 32
REL_MAX_DIST = 128

BLK = 128
SC = 256
INT_MIN = -(2 ** 31)
NEG_INF = float("-inf")
LOG2E = math.log2(math.e)
VMEM_LIMIT = 56 * 1024 * 1024


def _rms(xf, g):
    ms = jnp.mean(xf * xf, axis=-1, keepdims=True)
    return xf * lax.rsqrt(ms + EPS) * g


def _dot(a, b):
    return jnp.dot(a, b, preferred_element_type=F32)


def _dot_nt(a, b):
    return lax.dot_general(a, b, (((1,), (1,)), ((), ())), preferred_element_type=F32)


def _params(n_axes):
    return pltpu.CompilerParams(
        dimension_semantics=("arbitrary",) * n_axes, vmem_limit_bytes=VMEM_LIMIT)


def _resident(shape):
    zeros = (0,) * len(shape)
    return pl.BlockSpec(shape, lambda *_: zeros, pipeline_mode=pl.Buffered(1))


def _bucket(rel):
    n = np.maximum(rel, 0)
    max_exact = REL_BUCKETS // 2
    nf = np.maximum(n, max_exact).astype(np.float64)
    large = max_exact + (np.log(nf / max_exact) / math.log(REL_MAX_DIST / max_exact)
                         * (REL_BUCKETS - max_exact)).astype(np.int32)
    large = np.minimum(large, REL_BUCKETS - 1)
    return np.where(n < max_exact, n, large).astype(np.int32)


def _bucket_tables():
    k = np.arange(BLK)[:, None]
    q = np.arange(BLK)[None, :]
    return _bucket(np.stack([q - k, BLK + q - k]))


def _bias_kernel(rb_ref, bidx_ref, bts_ref, btd_ref):
    k = lax.broadcasted_iota(I32, (BLK, BLK), 0)
    q = lax.broadcasted_iota(I32, (BLK, BLK), 1)
    window = (k <= q, k > q)
    for h in range(N_HEADS):
        for kind in range(2):
            bidx = bidx_ref[kind]

            t = jnp.zeros((BLK, BLK), F32)
            for b in range(REL_BUCKETS):
                t = jnp.where(bidx == b, rb_ref[b, h], t)
            bts_ref[kind, h] = jnp.where(window[kind], t * LOG2E, NEG_INF)
            btd_ref[kind, h] = (t - rb_ref[REL_BUCKETS - 1, h]) * LOG2E
        bts_ref[2, h] = jnp.full((BLK, BLK), NEG_INF, F32)
        btd_ref[2, h] = jnp.zeros((BLK, BLK), F32)


def _bias_tiles(rel_bias):
    return pl.pallas_call(
        _bias_kernel,
        out_shape=(jax.ShapeDtypeStruct((3, N_HEADS, BLK, BLK), F32),
                   jax.ShapeDtypeStruct((3, N_HEADS, BLK, BLK), F32)),
        in_specs=[pl.BlockSpec(memory_space=pltpu.SMEM),
                  pl.BlockSpec(memory_space=pltpu.VMEM)],
        out_specs=(pl.BlockSpec(memory_space=pltpu.VMEM),
                   pl.BlockSpec(memory_space=pltpu.VMEM)),
        name="bias_tiles",
    )(rel_bias, jnp.asarray(_bucket_tables()))


A_QLAT = N_HEADS * A_LATENT
A_QIDX = IDX_HEADS * IDX_DIM
A_WT_ROWS = A_QLAT + A_QIDX + 16


def _a_proj_kernel(x_ref, g_ref, wt_ref, wn_ref, lg_ref, wgu_ref, wd_ref,
                   qr_ref, qir_ref, wto_ref, kidx_ref, ckv_ref, ckvt_ref,
                   wgu_bf_ref, wd_bf_ref, h_ref, ht_ref):
    tm = x_ref.shape[0]
    wgu_bf_ref[...] = wgu_ref[...].astype(BF16)
    wd_bf_ref[...] = wd_ref[...].astype(BF16)
    hf = _rms(x_ref[...], g_ref[...])
    h_ref[...] = hf.astype(BF16)
    ht_ref[...] = hf.T.astype(BF16)
    for c in range(A_QLAT // 512):
        r = _dot(wt_ref[c * 512:(c + 1) * 512, :], ht_ref[...]).astype(BF16)
        for hh in range(4):
            h = 4 * c + hh
            for blk in range(tm // BLK):
                qr_ref[blk, :, h * BLK:(h + 1) * BLK] = (
                    r[hh * A_LATENT:(hh + 1) * A_LATENT, blk * BLK:(blk + 1) * BLK])
    r = _dot(wt_ref[A_QLAT:A_QLAT + A_QIDX, :], ht_ref[...]).astype(BF16)
    for j in range(IDX_HEADS):
        for blk in range(tm // BLK):
            qir_ref[blk, :, j * BLK:(j + 1) * BLK] = (
                r[j * IDX_DIM:(j + 1) * IDX_DIM, blk * BLK:(blk + 1) * BLK])
    w = _dot(wt_ref[A_QLAT + A_QIDX:, :], ht_ref[...])
    wto_ref[...] = w[:IDX_HEADS] * (IDX_HEADS ** -0.5)
    t = _dot(h_ref[...], wn_ref[...])
    ckv = _rms(t[:, :A_LATENT], lg_ref[...])
    ckv_ref[...] = ckv.astype(BF16)
    kidx_ref[...] = t[:, A_LATENT:A_LATENT + IDX_DIM].astype(BF16)
    for b2 in range(tm // SC):
        ckvt_ref[b2] = ckv[b2 * SC:(b2 + 1) * SC, :].T.astype(BF16)


def _a_proj(x2, g, wt, wn, lg, wgu2, wd2, tm=512):
    n, d = x2.shape
    steps = n // tm
    gu_rows, dn_rows = wgu2.shape[0] // steps, wd2.shape[0] // steps
    row = lambda i: (i, 0)
    blk3 = lambda i: (i, 0, 0)
    return pl.pallas_call(
        _a_proj_kernel,
        grid=(steps,),
        in_specs=[pl.BlockSpec((tm, d), row),
                  _resident((1, d)),
                  _resident((A_WT_ROWS, d)),
                  _resident((d, 256)),
                  _resident((1, A_LATENT)),
                  pl.BlockSpec((gu_rows, wgu2.shape[1]), row),
                  pl.BlockSpec((dn_rows, wd2.shape[1]), row)],
        out_specs=(pl.BlockSpec((tm // BLK, A_LATENT, A_QLAT), blk3),
                   pl.BlockSpec((tm // BLK, IDX_DIM, IDX_HEADS * BLK), blk3),
                   pl.BlockSpec((IDX_HEADS, tm), lambda i: (0, i)),
                   pl.BlockSpec((tm, IDX_DIM), row),
                   pl.BlockSpec((tm, A_LATENT), row),
                   pl.BlockSpec((tm // SC, A_LATENT, SC), blk3),
                   pl.BlockSpec((gu_rows, wgu2.shape[1]), row),
                   pl.BlockSpec((dn_rows, wd2.shape[1]), row)),
        out_shape=(jax.ShapeDtypeStruct((n // BLK, A_LATENT, A_QLAT), BF16),
                   jax.ShapeDtypeStruct((n // BLK, IDX_DIM, IDX_HEADS * BLK), BF16),
                   jax.ShapeDtypeStruct((IDX_HEADS, n), F32),
                   jax.ShapeDtypeStruct((n, IDX_DIM), BF16),
                   jax.ShapeDtypeStruct((n, A_LATENT), BF16),
                   jax.ShapeDtypeStruct((n // SC, A_LATENT, SC), BF16),
                   jax.ShapeDtypeStruct(wgu2.shape, BF16),
                   jax.ShapeDtypeStruct(wd2.shape, BF16)),
        scratch_shapes=[pltpu.VMEM((tm, d), BF16), pltpu.VMEM((d, tm), BF16)],
        compiler_params=_params(1),
        name="a_proj",
    )(x2, g, wt, wn, lg, wgu2, wd2)


def _dsa_kernel(qr_ref, qir_ref, wt_ref, kidx_ref, ckv_ref, ckvt_ref, wuvt_ref,
                btd_ref, tri_ref, out_ref,
                score_ref, madd_ref, acc_ref, l_ref, *, k_sel, n_sc):
    i = pl.program_id(1)
    nsc = i // 2 + 1
    kf = float(k_sel)
    krow = lax.broadcasted_iota(I32, (SC, BLK), 0)
    qpos = i * BLK + lax.broadcasted_iota(I32, (SC, BLK), 1)

    def causal(sc):
        return sc * SC + krow <= qpos

    def key_rows(ref, sc, count=1):
        return ref[pl.ds(pl.multiple_of(sc * SC, SC), count * SC), :]

    def colsum(x):
        return jnp.sum(x.reshape(SC // 8, 8, BLK), axis=0)

    w2 = wt_ref[...] * (IDX_DIM ** -0.5)

    def index_group(sc0, count, last=False):
        d = _dot(key_rows(kidx_ref, sc0, count), qir_ref[0])
        for c in range(count):
            rows = slice(c * SC, (c + 1) * SC)
            s = w2[0:1, :] * jnp.maximum(d[rows, 0:BLK], 0.0)
            for j in range(1, IDX_HEADS):
                s = s + w2[j:j + 1, :] * jnp.maximum(d[rows, j * BLK:(j + 1) * BLK], 0.0)
            if last:
                s = jnp.where(causal(sc0 + c), s, NEG_INF)
            score_ref[sc0 + c] = s

    def index_body(t, carry):
        index_group(2 * t, 2)
        return carry
    lax.fori_loop(0, (nsc - 1) // 2, index_body, 0)

    @pl.when((nsc - 1) % 2 == 1)
    def _():
        index_group(nsc - 2, 1)
    index_group(nsc - 1, 1, last=True)

    @pl.when((i + 1) * BLK <= k_sel)
    def _():
        def body(sc, carry):
            madd_ref[sc] = jnp.where(causal(sc), 0.0, NEG_INF)
            return carry
        lax.fori_loop(0, nsc, body, 0)

    def key_to_float(key):
        return lax.bitcast_convert_type(key ^ ((key >> 31) & 0x7FFFFFFF), F32)

    def search(nblk):
        def count_ge(cand):
            cb = jnp.broadcast_to(cand, (8, BLK))
            parts = [jnp.zeros((8, BLK), F32)] * 8
            for t in range(nblk * BLK // 8):
                sc, r = divmod(t * 8, SC)
                ge = score_ref[sc, r:r + 8, :] >= cb
                parts[t % 8] = parts[t % 8] + jnp.where(ge, 1.0, 0.0)
            part = sum(parts[1:], parts[0])
            return jnp.sum(part, axis=0, keepdims=True)

        take = count_ge(jnp.zeros((1, BLK), F32)) >= kf
        ans = jnp.where(take, 0, INT_MIN)
        first = jnp.left_shift(1, 30)
        cand_f = jnp.where(take, key_to_float(first), key_to_float(INT_MIN | first))

        def bit_body(it, carry):
            ans, cand_f = carry
            cand = ans | jnp.left_shift(1, 30 - it)
            nxt = jnp.left_shift(1, jnp.maximum(29 - it, 0))
            f_take, f_keep = key_to_float(cand | nxt), key_to_float(ans | nxt)
            take = count_ge(cand_f) >= kf
            return jnp.where(take, cand, ans), jnp.where(take, f_take, f_keep)
        ans, _ = lax.fori_loop(0, 31, bit_body, (ans, cand_f))
        return key_to_float(ans)

    @pl.when((i + 1) * BLK > k_sel)
    def _():
        for nblk in range(k_sel // BLK + 1, 2 * n_sc + 1):
            @pl.when(i + 1 == nblk)
            def _(nblk=nblk):
                score_ref[n_sc] = jnp.broadcast_to(search(nblk), (SC, BLK))
        thr = score_ref[n_sc][0:1, :]

        def ge_body(sc, part):
            ge = score_ref[sc] >= thr
            madd_ref[sc] = jnp.where(ge, 0.0, NEG_INF)
            return part + colsum(jnp.where(ge, 1.0, 0.0))
        n_ge = jnp.sum(lax.fori_loop(0, nsc, ge_body, jnp.zeros((8, BLK), F32)),
                       axis=0, keepdims=True)

        @pl.when(jnp.max(n_ge) > kf)
        def _():
            def gt_body(sc, part):
                return part + colsum(jnp.where(score_ref[sc] > thr, 1.0, 0.0))
            n_gt = jnp.sum(lax.fori_loop(0, nsc, gt_body, jnp.zeros((8, BLK), F32)),
                           axis=0, keepdims=True)
            need = kf - n_gt

            def mask_body(sc, off):
                score = score_ref[sc]
                eq = score == thr
                eqf = jnp.where(eq, 1.0, 0.0)
                before = _dot(tri_ref[...], eqf.astype(BF16)) + off
                sel = (score > thr) | (eq & (before < need))
                madd_ref[sc] = jnp.where(sel, 0.0, NEG_INF)
                return off + jnp.sum(eqf, axis=0, keepdims=True)
            lax.fori_loop(0, nsc, mask_body, jnp.zeros((1, BLK), F32))

    c1 = (A_LATENT ** -0.5) * LOG2E
    hcols = N_HEADS * BLK

    def logits_group(sc0, m, count, near):
        lt = _dot(key_rows(ckv_ref, sc0, count), qr_ref[0])
        madds = [madd_ref[sc0 + c] for c in range(count)]
        new_m = []
        for h in range(N_HEADS):
            hs = slice(h * BLK, (h + 1) * BLK)
            mh = m[:, hs]
            for c in range(count):
                v = lt[c * SC:(c + 1) * SC, hs] * c1 + madds[c]
                if near:
                    halves = []
                    for half in range(2):
                        kind = jnp.clip(i - (2 * (sc0 + c) + half), 0, 2)
                        halves.append(v[half * BLK:(half + 1) * BLK] + btd_ref[kind, h])
                    v = jnp.concatenate(halves, axis=0)
                l_ref[sc0 + c, h] = v
                mh = jnp.maximum(mh, jnp.max(v, axis=0, keepdims=True))
            new_m.append(mh)
        return jnp.concatenate(new_m, axis=1)

    n_far = nsc - 2
    m = jnp.full((1, hcols), NEG_INF, F32)
    m = lax.fori_loop(0, n_far // 2, lambda t, m: logits_group(2 * t, m, 2, False), m)
    m = lax.cond((n_far > 0) & (n_far % 2 == 1),
                 lambda m: logits_group(n_far - 1, m, 1, False), lambda m: m, m)
    m = lax.cond(nsc >= 2, lambda m: logits_group(nsc - 2, m, 2, True),
                 lambda m: logits_group(0, m, 1, True), m)

    def pv_group(sc0, s, count):
        vt = jnp.concatenate([ckvt_ref[sc0 + c] for c in range(count)], axis=1)
        new_s = []
        for hp in range(N_HEADS // 2):
            cols = []
            for h in (2 * hp, 2 * hp + 1):
                hs = slice(h * BLK, (h + 1) * BLK)
                sh = s[:, hs]
                ps = []
                for c in range(count):
                    p = jnp.exp2(l_ref[sc0 + c, h] - m[:, hs])
                    sh = sh + colsum(p)
                    ps.append(p.astype(BF16))
                new_s.append(sh)
                cols.append(jnp.concatenate(ps, axis=0))
            pair = jnp.concatenate(cols, axis=1)
            acc_ref[hp] = acc_ref[hp] + _dot(vt, pair)
        return jnp.concatenate(new_s, axis=1)

    acc_ref[...] = jnp.zeros(acc_ref.shape, F32)
    s = lax.fori_loop(0, nsc // 2, lambda t, s: pv_group(2 * t, s, 2),
                      jnp.zeros((8, hcols), F32))
    s = lax.cond(nsc % 2 == 1, lambda s: pv_group(nsc - 1, s, 1), lambda s: s, s)
    inv = 1.0 / jnp.sum(s, axis=0, keepdims=True)

    outs = []
    for h in range(N_HEADS):
        hs = slice(h * BLK, (h + 1) * BLK)
        a = acc_ref[h // 2, :, (h % 2) * BLK:(h % 2 + 1) * BLK]
        o = (a * inv[:, hs]).astype(BF16)
        outs.append(_dot(wuvt_ref[h], o))
    out_ref[...] = jnp.concatenate(outs, axis=0).T.astype(BF16)


def _dsa_attn(qr, qir, wt, kidx, ckv, ckvt, wuvt, btd, batch, seq):
    nblk = seq // BLK
    n_sc = seq // SC
    k_sel = min(TOPK_MAX, seq // 4)
    hcols = N_HEADS * BLK
    tri = jnp.asarray(np.tril(np.ones((SC, SC), np.float32), -1), BF16)
    return pl.pallas_call(
        functools.partial(_dsa_kernel, k_sel=k_sel, n_sc=n_sc),
        grid=(batch, nblk),
        in_specs=[pl.BlockSpec((1, A_LATENT, hcols), lambda b, i: (b * nblk + i, 0, 0)),
                  pl.BlockSpec((1, IDX_DIM, IDX_HEADS * BLK), lambda b, i: (b * nblk + i, 0, 0)),
                  pl.BlockSpec((IDX_HEADS, BLK), lambda b, i: (0, b * nblk + i)),
                  pl.BlockSpec((seq, IDX_DIM), lambda b, i: (b, 0)),
                  pl.BlockSpec((seq, A_LATENT), lambda b, i: (b, 0)),
                  pl.BlockSpec((n_sc, A_LATENT, SC), lambda b, i: (b, 0, 0)),
                  _resident((N_HEADS, A_VHEAD, A_LATENT)),
                  _resident((3, N_HEADS, BLK, BLK)),
                  _resident((SC, SC))],
        out_specs=pl.BlockSpec((BLK, N_HEADS * A_VHEAD), lambda b, i: (b * nblk + i, 0)),
        out_shape=jax.ShapeDtypeStruct((batch * seq, N_HEADS * A_VHEAD), BF16),
        scratch_shapes=[
            pltpu.VMEM((n_sc + 1, SC, BLK), F32),
            pltpu.VMEM((n_sc, SC, BLK), F32),
            pltpu.VMEM((N_HEADS // 2, A_LATENT, 2 * BLK), F32),
            pltpu.VMEM((n_sc, N_HEADS, SC, BLK), F32),
        ],
        compiler_params=_params(2),
        name="dsa_attn",
    )(qr, qir, wt, kidx, ckv, ckvt, wuvt, btd, tri)


B_Q = N_HEADS * B_HEAD_DIM
B_KV = B_KV_HEADS * B_HEAD_DIM


def _b_proj_kernel(x_ref, g_ref, wt_ref, bcol_ref, wk_ref, bk_ref,
                   qr_ref, kk_ref, vt_ref, h_ref, ht_ref):
    tm = x_ref.shape[0]
    hf = _rms(x_ref[...], g_ref[...])
    h_ref[...] = hf.astype(BF16)
    ht_ref[...] = hf.T.astype(BF16)
    heads_per_dot = 512 // B_HEAD_DIM
    for c in range(B_Q // 512):
        rows = slice(c * 512, (c + 1) * 512)
        r = (_dot(wt_ref[rows, :], ht_ref[...]) + bcol_ref[rows, :]).astype(BF16)
        for hh in range(heads_per_dot):
            h = c * heads_per_dot + hh
            for blk in range(tm // BLK):
                qr_ref[blk, :, h * BLK:(h + 1) * BLK] = (
                    r[hh * B_HEAD_DIM:(hh + 1) * B_HEAD_DIM, blk * BLK:(blk + 1) * BLK])
    vt_ref[...] = (_dot(wt_ref[B_Q:, :], ht_ref[...]) + bcol_ref[B_Q:, :]).astype(BF16)
    t = _dot(h_ref[...], wk_ref[...]) + bk_ref[...]
    for kv in range(B_KV_HEADS):
        kk_ref[kv] = t[:, kv * B_HEAD_DIM:(kv + 1) * B_HEAD_DIM].astype(BF16)


def _b_proj(x2, g, wt, bcol, wk, bk, tm=512):
    n, d = x2.shape
    return pl.pallas_call(
        _b_proj_kernel,
        grid=(n // tm,),
        in_specs=[pl.BlockSpec((tm, d), lambda i: (i, 0)),
                  _resident((1, d)),
                  _resident((B_Q + B_KV, d)),
                  _resident((B_Q + B_KV, 1)),
                  _resident((d, B_KV)),
                  _resident((1, B_KV))],
        out_specs=(pl.BlockSpec((tm // BLK, B_HEAD_DIM, N_HEADS * BLK), lambda i: (i, 0, 0)),
                   pl.BlockSpec((B_KV_HEADS, tm, B_HEAD_DIM), lambda i: (0, i, 0)),
                   pl.BlockSpec((B_KV, tm), lambda i: (0, i))),
        out_shape=(jax.ShapeDtypeStruct((n // BLK, B_HEAD_DIM, N_HEADS * BLK), BF16),
                   jax.ShapeDtypeStruct((B_KV_HEADS, n, B_HEAD_DIM), BF16),
                   jax.ShapeDtypeStruct((B_KV, n), BF16)),
        scratch_shapes=[pltpu.VMEM((tm, d), BF16), pltpu.VMEM((d, tm), BF16)],
        compiler_params=_params(1),
        name="b_proj",
    )(x2, g, wt, bcol, wk, bk)


def _swa_kernel(sink_ref, qr_ref, kp_ref, kc_ref, vp_ref, vc_ref, bts_ref, out_ref):
    n = pl.program_id(1)
    c1 = (B_HEAD_DIM ** -0.5) * LOG2E
    group = N_HEADS // B_KV_HEADS
    blocks = (
        (qr_ref.at[0], kp_ref, kc_ref.at[:, 0:BLK, :], vp_ref, vc_ref.at[:, 0:BLK],
         jnp.where(n >= 1, 1, 2)),
        (qr_ref.at[1], kc_ref.at[:, 0:BLK, :], kc_ref.at[:, BLK:2 * BLK, :],
         vc_ref.at[:, 0:BLK], vc_ref.at[:, BLK:2 * BLK], 1))
    tasks = [(qb, kv, g2) for qb in range(2) for kv in range(B_KV_HEADS)
             for g2 in range(group // 2)]

    def scores(task):
        qb, kv, g2 = task
        q_ref, kp, kc = blocks[qb][0:3]
        h0 = kv * group + 2 * g2
        q = q_ref[:, h0 * BLK:(h0 + 2) * BLK]
        return _dot(kp[kv], q), _dot(kc[kv], q)

    def attend(task, lp, lc):
        qb, kv, g2 = task
        vp, vc, prev_kind = blocks[qb][3:6]
        vrows = slice(kv * B_HEAD_DIM, (kv + 1) * B_HEAD_DIM)
        pps, pcs, invs = [], [], []
        for j in range(2):
            h = kv * group + 2 * g2 + j
            lanes = slice(j * BLK, (j + 1) * BLK)
            sink = sink_ref[:, h * BLK:(h + 1) * BLK] * LOG2E
            a = lp[:, lanes] * c1 + bts_ref[prev_kind, h]
            b = lc[:, lanes] * c1 + bts_ref[0, h]
            m = jnp.maximum(jnp.max(jnp.maximum(a, b), axis=0, keepdims=True), sink)
            pp = jnp.exp2(a - m)
            pc = jnp.exp2(b - m)
            den = jnp.sum(pp + pc, axis=0, keepdims=True) + jnp.exp2(sink - m)
            invs.append(1.0 / den)
            pps.append(pp.astype(BF16))
            pcs.append(pc.astype(BF16))
        ot = (_dot(vp[vrows, :], jnp.concatenate(pps, axis=1))
              + _dot(vc[vrows, :], jnp.concatenate(pcs, axis=1)))
        return [ot[:, j * BLK:(j + 1) * BLK] * invs[j] for j in range(2)]

    outs = [[], []]
    ahead = 3
    queue = [scores(task) for task in tasks[:ahead]]
    for t, task in enumerate(tasks):
        if t + ahead < len(tasks):
            queue.append(scores(tasks[t + ahead]))
        outs[task[0]] += attend(task, *queue.pop(0))
    for qb in range(2):
        out_ref[qb * BLK:(qb + 1) * BLK, :] = (
            jnp.concatenate(outs[qb], axis=0).T.astype(BF16))


def _swa_attn(qr, kk, vt, sinks_row, bts, batch, seq):
    nb2 = seq // (2 * BLK)
    cur = lambda b, n: b * nb2 + n
    prev = lambda b, n: 2 * (b * nb2 + n) - jnp.minimum(n, 1)
    return pl.pallas_call(
        _swa_kernel,
        grid=(batch, nb2),
        in_specs=[_resident((1, N_HEADS * BLK)),
                  pl.BlockSpec((2, B_HEAD_DIM, N_HEADS * BLK), lambda b, n: (cur(b, n), 0, 0)),
                  pl.BlockSpec((B_KV_HEADS, BLK, B_HEAD_DIM), lambda b, n: (0, prev(b, n), 0)),
                  pl.BlockSpec((B_KV_HEADS, 2 * BLK, B_HEAD_DIM), lambda b, n: (0, cur(b, n), 0)),
                  pl.BlockSpec((B_KV, BLK), lambda b, n: (0, prev(b, n))),
                  pl.BlockSpec((B_KV, 2 * BLK), lambda b, n: (0, cur(b, n))),
                  _resident((3, N_HEADS, BLK, BLK))],
        out_specs=pl.BlockSpec((2 * BLK, B_Q), lambda b, n: (cur(b, n), 0)),
        out_shape=jax.ShapeDtypeStruct((batch * seq, B_Q), BF16),
        compiler_params=_params(2),
        name="swa_attn",
    )(sinks_row, qr, kk, kk, vt, vt, bts)


FF_CHUNK = 256


def _ffn_kernel(x_ref, o_ref, wo_ref, bo_ref, g_ref, wgu_ref, wd_ref, fg_ref,
                out_ref, h_ref, act_ref, *, d_ff, final):
    x1 = x_ref[...] + _dot(o_ref[...], wo_ref[...]) + bo_ref[...]
    out_ref[...] = x1
    h_ref[...] = _rms(x1, g_ref[...]).astype(BF16)
    for c in range(d_ff // FF_CHUNK):
        gate = _dot(h_ref[...], wgu_ref[:, c * FF_CHUNK:(c + 1) * FF_CHUNK])
        up = _dot(h_ref[...], wgu_ref[:, d_ff + c * FF_CHUNK:d_ff + (c + 1) * FF_CHUNK])
        act_ref[:, c * FF_CHUNK:(c + 1) * FF_CHUNK] = (
            gate * jax.nn.sigmoid(gate) * up).astype(BF16)
    y = out_ref[...] + _dot(act_ref[...], wd_ref[...])
    if final:
        y = _rms(y, fg_ref[...])
    out_ref[...] = y


def _ffn(x2, o, wo, bo, g, wgu_all, wd_all, layer, fg, final, tm=1024):
    n, d = x2.shape
    d_ff = wd_all.shape[1]
    row = lambda i: (i, 0)
    pick = lambda i: (layer, 0, 0)
    return pl.pallas_call(
        functools.partial(_ffn_kernel, d_ff=d_ff, final=final),
        grid=(n // tm,),
        in_specs=[pl.BlockSpec((tm, d), row),
                  pl.BlockSpec((tm, o.shape[1]), row),
                  _resident(wo.shape),
                  _resident((1, d)),
                  _resident((1, d)),
                  pl.BlockSpec((None,) + wgu_all.shape[1:], pick, pipeline_mode=pl.Buffered(1)),
                  pl.BlockSpec((None,) + wd_all.shape[1:], pick, pipeline_mode=pl.Buffered(1)),
                  _resident((1, d))],
        out_specs=pl.BlockSpec((tm, d), row),
        out_shape=jax.ShapeDtypeStruct((n, d), F32),
        scratch_shapes=[pltpu.VMEM((tm, d), BF16),
                        pltpu.VMEM((tm, d_ff), BF16)],
        compiler_params=_params(1),
        name="ffn",
    )(x2, o, wo, bo, g, wgu_all, wd_all, fg)


def kernel(x, rel_bias, attn_norm, ffn_norm, final_norm, a_w_in, a_latent_norm, a_w_uv, a_w_out, b_w_in, b_b_in, b_sinks, b_w_out, b_b_out, ffn_w_gate_up, ffn_w_down):
    batch, seq, d = x.shape
    x2 = x.reshape(batch * seq, d)
    bts, btd = _bias_tiles(rel_bias)
    fg = final_norm.reshape(1, d)

    wa = a_w_in[0].astype(BF16)
    o_ckv, o_qidx, o_kidx, o_widx = A_QLAT, A_QLAT + A_LATENT, A_QLAT + A_LATENT + A_QIDX, \
        A_QLAT + A_LATENT + A_QIDX + IDX_DIM
    wt = jnp.concatenate(
        [wa[:, :o_ckv], wa[:, o_qidx:o_kidx], wa[:, o_widx:], jnp.zeros((d, 16 - IDX_HEADS), BF16)],
        axis=1).T
    wn = jnp.concatenate(
        [wa[:, o_ckv:o_qidx], wa[:, o_kidx:o_widx], jnp.zeros((d, 256 - A_LATENT - IDX_DIM), BF16)],
        axis=1)
    gu_shape, dn_shape = ffn_w_gate_up.shape, ffn_w_down.shape
    qr, qir, wto, kidx, ckv, ckvt, wgu_bf, wd_bf = _a_proj(
        x2, attn_norm[0].reshape(1, d), wt, wn, a_latent_norm[0].reshape(1, A_LATENT),
        ffn_w_gate_up.reshape(-1, gu_shape[-1]), ffn_w_down.reshape(-1, dn_shape[-1]))
    wgu_all, wd_all = wgu_bf.reshape(gu_shape), wd_bf.reshape(dn_shape)
    wuvt = jnp.swapaxes(a_w_uv[0], 1, 2).astype(BF16)
    o = _dsa_attn(qr, qir, wto, kidx, ckv, ckvt, wuvt, btd, batch, seq)
    x2 = _ffn(x2, o, a_w_out[0].astype(BF16), jnp.zeros((1, d), F32),
              ffn_norm[0].reshape(1, d), wgu_all, wd_all, 0, fg, final=False)

    wb, bb = b_w_in[0].astype(BF16), b_b_in[0]
    wqv = jnp.concatenate([wb[:, :B_Q], wb[:, B_Q + B_KV:]], axis=1).T
    bqv = jnp.concatenate([bb[:B_Q], bb[B_Q + B_KV:]]).reshape(-1, 1)
    qr, kk, vt = _b_proj(
        x2, attn_norm[1].reshape(1, d), wqv, bqv, wb[:, B_Q:B_Q + B_KV],
        bb[B_Q:B_Q + B_KV].reshape(1, B_KV))
    sinks_row = jnp.repeat(b_sinks[0], BLK).reshape(1, N_HEADS * BLK)
    o = _swa_attn(qr, kk, vt, sinks_row, bts, batch, seq)
    x2 = _ffn(x2, o, b_w_out[0].astype(BF16), b_b_out[0].reshape(1, d),
              ffn_norm[1].reshape(1, d), wgu_all, wd_all, 1, fg, final=True)
    return x2.reshape(batch, seq, d)
```

```python
import functools
import math

import numpy as np
import jax
import jax.numpy as jnp
from jax import lax
from jax.experimental import pallas as pl
from jax.experimental.pallas import tpu as pltpu

F32 = jnp.float32
BF16 = jnp.bfloat16
I32 = jnp.int32

EPS = 1e-6
N_HEADS = 16
A_LATENT = 128
A_VHEAD = 64
IDX_HEADS = 8
IDX_DIM = 64
TOPK_MAX = 256
B_KV_HEADS = 2
B_HEAD_DIM = 64
REL_BUCKETS = 32
REL_MAX_DIST = 128

BLK = 128
SC = 256
INT_MIN = -(2 ** 31)
NEG_INF = float("-inf")
LOG2E = math.log2(math.e)
VMEM_LIMIT = 56 * 1024 * 1024


def _rms(xf, g):
    ms = jnp.mean(xf * xf, axis=-1, keepdims=True)
    return xf * lax.rsqrt(ms + EPS) * g


def _dot(a, b):
    return jnp.dot(a, b, preferred_element_type=F32)


def _dot_nt(a, b):
    return lax.dot_general(a, b, (((1,), (1,)), ((), ())), preferred_element_type=F32)


def _params(n_axes):
    return pltpu.CompilerParams(
        dimension_semantics=("arbitrary",) * n_axes, vmem_limit_bytes=VMEM_LIMIT)


def _resident(shape):
    zeros = (0,) * len(shape)
    return pl.BlockSpec(shape, lambda *_: zeros, pipeline_mode=pl.Buffered(1))


def _bucket(rel):
    n = np.maximum(rel, 0)
    max_exact = REL_BUCKETS // 2
    nf = np.maximum(n, max_exact).astype(np.float64)
    large = max_exact + (np.log(nf / max_exact) / math.log(REL_MAX_DIST / max_exact)
                         * (REL_BUCKETS - max_exact)).astype(np.int32)
    large = np.minimum(large, REL_BUCKETS - 1)
    return np.where(n < max_exact, n, large).astype(np.int32)


def _bucket_tables():
    k = np.arange(BLK)[:, None]
    q = np.arange(BLK)[None, :]
    return _bucket(np.stack([q - k, BLK + q - k]))


def _bias_kernel(rb_ref, bidx_ref, bts_ref, btd_ref):
    k = lax.broadcasted_iota(I32, (BLK, BLK), 0)
    q = lax.broadcasted_iota(I32, (BLK, BLK), 1)
    window = (k <= q, k > q)
    for h in range(N_HEADS):
        for kind in range(2):
            bidx = bidx_ref[kind]

            t = jnp.zeros((BLK, BLK), F32)
            for b in range(REL_BUCKETS):
                t = jnp.where(bidx == b, rb_ref[b, h], t)
            bts_ref[kind, h] = jnp.where(window[kind], t * LOG2E, NEG_INF)
            btd_ref[kind, h] = (t - rb_ref[REL_BUCKETS - 1, h]) * LOG2E
        bts_ref[2, h] = jnp.full((BLK, BLK), NEG_INF, F32)
        btd_ref[2, h] = jnp.zeros((BLK, BLK), F32)


def _bias_tiles(rel_bias):
    return pl.pallas_call(
        _bias_kernel,
        out_shape=(jax.ShapeDtypeStruct((3, N_HEADS, BLK, BLK), F32),
                   jax.ShapeDtypeStruct((3, N_HEADS, BLK, BLK), F32)),
        in_specs=[pl.BlockSpec(memory_space=pltpu.SMEM),
                  pl.BlockSpec(memory_space=pltpu.VMEM)],
        out_specs=(pl.BlockSpec(memory_space=pltpu.VMEM),
                   pl.BlockSpec(memory_space=pltpu.VMEM)),
        name="bias_tiles",
    )(rel_bias, jnp.asarray(_bucket_tables()))


A_QLAT = N_HEADS * A_LATENT
A_QIDX = IDX_HEADS * IDX_DIM
A_QIDX_ROW = A_QLAT + A_LATENT
A_KIDX_ROW = A_QIDX_ROW + A_QIDX
A_IN = A_KIDX_ROW + IDX_DIM + IDX_HEADS
A_KC_ROW = A_KIDX_ROW + IDX_DIM + 16
A_WT_ROWS = A_KC_ROW + 2 * A_LATENT


def _a_proj_kernel(x_ref, g_ref, w_ref, lg_ref, wgu_ref, wd_ref,
                   qr_ref, qir_ref, wto_ref, kidx_ref, ckv_ref, ckvt_ref,
                   wgu_bf_ref, wd_bf_ref, h_ref, ht_ref, wt_ref):
    tm = x_ref.shape[0]

    @pl.when(pl.program_id(0) == 0)
    def _():
        wt_ref[0:A_IN, :] = w_ref[...].astype(BF16)
        wt_ref[A_IN:A_KC_ROW, :] = jnp.zeros((A_KC_ROW - A_IN, wt_ref.shape[1]), BF16)
        wt_ref[A_KC_ROW:A_KC_ROW + A_LATENT, :] = w_ref[A_QLAT:A_QIDX_ROW, :].astype(BF16)
        wt_ref[A_KC_ROW + A_LATENT:A_KC_ROW + A_LATENT + IDX_DIM, :] = (
            w_ref[A_KIDX_ROW:A_KIDX_ROW + IDX_DIM, :].astype(BF16))
        wt_ref[A_KC_ROW + A_LATENT + IDX_DIM:, :] = jnp.zeros(
            (2 * A_LATENT - A_LATENT - IDX_DIM, wt_ref.shape[1]), BF16)

    wgu_bf_ref[...] = wgu_ref[...].astype(BF16)
    wd_bf_ref[...] = wd_ref[...].astype(BF16)
    hf = _rms(x_ref[...], g_ref[...])
    h_ref[...] = hf.astype(BF16)
    ht_ref[...] = hf.T.astype(BF16)
    for c in range(A_QLAT // 512):
        r = _dot(wt_ref[c * 512:(c + 1) * 512, :], ht_ref[...]).astype(BF16)
        for hh in range(4):
            h = 4 * c + hh
            for blk in range(tm // BLK):
                qr_ref[blk, :, h * BLK:(h + 1) * BLK] = (
                    r[hh * A_LATENT:(hh + 1) * A_LATENT, blk * BLK:(blk + 1) * BLK])
    r = _dot(wt_ref[A_QIDX_ROW:A_KIDX_ROW, :], ht_ref[...]).astype(BF16)
    for j in range(IDX_HEADS):
        for blk in range(tm // BLK):
            qir_ref[blk, :, j * BLK:(j + 1) * BLK] = (
                r[j * IDX_DIM:(j + 1) * IDX_DIM, blk * BLK:(blk + 1) * BLK])
    w = _dot(wt_ref[A_KIDX_ROW + IDX_DIM:A_KC_ROW, :], ht_ref[...])
    wto_ref[...] = w[:IDX_HEADS] * (IDX_HEADS ** -0.5)
    t = _dot_nt(h_ref[...], wt_ref[A_KC_ROW:, :])
    ckv = _rms(t[:, :A_LATENT], lg_ref[...])
    ckv_ref[...] = ckv.astype(BF16)
    kidx_ref[...] = t[:, A_LATENT:A_LATENT + IDX_DIM].astype(BF16)
    for b2 in range(tm // SC):
        ckvt_ref[b2] = ckv[b2 * SC:(b2 + 1) * SC, :].T.astype(BF16)


def _a_proj(x2, g, w, lg, wgu2, wd2, tm=512):
    n, d = x2.shape
    steps = n // tm
    gu_rows, dn_rows = wgu2.shape[0] // steps, wd2.shape[0] // steps
    row = lambda i: (i, 0)
    blk3 = lambda i: (i, 0, 0)
    return pl.pallas_call(
        _a_proj_kernel,
        grid=(steps,),
        in_specs=[pl.BlockSpec((tm, d), row),
                  _resident((1, d)),
                  pl.BlockSpec((None,) + w.shape[1:], lambda i: (0, 0, 0),
                               pipeline_mode=pl.Buffered(1)),
                  _resident((1, A_LATENT)),
                  pl.BlockSpec((gu_rows, wgu2.shape[1]), row),
                  pl.BlockSpec((dn_rows, wd2.shape[1]), row)],
        out_specs=(pl.BlockSpec((tm // BLK, A_LATENT, A_QLAT), blk3),
                   pl.BlockSpec((tm // BLK, IDX_DIM, IDX_HEADS * BLK), blk3),
                   pl.BlockSpec((IDX_HEADS, tm), lambda i: (0, i)),
                   pl.BlockSpec((tm, IDX_DIM), row),
                   pl.BlockSpec((tm, A_LATENT), row),
                   pl.BlockSpec((tm // SC, A_LATENT, SC), blk3),
                   pl.BlockSpec((gu_rows, wgu2.shape[1]), row),
                   pl.BlockSpec((dn_rows, wd2.shape[1]), row)),
        out_shape=(jax.ShapeDtypeStruct((n // BLK, A_LATENT, A_QLAT), BF16),
                   jax.ShapeDtypeStruct((n // BLK, IDX_DIM, IDX_HEADS * BLK), BF16),
                   jax.ShapeDtypeStruct((IDX_HEADS, n), F32),
                   jax.ShapeDtypeStruct((n, IDX_DIM), BF16),
                   jax.ShapeDtypeStruct((n, A_LATENT), BF16),
                   jax.ShapeDtypeStruct((n // SC, A_LATENT, SC), BF16),
                   jax.ShapeDtypeStruct(wgu2.shape, BF16),
                   jax.ShapeDtypeStruct(wd2.shape, BF16)),
        scratch_shapes=[pltpu.VMEM((tm, d), BF16), pltpu.VMEM((d, tm), BF16),
                        pltpu.VMEM((A_WT_ROWS, d), BF16)],
        compiler_params=_params(1),
        name="a_proj",
    )(x2, g, w, lg, wgu2, wd2)


def _dsa_kernel(qr_ref, qir_ref, wt_ref, kidx_ref, ckv_ref, ckvt_ref, wuvt_ref,
                btd_ref, tri_ref, out_ref,
                score_ref, madd_ref, acc_ref, l_ref, *, k_sel, n_sc):
    i = pl.program_id(1)
    nsc = i // 2 + 1
    kf = float(k_sel)
    krow = lax.broadcasted_iota(I32, (SC, BLK), 0)
    qpos = i * BLK + lax.broadcasted_iota(I32, (SC, BLK), 1)

    def causal(sc):
        return sc * SC + krow <= qpos

    def key_rows(ref, sc, count=1):
        return ref[pl.ds(pl.multiple_of(sc * SC, SC), count * SC), :]

    def colsum(x):
        return jnp.sum(x.reshape(SC // 8, 8, BLK), axis=0)

    w2 = wt_ref[...] * (IDX_DIM ** -0.5)

    def index_group(sc0, count, last=False):
        d = _dot(key_rows(kidx_ref, sc0, count), qir_ref[0])
        for c in range(count):
            rows = slice(c * SC, (c + 1) * SC)
            s = w2[0:1, :] * jnp.maximum(d[rows, 0:BLK], 0.0)
            for j in range(1, IDX_HEADS):
                s = s + w2[j:j + 1, :] * jnp.maximum(d[rows, j * BLK:(j + 1) * BLK], 0.0)
            if last:
                s = jnp.where(causal(sc0 + c), s, NEG_INF)
            score_ref[sc0 + c] = s

    def index_body(t, carry):
        index_group(2 * t, 2)
        return carry
    lax.fori_loop(0, (nsc - 1) // 2, index_body, 0)

    @pl.when((nsc - 1) % 2 == 1)
    def _():
        index_group(nsc - 2, 1)
    index_group(nsc - 1, 1, last=True)

    @pl.when((i + 1) * BLK <= k_sel)
    def _():
        def body(sc, carry):
            madd_ref[sc] = jnp.where(causal(sc), 0.0, NEG_INF)
            return carry
        lax.fori_loop(0, nsc, body, 0)

    def key_to_float(key):
        return lax.bitcast_convert_type(key ^ ((key >> 31) & 0x7FFFFFFF), F32)

    def search(nblk):
        def count_ge(cand):
            cb = jnp.broadcast_to(cand, (8, BLK))
            parts = [jnp.zeros((8, BLK), F32)] * 8
            for t in range(nblk * BLK // 8):
                sc, r = divmod(t * 8, SC)
                ge = score_ref[sc, r:r + 8, :] >= cb
                parts[t % 8] = parts[t % 8] + jnp.where(ge, 1.0, 0.0)
            part = sum(parts[1:], parts[0])
            return jnp.sum(part, axis=0, keepdims=True)

        take = count_ge(jnp.zeros((1, BLK), F32)) >= kf
        ans = jnp.where(take, 0, INT_MIN)
        first = jnp.left_shift(1, 30)
        cand_f = jnp.where(take, key_to_float(first), key_to_float(INT_MIN | first))

        def bit_body(it, carry):
            ans, cand_f = carry
            cand = ans | jnp.left_shift(1, 30 - it)
            nxt = jnp.left_shift(1, jnp.maximum(29 - it, 0))
            f_take, f_keep = key_to_float(cand | nxt), key_to_float(ans | nxt)
            take = count_ge(cand_f) >= kf
            return jnp.where(take, cand, ans), jnp.where(take, f_take, f_keep)
        ans, _ = lax.fori_loop(0, 31, bit_body, (ans, cand_f))
        return key_to_float(ans)

    @pl.when((i + 1) * BLK > k_sel)
    def _():
        for nblk in range(k_sel // BLK + 1, 2 * n_sc + 1):
            @pl.when(i + 1 == nblk)
            def _(nblk=nblk):
                score_ref[n_sc] = jnp.broadcast_to(search(nblk), (SC, BLK))
        thr = score_ref[n_sc][0:1, :]

        def ge_body(sc, part):
            ge = score_ref[sc] >= thr
            madd_ref[sc] = jnp.where(ge, 0.0, NEG_INF)
            return part + colsum(jnp.where(ge, 1.0, 0.0))
        n_ge = jnp.sum(lax.fori_loop(0, nsc, ge_body, jnp.zeros((8, BLK), F32)),
                       axis=0, keepdims=True)

        @pl.when(jnp.max(n_ge) > kf)
        def _():
            def gt_body(sc, part):
                return part + colsum(jnp.where(score_ref[sc] > thr, 1.0, 0.0))
            n_gt = jnp.sum(lax.fori_loop(0, nsc, gt_body, jnp.zeros((8, BLK), F32)),
                           axis=0, keepdims=True)
            need = kf - n_gt

            def mask_body(sc, off):
                score = score_ref[sc]
                eq = score == thr
                eqf = jnp.where(eq, 1.0, 0.0)
                before = _dot(tri_ref[...], eqf.astype(BF16)) + off
                sel = (score > thr) | (eq & (before < need))
                madd_ref[sc] = jnp.where(sel, 0.0, NEG_INF)
                return off + jnp.sum(eqf, axis=0, keepdims=True)
            lax.fori_loop(0, nsc, mask_body, jnp.zeros((1, BLK), F32))

    c1 = (A_LATENT ** -0.5) * LOG2E
    hcols = N_HEADS * BLK

    def logits_group(sc0, m, count, near):
        lt = _dot(key_rows(ckv_ref, sc0, count), qr_ref[0])
        madds = [madd_ref[sc0 + c] for c in range(count)]
        new_m = []
        for h in range(N_HEADS):
            hs = slice(h * BLK, (h + 1) * BLK)
            mh = m[:, hs]
            for c in range(count):
                v = lt[c * SC:(c + 1) * SC, hs] * c1 + madds[c]
                if near:
                    halves = []
                    for half in range(2):
                        kind = jnp.clip(i - (2 * (sc0 + c) + half), 0, 2)
                        halves.append(v[half * BLK:(half + 1) * BLK] + btd_ref[kind, h])
                    v = jnp.concatenate(halves, axis=0)
                l_ref[sc0 + c, h] = v
                mh = jnp.maximum(mh, jnp.max(v, axis=0, keepdims=True))
            new_m.append(mh)
        return jnp.concatenate(new_m, axis=1)

    n_far = nsc - 2
    m = jnp.full((1, hcols), NEG_INF, F32)
    m = lax.fori_loop(0, n_far // 2, lambda t, m: logits_group(2 * t, m, 2, False), m)
    m = lax.cond((n_far > 0) & (n_far % 2 == 1),
                 lambda m: logits_group(n_far - 1, m, 1, False), lambda m: m, m)
    m = lax.cond(nsc >= 2, lambda m: logits_group(nsc - 2, m, 2, True),
                 lambda m: logits_group(0, m, 1, True), m)

    def pv_group(sc0, s, count):
        vt = jnp.concatenate([ckvt_ref[sc0 + c] for c in range(count)], axis=1)
        new_s = []
        for hp in range(N_HEADS // 2):
            cols = []
            for h in (2 * hp, 2 * hp + 1):
                hs = slice(h * BLK, (h + 1) * BLK)
                sh = s[:, hs]
                ps = []
                for c in range(count):
                    p = jnp.exp2(l_ref[sc0 + c, h] - m[:, hs])
                    sh = sh + colsum(p)
                    ps.append(p.astype(BF16))
                new_s.append(sh)
                cols.append(jnp.concatenate(ps, axis=0))
            pair = jnp.concatenate(cols, axis=1)
            acc_ref[hp] = acc_ref[hp] + _dot(vt, pair)
        return jnp.concatenate(new_s, axis=1)

    acc_ref[...] = jnp.zeros(acc_ref.shape, F32)
    s = lax.fori_loop(0, nsc // 2, lambda t, s: pv_group(2 * t, s, 2),
                      jnp.zeros((8, hcols), F32))
    s = lax.cond(nsc % 2 == 1, lambda s: pv_group(nsc - 1, s, 1), lambda s: s, s)
    inv = 1.0 / jnp.sum(s, axis=0, keepdims=True)

    outs = []
    for h in range(N_HEADS):
        hs = slice(h * BLK, (h + 1) * BLK)
        a = acc_ref[h // 2, :, (h % 2) * BLK:(h % 2 + 1) * BLK]
        o = (a * inv[:, hs]).astype(BF16)
        outs.append(_dot(wuvt_ref[h], o))
    out_ref[...] = jnp.concatenate(outs, axis=0).T.astype(BF16)


def _dsa_attn(qr, qir, wt, kidx, ckv, ckvt, wuvt, btd, batch, seq):
    nblk = seq // BLK
    n_sc = seq // SC
    k_sel = min(TOPK_MAX, seq // 4)
    hcols = N_HEADS * BLK
    tri = jnp.asarray(np.tril(np.ones((SC, SC), np.float32), -1), BF16)
    return pl.pallas_call(
        functools.partial(_dsa_kernel, k_sel=k_sel, n_sc=n_sc),
        grid=(batch, nblk),
        in_specs=[pl.BlockSpec((1, A_LATENT, hcols), lambda b, i: (b * nblk + i, 0, 0)),
                  pl.BlockSpec((1, IDX_DIM, IDX_HEADS * BLK), lambda b, i: (b * nblk + i, 0, 0)),
                  pl.BlockSpec((IDX_HEADS, BLK), lambda b, i: (0, b * nblk + i)),
                  pl.BlockSpec((seq, IDX_DIM), lambda b, i: (b, 0)),
                  pl.BlockSpec((seq, A_LATENT), lambda b, i: (b, 0)),
                  pl.BlockSpec((n_sc, A_LATENT, SC), lambda b, i: (b, 0, 0)),
                  _resident((N_HEADS, A_VHEAD, A_LATENT)),
                  _resident((3, N_HEADS, BLK, BLK)),
                  _resident((SC, SC))],
        out_specs=pl.BlockSpec((BLK, N_HEADS * A_VHEAD), lambda b, i: (b * nblk + i, 0)),
        out_shape=jax.ShapeDtypeStruct((batch * seq, N_HEADS * A_VHEAD), BF16),
        scratch_shapes=[
            pltpu.VMEM((n_sc + 1, SC, BLK), F32),
            pltpu.VMEM((n_sc, SC, BLK), F32),
            pltpu.VMEM((N_HEADS // 2, A_LATENT, 2 * BLK), F32),
            pltpu.VMEM((n_sc, N_HEADS, SC, BLK), F32),
        ],
        compiler_params=_params(2),
        name="dsa_attn",
    )(qr, qir, wt, kidx, ckv, ckvt, wuvt, btd, tri)


B_Q = N_HEADS * B_HEAD_DIM
B_KV = B_KV_HEADS * B_HEAD_DIM


def _b_proj_kernel(x_ref, g_ref, wt_ref, bcol_ref, wk_ref, bk_ref,
                   qr_ref, kk_ref, vt_ref, h_ref, ht_ref):
    tm = x_ref.shape[0]
    hf = _rms(x_ref[...], g_ref[...])
    h_ref[...] = hf.astype(BF16)
    ht_ref[...] = hf.T.astype(BF16)
    heads_per_dot = 512 // B_HEAD_DIM
    for c in range(B_Q // 512):
        rows = slice(c * 512, (c + 1) * 512)
        r = (_dot(wt_ref[rows, :], ht_ref[...]) + bcol_ref[rows, :]).astype(BF16)
        for hh in range(heads_per_dot):
            h = c * heads_per_dot + hh
            for blk in range(tm // BLK):
                qr_ref[blk, :, h * BLK:(h + 1) * BLK] = (
                    r[hh * B_HEAD_DIM:(hh + 1) * B_HEAD_DIM, blk * BLK:(blk + 1) * BLK])
    vt_ref[...] = (_dot(wt_ref[B_Q:, :], ht_ref[...]) + bcol_ref[B_Q:, :]).astype(BF16)
    t = _dot(h_ref[...], wk_ref[...]) + bk_ref[...]
    for kv in range(B_KV_HEADS):
        kk_ref[kv] = t[:, kv * B_HEAD_DIM:(kv + 1) * B_HEAD_DIM].astype(BF16)


def _b_proj(x2, g, wt, bcol, wk, bk, tm=512):
    n, d = x2.shape
    return pl.pallas_call(
        _b_proj_kernel,
        grid=(n // tm,),
        in_specs=[pl.BlockSpec((tm, d), lambda i: (i, 0)),
                  _resident((1, d)),
                  _resident((B_Q + B_KV, d)),
                  _resident((B_Q + B_KV, 1)),
                  _resident((d, B_KV)),
                  _resident((1, B_KV))],
        out_specs=(pl.BlockSpec((tm // BLK, B_HEAD_DIM, N_HEADS * BLK), lambda i: (i, 0, 0)),
                   pl.BlockSpec((B_KV_HEADS, tm, B_HEAD_DIM), lambda i: (0, i, 0)),
                   pl.BlockSpec((B_KV, tm), lambda i: (0, i))),
        out_shape=(jax.ShapeDtypeStruct((n // BLK, B_HEAD_DIM, N_HEADS * BLK), BF16),
                   jax.ShapeDtypeStruct((B_KV_HEADS, n, B_HEAD_DIM), BF16),
                   jax.ShapeDtypeStruct((B_KV, n), BF16)),
        scratch_shapes=[pltpu.VMEM((tm, d), BF16), pltpu.VMEM((d, tm), BF16)],
        compiler_params=_params(1),
        name="b_proj",
    )(x2, g, wt, bcol, wk, bk)


def _swa_kernel(sink_ref, qr_ref, kp_ref, kc_ref, vp_ref, vc_ref, bts_ref, out_ref):
    n = pl.program_id(1)
    c1 = (B_HEAD_DIM ** -0.5) * LOG2E
    group = N_HEADS // B_KV_HEADS
    blocks = (
        (qr_ref.at[0], kp_ref, kc_ref.at[:, 0:BLK, :], vp_ref, vc_ref.at[:, 0:BLK],
         jnp.where(n >= 1, 1, 2)),
        (qr_ref.at[1], kc_ref.at[:, 0:BLK, :], kc_ref.at[:, BLK:2 * BLK, :],
         vc_ref.at[:, 0:BLK], vc_ref.at[:, BLK:2 * BLK], 1))
    tasks = [(qb, kv, g2) for qb in range(2) for kv in range(B_KV_HEADS)
             for g2 in range(group // 2)]

    def scores(task):
        qb, kv, g2 = task
        q_ref, kp, kc = blocks[qb][0:3]
        h0 = kv * group + 2 * g2
        q = q_ref[:, h0 * BLK:(h0 + 2) * BLK]
        return _dot(kp[kv], q), _dot(kc[kv], q)

    def attend(task, lp, lc):
        qb, kv, g2 = task
        vp, vc, prev_kind = blocks[qb][3:6]
        vrows = slice(kv * B_HEAD_DIM, (kv + 1) * B_HEAD_DIM)
        pps, pcs, invs = [], [], []
        for j in range(2):
            h = kv * group + 2 * g2 + j
            lanes = slice(j * BLK, (j + 1) * BLK)
            sink = sink_ref[:, h * BLK:(h + 1) * BLK] * LOG2E
            a = lp[:, lanes] * c1 + bts_ref[prev_kind, h]
            b = lc[:, lanes] * c1 + bts_ref[0, h]
            m = jnp.maximum(jnp.max(jnp.maximum(a, b), axis=0, keepdims=True), sink)
            pp = jnp.exp2(a - m)
            pc = jnp.exp2(b - m)
            den = jnp.sum(pp + pc, axis=0, keepdims=True) + jnp.exp2(sink - m)
            invs.append(1.0 / den)
            pps.append(pp.astype(BF16))
            pcs.append(pc.astype(BF16))
        ot = (_dot(vp[vrows, :], jnp.concatenate(pps, axis=1))
              + _dot(vc[vrows, :], jnp.concatenate(pcs, axis=1)))
        return [ot[:, j * BLK:(j + 1) * BLK] * invs[j] for j in range(2)]

    outs = [[], []]
    ahead = 3
    queue = [scores(task) for task in tasks[:ahead]]
    for t, task in enumerate(tasks):
        if t + ahead < len(tasks):
            queue.append(scores(tasks[t + ahead]))
        outs[task[0]] += attend(task, *queue.pop(0))
    for qb in range(2):
        out_ref[qb * BLK:(qb + 1) * BLK, :] = (
            jnp.concatenate(outs[qb], axis=0).T.astype(BF16))


def _swa_attn(qr, kk, vt, sinks_row, bts, batch, seq):
    nb2 = seq // (2 * BLK)
    cur = lambda b, n: b * nb2 + n
    prev = lambda b, n: 2 * (b * nb2 + n) - jnp.minimum(n, 1)
    return pl.pallas_call(
        _swa_kernel,
        grid=(batch, nb2),
        in_specs=[_resident((1, N_HEADS * BLK)),
                  pl.BlockSpec((2, B_HEAD_DIM, N_HEADS * BLK), lambda b, n: (cur(b, n), 0, 0)),
                  pl.BlockSpec((B_KV_HEADS, BLK, B_HEAD_DIM), lambda b, n: (0, prev(b, n), 0)),
                  pl.BlockSpec((B_KV_HEADS, 2 * BLK, B_HEAD_DIM), lambda b, n: (0, cur(b, n), 0)),
                  pl.BlockSpec((B_KV, BLK), lambda b, n: (0, prev(b, n))),
                  pl.BlockSpec((B_KV, 2 * BLK), lambda b, n: (0, cur(b, n))),
                  _resident((3, N_HEADS, BLK, BLK))],
        out_specs=pl.BlockSpec((2 * BLK, B_Q), lambda b, n: (cur(b, n), 0)),
        out_shape=jax.ShapeDtypeStruct((batch * seq, B_Q), BF16),
        compiler_params=_params(2),
        name="swa_attn",
    )(sinks_row, qr, kk, kk, vt, vt, bts)


FF_CHUNK = 256


def _ffn_kernel(x_ref, o_ref, wo_ref, bo_ref, g_ref, wgu_ref, wd_ref, fg_ref,
                out_ref, h_ref, act_ref, *, d_ff, final):
    x1 = x_ref[...] + _dot(o_ref[...], wo_ref[...]) + bo_ref[...]
    out_ref[...] = x1
    h_ref[...] = _rms(x1, g_ref[...]).astype(BF16)
    for c in range(d_ff // FF_CHUNK):
        gate = _dot(h_ref[...], wgu_ref[:, c * FF_CHUNK:(c + 1) * FF_CHUNK])
        up = _dot(h_ref[...], wgu_ref[:, d_ff + c * FF_CHUNK:d_ff + (c + 1) * FF_CHUNK])
        act_ref[:, c * FF_CHUNK:(c + 1) * FF_CHUNK] = (
            gate * jax.nn.sigmoid(gate) * up).astype(BF16)
    y = out_ref[...] + _dot(act_ref[...], wd_ref[...])
    if final:
        y = _rms(y, fg_ref[...])
    out_ref[...] = y


def _ffn(x2, o, wo, bo, g, wgu_all, wd_all, layer, fg, final, tm=1024):
    n, d = x2.shape
    d_ff = wd_all.shape[1]
    row = lambda i: (i, 0)
    pick = lambda i: (layer, 0, 0)
    return pl.pallas_call(
        functools.partial(_ffn_kernel, d_ff=d_ff, final=final),
        grid=(n // tm,),
        in_specs=[pl.BlockSpec((tm, d), row),
                  pl.BlockSpec((tm, o.shape[1]), row),
                  _resident(wo.shape),
                  _resident((1, d)),
                  _resident((1, d)),
                  pl.BlockSpec((None,) + wgu_all.shape[1:], pick, pipeline_mode=pl.Buffered(1)),
                  pl.BlockSpec((None,) + wd_all.shape[1:], pick, pipeline_mode=pl.Buffered(1)),
                  _resident((1, d))],
        out_specs=pl.BlockSpec((tm, d), row),
        out_shape=jax.ShapeDtypeStruct((n, d), F32),
        scratch_shapes=[pltpu.VMEM((tm, d), BF16),
                        pltpu.VMEM((tm, d_ff), BF16)],
        compiler_params=_params(1),
        name="ffn",
    )(x2, o, wo, bo, g, wgu_all, wd_all, fg)


def kernel(x, rel_bias, attn_norm, ffn_norm, final_norm, a_w_in, a_latent_norm, a_w_uv, a_w_out, b_w_in, b_b_in, b_sinks, b_w_out, b_b_out, ffn_w_gate_up, ffn_w_down):
    batch, seq, d = x.shape
    x2 = x.reshape(batch * seq, d)
    bts, btd = _bias_tiles(rel_bias)
    fg = final_norm.reshape(1, d)

    assert a_w_in.shape[2] == A_IN
    gu_shape, dn_shape = ffn_w_gate_up.shape, ffn_w_down.shape
    qr, qir, wto, kidx, ckv, ckvt, wgu_bf, wd_bf = _a_proj(
        x2, attn_norm[0].reshape(1, d), jnp.swapaxes(a_w_in, 1, 2),
        a_latent_norm[0].reshape(1, A_LATENT),
        ffn_w_gate_up.reshape(-1, gu_shape[-1]), ffn_w_down.reshape(-1, dn_shape[-1]))
    wgu_all, wd_all = wgu_bf.reshape(gu_shape), wd_bf.reshape(dn_shape)
    wuvt = jnp.swapaxes(a_w_uv[0], 1, 2).astype(BF16)
    o = _dsa_attn(qr, qir, wto, kidx, ckv, ckvt, wuvt, btd, batch, seq)
    x2 = _ffn(x2, o, a_w_out[0].astype(BF16), jnp.zeros((1, d), F32),
              ffn_norm[0].reshape(1, d), wgu_all, wd_all, 0, fg, final=False)

    wb, bb = b_w_in[0].astype(BF16), b_b_in[0]
    wqv = jnp.concatenate([wb[:, :B_Q], wb[:, B_Q + B_KV:]], axis=1).T
    bqv = jnp.concatenate([bb[:B_Q], bb[B_Q + B_KV:]]).reshape(-1, 1)
    qr, kk, vt = _b_proj(
        x2, attn_norm[1].reshape(1, d), wqv, bqv, wb[:, B_Q:B_Q + B_KV],
        bb[B_Q:B_Q + B_KV].reshape(1, B_KV))
    sinks_row = jnp.repeat(b_sinks[0], BLK).reshape(1, N_HEADS * BLK)
    o = _swa_attn(qr, kk, vt, sinks_row, bts, batch, seq)
    x2 = _ffn(x2, o, b_w_out[0].astype(BF16), b_b_out[0].reshape(1, d),
              ffn_norm[1].reshape(1, d), wgu_all, wd_all, 1, fg, final=True)
    return x2.reshape(batch, seq, d)
```

```python
import functools
import math

import numpy as np
import jax
import jax.numpy as jnp
from jax import lax
from jax.experimental import pallas as pl
from jax.experimental.pallas import tpu as pltpu

F32 = jnp.float32
BF16 = jnp.bfloat16
I32 = jnp.int32

EPS = 1e-6
N_HEADS = 16
A_LATENT = 128
A_VHEAD = 64
IDX_HEADS = 8
IDX_DIM = 64
TOPK_MAX = 256
B_KV_HEADS = 2
B_HEAD_DIM = 64
REL_BUCKETS = 32
REL_MAX_DIST = 128

BLK = 128
SC = 256
INT_MIN = -(2 ** 31)
NEG_INF = float("-inf")
LOG2E = math.log2(math.e)
VMEM_LIMIT = 56 * 1024 * 1024


def _rms(xf, g):
    ms = jnp.mean(xf * xf, axis=-1, keepdims=True)
    return xf * lax.rsqrt(ms + EPS) * g


def _dot(a, b):
    return jnp.dot(a, b, preferred_element_type=F32)


def _dot_nt(a, b):
    return lax.dot_general(a, b, (((1,), (1,)), ((), ())), preferred_element_type=F32)


def _params(n_axes):
    return pltpu.CompilerParams(
        dimension_semantics=("arbitrary",) * n_axes, vmem_limit_bytes=VMEM_LIMIT)


def _resident(shape):
    zeros = (0,) * len(shape)
    return pl.BlockSpec(shape, lambda *_: zeros, pipeline_mode=pl.Buffered(1))


def _bucket(rel):
    n = np.maximum(rel, 0)
    max_exact = REL_BUCKETS // 2
    nf = np.maximum(n, max_exact).astype(np.float64)
    large = max_exact + (np.log(nf / max_exact) / math.log(REL_MAX_DIST / max_exact)
                         * (REL_BUCKETS - max_exact)).astype(np.int32)
    large = np.minimum(large, REL_BUCKETS - 1)
    return np.where(n < max_exact, n, large).astype(np.int32)


def _bucket_tables():
    k = np.arange(BLK)[:, None]
    q = np.arange(BLK)[None, :]
    return _bucket(np.stack([q - k, BLK + q - k]))


def _bias_kernel(rb_ref, bidx_ref, bts_ref, btd_ref):
    k = lax.broadcasted_iota(I32, (BLK, BLK), 0)
    q = lax.broadcasted_iota(I32, (BLK, BLK), 1)
    window = (k <= q, k > q)
    for h in range(N_HEADS):
        for kind in range(2):
            bidx = bidx_ref[kind]

            t = jnp.zeros((BLK, BLK), F32)
            for b in range(REL_BUCKETS):
                t = jnp.where(bidx == b, rb_ref[b, h], t)
            bts_ref[kind, h] = jnp.where(window[kind], t * LOG2E, NEG_INF)
            btd_ref[kind, h] = (t - rb_ref[REL_BUCKETS - 1, h]) * LOG2E
        bts_ref[2, h] = jnp.full((BLK, BLK), NEG_INF, F32)
        btd_ref[2, h] = jnp.zeros((BLK, BLK), F32)


def _bias_tiles(rel_bias):
    return pl.pallas_call(
        _bias_kernel,
        out_shape=(jax.ShapeDtypeStruct((3, N_HEADS, BLK, BLK), F32),
                   jax.ShapeDtypeStruct((3, N_HEADS, BLK, BLK), F32)),
        in_specs=[pl.BlockSpec(memory_space=pltpu.SMEM),
                  pl.BlockSpec(memory_space=pltpu.VMEM)],
        out_specs=(pl.BlockSpec(memory_space=pltpu.VMEM),
                   pl.BlockSpec(memory_space=pltpu.VMEM)),
        name="bias_tiles",
    )(rel_bias, jnp.asarray(_bucket_tables()))


A_QLAT = N_HEADS * A_LATENT
A_QIDX = IDX_HEADS * IDX_DIM
A_QIDX_ROW = A_QLAT + A_LATENT
A_KIDX_ROW = A_QIDX_ROW + A_QIDX
A_IN = A_KIDX_ROW + IDX_DIM + IDX_HEADS
A_KC_ROW = A_KIDX_ROW + IDX_DIM + 16
A_WT_ROWS = A_KC_ROW + 2 * A_LATENT


def _a_proj_kernel(x_ref, g_ref, w_ref, lg_ref, wgu_ref, wd_ref,
                   qr_ref, qir_ref, wto_ref, kidx_ref, ckv_ref, ckvt_ref,
                   wgu_bf_ref, wd_bf_ref, h_ref, ht_ref, wt_ref):
    tm = x_ref.shape[0]

    @pl.when(pl.program_id(0) == 0)
    def _():
        wt_ref[0:A_IN, :] = w_ref[...].astype(BF16)
        wt_ref[A_IN:A_KC_ROW, :] = jnp.zeros((A_KC_ROW - A_IN, wt_ref.shape[1]), BF16)
        wt_ref[A_KC_ROW:A_KC_ROW + A_LATENT, :] = w_ref[A_QLAT:A_QIDX_ROW, :].astype(BF16)
        wt_ref[A_KC_ROW + A_LATENT:A_KC_ROW + A_LATENT + IDX_DIM, :] = (
            w_ref[A_KIDX_ROW:A_KIDX_ROW + IDX_DIM, :].astype(BF16))
        wt_ref[A_KC_ROW + A_LATENT + IDX_DIM:, :] = jnp.zeros(
            (2 * A_LATENT - A_LATENT - IDX_DIM, wt_ref.shape[1]), BF16)

    wgu_bf_ref[...] = wgu_ref[...].astype(BF16)
    wd_bf_ref[...] = wd_ref[...].astype(BF16)
    hf = _rms(x_ref[...], g_ref[...])
    h_ref[...] = hf.astype(BF16)
    ht_ref[...] = hf.T.astype(BF16)
    for c in range(A_QLAT // 512):
        r = _dot(wt_ref[c * 512:(c + 1) * 512, :], ht_ref[...]).astype(BF16)
        for hh in range(4):
            h = 4 * c + hh
            for blk in range(tm // BLK):
                qr_ref[blk, :, h * BLK:(h + 1) * BLK] = (
                    r[hh * A_LATENT:(hh + 1) * A_LATENT, blk * BLK:(blk + 1) * BLK])
    r = _dot(wt_ref[A_QIDX_ROW:A_KIDX_ROW, :], ht_ref[...]).astype(BF16)
    for j in range(IDX_HEADS):
        for blk in range(tm // BLK):
            qir_ref[blk, :, j * BLK:(j + 1) * BLK] = (
                r[j * IDX_DIM:(j + 1) * IDX_DIM, blk * BLK:(blk + 1) * BLK])
    w = _dot(wt_ref[A_KIDX_ROW + IDX_DIM:A_KC_ROW, :], ht_ref[...])
    wto_ref[...] = w[:IDX_HEADS] * (IDX_HEADS ** -0.5)
    t = _dot_nt(h_ref[...], wt_ref[A_KC_ROW:, :])
    ckv = _rms(t[:, :A_LATENT], lg_ref[...])
    ckv_ref[...] = ckv.astype(BF16)
    kidx_ref[...] = t[:, A_LATENT:A_LATENT + IDX_DIM].astype(BF16)
    for b2 in range(tm // SC):
        ckvt_ref[b2] = ckv[b2 * SC:(b2 + 1) * SC, :].T.astype(BF16)


def _a_proj(x2, g, w, lg, wgu2, wd2, tm=512):
    n, d = x2.shape
    steps = n // tm
    gu_rows, dn_rows = wgu2.shape[0] // steps, wd2.shape[0] // steps
    row = lambda i: (i, 0)
    blk3 = lambda i: (i, 0, 0)
    return pl.pallas_call(
        _a_proj_kernel,
        grid=(steps,),
        in_specs=[pl.BlockSpec((tm, d), row),
                  _resident((1, d)),
                  pl.BlockSpec((None,) + w.shape[1:], lambda i: (0, 0, 0),
                               pipeline_mode=pl.Buffered(1)),
                  _resident((1, A_LATENT)),
                  pl.BlockSpec((gu_rows, wgu2.shape[1]), row),
                  pl.BlockSpec((dn_rows, wd2.shape[1]), row)],
        out_specs=(pl.BlockSpec((tm // BLK, A_LATENT, A_QLAT), blk3),
                   pl.BlockSpec((tm // BLK, IDX_DIM, IDX_HEADS * BLK), blk3),
                   pl.BlockSpec((IDX_HEADS, tm), lambda i: (0, i)),
                   pl.BlockSpec((tm, IDX_DIM), row),
                   pl.BlockSpec((tm, A_LATENT), row),
                   pl.BlockSpec((tm // SC, A_LATENT, SC), blk3),
                   pl.BlockSpec((gu_rows, wgu2.shape[1]), row),
                   pl.BlockSpec((dn_rows, wd2.shape[1]), row)),
        out_shape=(jax.ShapeDtypeStruct((n // BLK, A_LATENT, A_QLAT), BF16),
                   jax.ShapeDtypeStruct((n // BLK, IDX_DIM, IDX_HEADS * BLK), BF16),
                   jax.ShapeDtypeStruct((IDX_HEADS, n), F32),
                   jax.ShapeDtypeStruct((n, IDX_DIM), BF16),
                   jax.ShapeDtypeStruct((n, A_LATENT), BF16),
                   jax.ShapeDtypeStruct((n // SC, A_LATENT, SC), BF16),
                   jax.ShapeDtypeStruct(wgu2.shape, BF16),
                   jax.ShapeDtypeStruct(wd2.shape, BF16)),
        scratch_shapes=[pltpu.VMEM((tm, d), BF16), pltpu.VMEM((d, tm), BF16),
                        pltpu.VMEM((A_WT_ROWS, d), BF16)],
        compiler_params=_params(1),
        name="a_proj",
    )(x2, g, w, lg, wgu2, wd2)


def _dsa_kernel(qr_ref, qirn_ref, wtn_ref, kidx_ref, ckv_ref, ckvt_ref, wuvt_ref,
                btd_ref, tri_ref, out_ref,
                score_ref, madd_ref, acc_ref, l_ref, *, k_sel, n_sc):
    i = pl.program_id(1)
    nsc = i // 2 + 1
    kf = float(k_sel)
    krow = lax.broadcasted_iota(I32, (SC, BLK), 0)
    qpos = i * BLK + lax.broadcasted_iota(I32, (SC, BLK), 1)

    def causal(sc):
        return sc * SC + krow <= qpos

    def key_rows(ref, sc, count=1):
        return ref[pl.ds(pl.multiple_of(sc * SC, SC), count * SC), :]

    def colsum(x):
        return jnp.sum(x.reshape(SC // 8, 8, BLK), axis=0)

    w2 = wtn_ref[...] * (IDX_DIM ** -0.5)
    nsc_next = (i + 1) // 2 + 1
    qpos_next = qpos + BLK

    def index_next(sc0, count):
        d = _dot(key_rows(kidx_ref, sc0, count), qirn_ref[0])
        for c in range(count):
            rows = slice(c * SC, (c + 1) * SC)
            s = w2[0:1, :] * jnp.maximum(d[rows, 0:BLK], 0.0)
            for j in range(1, IDX_HEADS):
                s = s + w2[j:j + 1, :] * jnp.maximum(d[rows, j * BLK:(j + 1) * BLK], 0.0)
            score_ref[sc0 + c] = jnp.where((sc0 + c) * SC + krow <= qpos_next, s, NEG_INF)

    @pl.when((i + 1) * BLK <= k_sel)
    def _():
        def body(sc, carry):
            madd_ref[sc] = jnp.where(causal(sc), 0.0, NEG_INF)
            return carry
        lax.fori_loop(0, nsc, body, 0)

    def key_to_float(key):
        return lax.bitcast_convert_type(key ^ ((key >> 31) & 0x7FFFFFFF), F32)

    def search(nblk):
        def count_ge(cand):
            cb = jnp.broadcast_to(cand, (8, BLK))
            parts = [jnp.zeros((8, BLK), F32)] * 8
            for t in range(nblk * BLK // 8):
                sc, r = divmod(t * 8, SC)
                ge = score_ref[sc, r:r + 8, :] >= cb
                parts[t % 8] = parts[t % 8] + jnp.where(ge, 1.0, 0.0)
            part = sum(parts[1:], parts[0])
            return jnp.sum(part, axis=0, keepdims=True)

        take = count_ge(jnp.zeros((1, BLK), F32)) >= kf
        ans = jnp.where(take, 0, INT_MIN)
        first = jnp.left_shift(1, 30)
        cand_f = jnp.where(take, key_to_float(first), key_to_float(INT_MIN | first))

        def bit_body(it, carry):
            ans, cand_f = carry
            cand = ans | jnp.left_shift(1, 30 - it)
            nxt = jnp.left_shift(1, jnp.maximum(29 - it, 0))
            f_take, f_keep = key_to_float(cand | nxt), key_to_float(ans | nxt)
            take = count_ge(cand_f) >= kf
            return jnp.where(take, cand, ans), jnp.where(take, f_take, f_keep)
        ans, _ = lax.fori_loop(0, 31, bit_body, (ans, cand_f))
        return key_to_float(ans)

    @pl.when((i + 1) * BLK > k_sel)
    def _():
        for nblk in range(k_sel // BLK + 1, 2 * n_sc + 1):
            @pl.when(i + 1 == nblk)
            def _(nblk=nblk):
                score_ref[n_sc] = jnp.broadcast_to(search(nblk), (SC, BLK))
        thr = score_ref[n_sc][0:1, :]

        def ge_body(sc, part):
            ge = score_ref[sc] >= thr
            madd_ref[sc] = jnp.where(ge, 0.0, NEG_INF)
            return part + colsum(jnp.where(ge, 1.0, 0.0))
        n_ge = jnp.sum(lax.fori_loop(0, nsc, ge_body, jnp.zeros((8, BLK), F32)),
                       axis=0, keepdims=True)

        @pl.when(jnp.max(n_ge) > kf)
        def _():
            def gt_body(sc, part):
                return part + colsum(jnp.where(score_ref[sc] > thr, 1.0, 0.0))
            n_gt = jnp.sum(lax.fori_loop(0, nsc, gt_body, jnp.zeros((8, BLK), F32)),
                           axis=0, keepdims=True)
            need = kf - n_gt

            def mask_body(sc, off):
                score = score_ref[sc]
                eq = score == thr
                eqf = jnp.where(eq, 1.0, 0.0)
                before = _dot(tri_ref[...], eqf.astype(BF16)) + off
                sel = (score > thr) | (eq & (before < need))
                madd_ref[sc] = jnp.where(sel, 0.0, NEG_INF)
                return off + jnp.sum(eqf, axis=0, keepdims=True)
            lax.fori_loop(0, nsc, mask_body, jnp.zeros((1, BLK), F32))

    c1 = (A_LATENT ** -0.5) * LOG2E
    hcols = N_HEADS * BLK

    def logits_group(sc0, m, count, near):
        lt = _dot(key_rows(ckv_ref, sc0, count), qr_ref[0])
        madds = [madd_ref[sc0 + c] for c in range(count)]
        new_m = []
        for h in range(N_HEADS):
            hs = slice(h * BLK, (h + 1) * BLK)
            mh = m[:, hs]
            for c in range(count):
                v = lt[c * SC:(c + 1) * SC, hs] * c1 + madds[c]
                if near:
                    halves = []
                    for half in range(2):
                        kind = jnp.clip(i - (2 * (sc0 + c) + half), 0, 2)
                        halves.append(v[half * BLK:(half + 1) * BLK] + btd_ref[kind, h])
                    v = jnp.concatenate(halves, axis=0)
                l_ref[sc0 + c, h] = v
                mh = jnp.maximum(mh, jnp.max(v, axis=0, keepdims=True))
            new_m.append(mh)
        return jnp.concatenate(new_m, axis=1)

    n_far = nsc - 2
    m = jnp.full((1, hcols), NEG_INF, F32)
    m = lax.fori_loop(0, n_far // 2, lambda t, m: logits_group(2 * t, m, 2, False), m)
    m = lax.cond((n_far > 0) & (n_far % 2 == 1),
                 lambda m: logits_group(n_far - 1, m, 1, False), lambda m: m, m)
    m = lax.cond(nsc >= 2, lambda m: logits_group(nsc - 2, m, 2, True),
                 lambda m: logits_group(0, m, 1, True), m)

    def pv_group(sc0, s, count):
        index_next(sc0, count)
        vt = jnp.concatenate([ckvt_ref[sc0 + c] for c in range(count)], axis=1)
        new_s = []
        for hp in range(N_HEADS // 2):
            cols = []
            for h in (2 * hp, 2 * hp + 1):
                hs = slice(h * BLK, (h + 1) * BLK)
                sh = s[:, hs]
                ps = []
                for c in range(count):
                    p = jnp.exp2(l_ref[sc0 + c, h] - m[:, hs])
                    sh = sh + colsum(p)
                    ps.append(p.astype(BF16))
                new_s.append(sh)
                cols.append(jnp.concatenate(ps, axis=0))
            pair = jnp.concatenate(cols, axis=1)
            acc_ref[hp] = acc_ref[hp] + _dot(vt, pair)
        return jnp.concatenate(new_s, axis=1)

    acc_ref[...] = jnp.zeros(acc_ref.shape, F32)
    s = lax.fori_loop(0, nsc // 2, lambda t, s: pv_group(2 * t, s, 2),
                      jnp.zeros((8, hcols), F32))
    s = lax.cond(nsc % 2 == 1, lambda s: pv_group(nsc - 1, s, 1), lambda s: s, s)
    inv = 1.0 / jnp.sum(s, axis=0, keepdims=True)

    @pl.when((nsc_next > nsc) & (nsc_next <= n_sc))
    def _():
        index_next(nsc, 1)

    outs = []
    for h in range(N_HEADS):
        hs = slice(h * BLK, (h + 1) * BLK)
        a = acc_ref[h // 2, :, (h % 2) * BLK:(h % 2 + 1) * BLK]
        o = (a * inv[:, hs]).astype(BF16)
        outs.append(_dot(wuvt_ref[h], o))
    out_ref[...] = jnp.concatenate(outs, axis=0).T.astype(BF16)


def _dsa_attn(qr, qir, wt, kidx, ckv, ckvt, wuvt, btd, batch, seq):
    nblk = seq // BLK
    n_sc = seq // SC
    k_sel = min(TOPK_MAX, seq // 4)
    hcols = N_HEADS * BLK
    tri = jnp.asarray(np.tril(np.ones((SC, SC), np.float32), -1), BF16)
    return pl.pallas_call(
        functools.partial(_dsa_kernel, k_sel=k_sel, n_sc=n_sc),
        grid=(batch, nblk),
        in_specs=[pl.BlockSpec((1, A_LATENT, hcols), lambda b, i: (b * nblk + i, 0, 0)),
                  pl.BlockSpec((1, IDX_DIM, IDX_HEADS * BLK),
                               lambda b, i: (b * nblk + jnp.minimum(i + 1, nblk - 1), 0, 0)),
                  pl.BlockSpec((IDX_HEADS, BLK),
                               lambda b, i: (0, b * nblk + jnp.minimum(i + 1, nblk - 1))),
                  pl.BlockSpec((seq, IDX_DIM), lambda b, i: (b, 0)),
                  pl.BlockSpec((seq, A_LATENT), lambda b, i: (b, 0)),
                  pl.BlockSpec((n_sc, A_LATENT, SC), lambda b, i: (b, 0, 0)),
                  _resident((N_HEADS, A_VHEAD, A_LATENT)),
                  _resident((3, N_HEADS, BLK, BLK)),
                  _resident((SC, SC))],
        out_specs=pl.BlockSpec((BLK, N_HEADS * A_VHEAD), lambda b, i: (b * nblk + i, 0)),
        out_shape=jax.ShapeDtypeStruct((batch * seq, N_HEADS * A_VHEAD), BF16),
        scratch_shapes=[
            pltpu.VMEM((n_sc + 1, SC, BLK), F32),
            pltpu.VMEM((n_sc, SC, BLK), F32),
            pltpu.VMEM((N_HEADS // 2, A_LATENT, 2 * BLK), F32),
            pltpu.VMEM((n_sc, N_HEADS, SC, BLK), F32),
        ],
        compiler_params=_params(2),
        name="dsa_attn",
    )(qr, qir, wt, kidx, ckv, ckvt, wuvt, btd, tri)


B_Q = N_HEADS * B_HEAD_DIM
B_KV = B_KV_HEADS * B_HEAD_DIM


def _b_proj_kernel(x_ref, g_ref, wt_ref, bcol_ref, wk_ref, bk_ref,
                   qr_ref, kk_ref, vt_ref, h_ref, ht_ref):
    tm = x_ref.shape[0]
    hf = _rms(x_ref[...], g_ref[...])
    h_ref[...] = hf.astype(BF16)
    ht_ref[...] = hf.T.astype(BF16)
    heads_per_dot = 512 // B_HEAD_DIM
    for c in range(B_Q // 512):
        rows = slice(c * 512, (c + 1) * 512)
        r = (_dot(wt_ref[rows, :], ht_ref[...]) + bcol_ref[rows, :]).astype(BF16)
        for hh in range(heads_per_dot):
            h = c * heads_per_dot + hh
            for blk in range(tm // BLK):
                qr_ref[blk, :, h * BLK:(h + 1) * BLK] = (
                    r[hh * B_HEAD_DIM:(hh + 1) * B_HEAD_DIM, blk * BLK:(blk + 1) * BLK])
    vt_ref[...] = (_dot(wt_ref[B_Q:, :], ht_ref[...]) + bcol_ref[B_Q:, :]).astype(BF16)
    t = _dot(h_ref[...], wk_ref[...]) + bk_ref[...]
    for kv in range(B_KV_HEADS):
        kk_ref[kv] = t[:, kv * B_HEAD_DIM:(kv + 1) * B_HEAD_DIM].astype(BF16)


def _b_proj(x2, g, wt, bcol, wk, bk, tm=512):
    n, d = x2.shape
    return pl.pallas_call(
        _b_proj_kernel,
        grid=(n // tm,),
        in_specs=[pl.BlockSpec((tm, d), lambda i: (i, 0)),
                  _resident((1, d)),
                  _resident((B_Q + B_KV, d)),
                  _resident((B_Q + B_KV, 1)),
                  _resident((d, B_KV)),
                  _resident((1, B_KV))],
        out_specs=(pl.BlockSpec((tm // BLK, B_HEAD_DIM, N_HEADS * BLK), lambda i: (i, 0, 0)),
                   pl.BlockSpec((B_KV_HEADS, tm, B_HEAD_DIM), lambda i: (0, i, 0)),
                   pl.BlockSpec((B_KV, tm), lambda i: (0, i))),
        out_shape=(jax.ShapeDtypeStruct((n // BLK, B_HEAD_DIM, N_HEADS * BLK), BF16),
                   jax.ShapeDtypeStruct((B_KV_HEADS, n, B_HEAD_DIM), BF16),
                   jax.ShapeDtypeStruct((B_KV, n), BF16)),
        scratch_shapes=[pltpu.VMEM((tm, d), BF16), pltpu.VMEM((d, tm), BF16)],
        compiler_params=_params(1),
        name="b_proj",
    )(x2, g, wt, bcol, wk, bk)


def _swa_kernel(sink_ref, qr_ref, kp_ref, kc_ref, vp_ref, vc_ref, bts_ref, out_ref):
    n = pl.program_id(1)
    c1 = (B_HEAD_DIM ** -0.5) * LOG2E
    group = N_HEADS // B_KV_HEADS
    blocks = (
        (qr_ref.at[0], kp_ref, kc_ref.at[:, 0:BLK, :], vp_ref, vc_ref.at[:, 0:BLK],
         jnp.where(n >= 1, 1, 2)),
        (qr_ref.at[1], kc_ref.at[:, 0:BLK, :], kc_ref.at[:, BLK:2 * BLK, :],
         vc_ref.at[:, 0:BLK], vc_ref.at[:, BLK:2 * BLK], 1))
    tasks = [(qb, kv, g2) for qb in range(2) for kv in range(B_KV_HEADS)
             for g2 in range(group // 2)]

    def scores(task):
        qb, kv, g2 = task
        q_ref, kp, kc = blocks[qb][0:3]
        h0 = kv * group + 2 * g2
        q = q_ref[:, h0 * BLK:(h0 + 2) * BLK]
        return _dot(kp[kv], q), _dot(kc[kv], q)

    def attend(task, lp, lc):
        qb, kv, g2 = task
        vp, vc, prev_kind = blocks[qb][3:6]
        vrows = slice(kv * B_HEAD_DIM, (kv + 1) * B_HEAD_DIM)
        pps, pcs, invs = [], [], []
        for j in range(2):
            h = kv * group + 2 * g2 + j
            lanes = slice(j * BLK, (j + 1) * BLK)
            sink = sink_ref[:, h * BLK:(h + 1) * BLK] * LOG2E
            a = lp[:, lanes] * c1 + bts_ref[prev_kind, h]
            b = lc[:, lanes] * c1 + bts_ref[0, h]
            m = jnp.maximum(jnp.max(jnp.maximum(a, b), axis=0, keepdims=True), sink)
            pp = jnp.exp2(a - m)
            pc = jnp.exp2(b - m)
            den = jnp.sum(pp + pc, axis=0, keepdims=True) + jnp.exp2(sink - m)
            invs.append(1.0 / den)
            pps.append(pp.astype(BF16))
            pcs.append(pc.astype(BF16))
        ot = (_dot(vp[vrows, :], jnp.concatenate(pps, axis=1))
              + _dot(vc[vrows, :], jnp.concatenate(pcs, axis=1)))
        return [ot[:, j * BLK:(j + 1) * BLK] * invs[j] for j in range(2)]

    outs = [[], []]
    ahead = 3
    queue = [scores(task) for task in tasks[:ahead]]
    for t, task in enumerate(tasks):
        if t + ahead < len(tasks):
            queue.append(scores(tasks[t + ahead]))
        outs[task[0]] += attend(task, *queue.pop(0))
    for qb in range(2):
        out_ref[qb * BLK:(qb + 1) * BLK, :] = (
            jnp.concatenate(outs[qb], axis=0).T.astype(BF16))


def _swa_attn(qr, kk, vt, sinks_row, bts, batch, seq):
    nb2 = seq // (2 * BLK)
    cur = lambda b, n: b * nb2 + n
    prev = lambda b, n: 2 * (b * nb2 + n) - jnp.minimum(n, 1)
    return pl.pallas_call(
        _swa_kernel,
        grid=(batch, nb2),
        in_specs=[_resident((1, N_HEADS * BLK)),
                  pl.BlockSpec((2, B_HEAD_DIM, N_HEADS * BLK), lambda b, n: (cur(b, n), 0, 0)),
                  pl.BlockSpec((B_KV_HEADS, BLK, B_HEAD_DIM), lambda b, n: (0, prev(b, n), 0)),
                  pl.BlockSpec((B_KV_HEADS, 2 * BLK, B_HEAD_DIM), lambda b, n: (0, cur(b, n), 0)),
                  pl.BlockSpec((B_KV, BLK), lambda b, n: (0, prev(b, n))),
                  pl.BlockSpec((B_KV, 2 * BLK), lambda b, n: (0, cur(b, n))),
                  _resident((3, N_HEADS, BLK, BLK))],
        out_specs=pl.BlockSpec((2 * BLK, B_Q), lambda b, n: (cur(b, n), 0)),
        out_shape=jax.ShapeDtypeStruct((batch * seq, B_Q), BF16),
        compiler_params=_params(2),
        name="swa_attn",
    )(sinks_row, qr, kk, kk, vt, vt, bts)


FF_CHUNK = 256


def _ffn_kernel(x_ref, o_ref, wo_ref, bo_ref, g_ref, wgu_ref, wd_ref, fg_ref,
                out_ref, h_ref, act_ref, *, d_ff, final):
    x1 = x_ref[...] + _dot(o_ref[...], wo_ref[...]) + bo_ref[...]
    out_ref[...] = x1
    h_ref[...] = _rms(x1, g_ref[...]).astype(BF16)
    for c in range(d_ff // FF_CHUNK):
        gate = _dot(h_ref[...], wgu_ref[:, c * FF_CHUNK:(c + 1) * FF_CHUNK])
        up = _dot(h_ref[...], wgu_ref[:, d_ff + c * FF_CHUNK:d_ff + (c + 1) * FF_CHUNK])
        act_ref[:, c * FF_CHUNK:(c + 1) * FF_CHUNK] = (
            gate * jax.nn.sigmoid(gate) * up).astype(BF16)
    y = out_ref[...] + _dot(act_ref[...], wd_ref[...])
    if final:
        y = _rms(y, fg_ref[...])
    out_ref[...] = y


def _ffn(x2, o, wo, bo, g, wgu_all, wd_all, layer, fg, final, tm=1024):
    n, d = x2.shape
    d_ff = wd_all.shape[1]
    row = lambda i: (i, 0)
    pick = lambda i: (layer, 0, 0)
    return pl.pallas_call(
        functools.partial(_ffn_kernel, d_ff=d_ff, final=final),
        grid=(n // tm,),
        in_specs=[pl.BlockSpec((tm, d), row),
                  pl.BlockSpec((tm, o.shape[1]), row),
                  _resident(wo.shape),
                  _resident((1, d)),
                  _resident((1, d)),
                  pl.BlockSpec((None,) + wgu_all.shape[1:], pick, pipeline_mode=pl.Buffered(1)),
                  pl.BlockSpec((None,) + wd_all.shape[1:], pick, pipeline_mode=pl.Buffered(1)),
                  _resident((1, d))],
        out_specs=pl.BlockSpec((tm, d), row),
        out_shape=jax.ShapeDtypeStruct((n, d), F32),
        scratch_shapes=[pltpu.VMEM((tm, d), BF16),
                        pltpu.VMEM((tm, d_ff), BF16)],
        compiler_params=_params(1),
        name="ffn",
    )(x2, o, wo, bo, g, wgu_all, wd_all, fg)


def kernel(x, rel_bias, attn_norm, ffn_norm, final_norm, a_w_in, a_latent_norm, a_w_uv, a_w_out, b_w_in, b_b_in, b_sinks, b_w_out, b_b_out, ffn_w_gate_up, ffn_w_down):
    batch, seq, d = x.shape
    x2 = x.reshape(batch * seq, d)
    bts, btd = _bias_tiles(rel_bias)
    fg = final_norm.reshape(1, d)

    assert a_w_in.shape[2] == A_IN
    gu_shape, dn_shape = ffn_w_gate_up.shape, ffn_w_down.shape
    qr, qir, wto, kidx, ckv, ckvt, wgu_bf, wd_bf = _a_proj(
        x2, attn_norm[0].reshape(1, d), jnp.swapaxes(a_w_in, 1, 2),
        a_latent_norm[0].reshape(1, A_LATENT),
        ffn_w_gate_up.reshape(-1, gu_shape[-1]), ffn_w_down.reshape(-1, dn_shape[-1]))
    wgu_all, wd_all = wgu_bf.reshape(gu_shape), wd_bf.reshape(dn_shape)
    wuvt = jnp.swapaxes(a_w_uv[0], 1, 2).astype(BF16)
    o = _dsa_attn(qr, qir, wto, kidx, ckv, ckvt, wuvt, btd, batch, seq)
    x2 = _ffn(x2, o, a_w_out[0].astype(BF16), jnp.zeros((1, d), F32),
              ffn_norm[0].reshape(1, d), wgu_all, wd_all, 0, fg, final=False)

    wb, bb = b_w_in[0].astype(BF16), b_b_in[0]
    wqv = jnp.concatenate([wb[:, :B_Q], wb[:, B_Q + B_KV:]], axis=1).T
    bqv = jnp.concatenate([bb[:B_Q], bb[B_Q + B_KV:]]).reshape(-1, 1)
    qr, kk, vt = _b_proj(
        x2, attn_norm[1].reshape(1, d), wqv, bqv, wb[:, B_Q:B_Q + B_KV],
        bb[B_Q:B_Q + B_KV].reshape(1, B_KV))
    sinks_row = jnp.repeat(b_sinks[0], BLK).reshape(1, N_HEADS * BLK)
    o = _swa_attn(qr, kk, vt, sinks_row, bts, batch, seq)
    x2 = _ffn(x2, o, b_w_out[0].astype(BF16), b_b_out[0].reshape(1, d),
              ffn_norm[1].reshape(1, d), wgu_all, wd_all, 1, fg, final=True)
    return x2.reshape(batch, seq, d)
```

```python
import functools
import math

import numpy as np
import jax
import jax.numpy as jnp
from jax import lax
from jax.experimental import pallas as pl
from jax.experimental.pallas import tpu as pltpu

F32 = jnp.float32
BF16 = jnp.bfloat16
I32 = jnp.int32

EPS = 1e-6
N_HEADS = 16
A_LATENT = 128
A_VHEAD = 64
IDX_HEADS = 8
IDX_DIM = 64
TOPK_MAX = 256
B_KV_HEADS = 2
B_HEAD_DIM = 64
REL_BUCKETS = 32
REL_MAX_DIST = 128

BLK = 128
SC = 256
PROJ_ROWS = 512
INT_MIN = -(2 ** 31)
NEG_INF = float("-inf")
LOG2E = math.log2(math.e)
VMEM_LIMIT = 56 * 1024 * 1024


def _rms(xf, g):
    ms = jnp.mean(xf * xf, axis=-1, keepdims=True)
    return xf * lax.rsqrt(ms + EPS) * g


def _dot(a, b):
    return jnp.dot(a, b, preferred_element_type=F32)


def _dot_nt(a, b):
    return lax.dot_general(a, b, (((1,), (1,)), ((), ())), preferred_element_type=F32)


def _params(n_axes):
    return pltpu.CompilerParams(
        dimension_semantics=("arbitrary",) * n_axes, vmem_limit_bytes=VMEM_LIMIT)


def _resident(shape):
    zeros = (0,) * len(shape)
    return pl.BlockSpec(shape, lambda *_: zeros, pipeline_mode=pl.Buffered(1))


def _bucket(rel):
    n = np.maximum(rel, 0)
    max_exact = REL_BUCKETS // 2
    nf = np.maximum(n, max_exact).astype(np.float64)
    large = max_exact + (np.log(nf / max_exact) / math.log(REL_MAX_DIST / max_exact)
                         * (REL_BUCKETS - max_exact)).astype(np.int32)
    large = np.minimum(large, REL_BUCKETS - 1)
    return np.where(n < max_exact, n, large).astype(np.int32)


def _bucket_tables():
    k = np.arange(BLK)[:, None]
    q = np.arange(BLK)[None, :]
    return _bucket(np.stack([q - k, BLK + q - k]))


def _bias_kernel(rb_ref, bidx_ref, bts_ref, btd_ref):
    k = lax.broadcasted_iota(I32, (BLK, BLK), 0)
    q = lax.broadcasted_iota(I32, (BLK, BLK), 1)
    window = (k <= q, k > q)
    for h in range(N_HEADS):
        for kind in range(2):
            bidx = bidx_ref[kind]

            t = jnp.zeros((BLK, BLK), F32)
            for b in range(REL_BUCKETS):
                t = jnp.where(bidx == b, rb_ref[b, h], t)
            bts_ref[kind, h] = jnp.where(window[kind], t * LOG2E, NEG_INF)
            btd_ref[kind, h] = (t - rb_ref[REL_BUCKETS - 1, h]) * LOG2E
        bts_ref[2, h] = jnp.full((BLK, BLK), NEG_INF, F32)
        btd_ref[2, h] = jnp.zeros((BLK, BLK), F32)


def _bias_tiles(rel_bias):
    return pl.pallas_call(
        _bias_kernel,
        out_shape=(jax.ShapeDtypeStruct((3, N_HEADS, BLK, BLK), F32),
                   jax.ShapeDtypeStruct((3, N_HEADS, BLK, BLK), F32)),
        in_specs=[pl.BlockSpec(memory_space=pltpu.SMEM),
                  pl.BlockSpec(memory_space=pltpu.VMEM)],
        out_specs=(pl.BlockSpec(memory_space=pltpu.VMEM),
                   pl.BlockSpec(memory_space=pltpu.VMEM)),
        name="bias_tiles",
    )(rel_bias, jnp.asarray(_bucket_tables()))


A_QLAT = N_HEADS * A_LATENT
A_QIDX = IDX_HEADS * IDX_DIM
A_QIDX_ROW = A_QLAT + A_LATENT
A_KIDX_ROW = A_QIDX_ROW + A_QIDX
A_IN = A_KIDX_ROW + IDX_DIM + IDX_HEADS
A_KC_ROW = A_KIDX_ROW + IDX_DIM + 16
A_WT_ROWS = A_KC_ROW + 2 * A_LATENT


def _a_proj_kernel(x_ref, g_ref, w_ref, lg_ref, wgu_ref, wd_ref,
                   qr_ref, qir_ref, wto_ref, kidx_ref, ckv_ref, ckvt_ref,
                   wgu_bf_ref, wd_bf_ref, h_ref, ht_ref, wt_ref):
    tm = x_ref.shape[0]

    @pl.when(pl.program_id(0) == 0)
    def _():
        wt_ref[0:A_IN, :] = w_ref[...].astype(BF16)
        wt_ref[A_IN:A_KC_ROW, :] = jnp.zeros((A_KC_ROW - A_IN, wt_ref.shape[1]), BF16)
        wt_ref[A_KC_ROW:A_KC_ROW + A_LATENT, :] = w_ref[A_QLAT:A_QIDX_ROW, :].astype(BF16)
        wt_ref[A_KC_ROW + A_LATENT:A_KC_ROW + A_LATENT + IDX_DIM, :] = (
            w_ref[A_KIDX_ROW:A_KIDX_ROW + IDX_DIM, :].astype(BF16))
        wt_ref[A_KC_ROW + A_LATENT + IDX_DIM:, :] = jnp.zeros(
            (2 * A_LATENT - A_LATENT - IDX_DIM, wt_ref.shape[1]), BF16)

    wgu_bf_ref[...] = wgu_ref[...].astype(BF16)
    wd_bf_ref[...] = wd_ref[...].astype(BF16)
    hf = _rms(x_ref[...], g_ref[...])
    h_ref[...] = hf.astype(BF16)
    ht_ref[...] = hf.T.astype(BF16)
    heads_per_dot = PROJ_ROWS // A_LATENT
    for c in range(A_QLAT // PROJ_ROWS):
        r = _dot(wt_ref[c * PROJ_ROWS:(c + 1) * PROJ_ROWS, :], ht_ref[...]).astype(BF16)
        for hh in range(heads_per_dot):
            h = heads_per_dot * c + hh
            for blk in range(tm // BLK):
                qr_ref[blk, :, h * BLK:(h + 1) * BLK] = (
                    r[hh * A_LATENT:(hh + 1) * A_LATENT, blk * BLK:(blk + 1) * BLK])
    r = _dot(wt_ref[A_QIDX_ROW:A_KIDX_ROW, :], ht_ref[...]).astype(BF16)
    for j in range(IDX_HEADS):
        for blk in range(tm // BLK):
            qir_ref[blk, :, j * BLK:(j + 1) * BLK] = (
                r[j * IDX_DIM:(j + 1) * IDX_DIM, blk * BLK:(blk + 1) * BLK])
    w = _dot(wt_ref[A_KIDX_ROW + IDX_DIM:A_KC_ROW, :], ht_ref[...])
    wto_ref[...] = w[:IDX_HEADS] * (IDX_HEADS ** -0.5)
    t = _dot_nt(h_ref[...], wt_ref[A_KC_ROW:, :])
    ckv = _rms(t[:, :A_LATENT], lg_ref[...])
    ckv_ref[...] = ckv.astype(BF16)
    kidx_ref[...] = t[:, A_LATENT:A_LATENT + IDX_DIM].astype(BF16)
    for b2 in range(tm // SC):
        ckvt_ref[b2] = ckv[b2 * SC:(b2 + 1) * SC, :].T.astype(BF16)


def _a_proj(x2, g, w, lg, wgu2, wd2, tm=512):
    n, d = x2.shape
    steps = n // tm
    gu_rows, dn_rows = wgu2.shape[0] // steps, wd2.shape[0] // steps
    row = lambda i: (i, 0)
    blk3 = lambda i: (i, 0, 0)
    return pl.pallas_call(
        _a_proj_kernel,
        grid=(steps,),
        in_specs=[pl.BlockSpec((tm, d), row),
                  _resident((1, d)),
                  pl.BlockSpec((None,) + w.shape[1:], lambda i: (0, 0, 0),
                               pipeline_mode=pl.Buffered(1)),
                  _resident((1, A_LATENT)),
                  pl.BlockSpec((gu_rows, wgu2.shape[1]), row),
                  pl.BlockSpec((dn_rows, wd2.shape[1]), row)],
        out_specs=(pl.BlockSpec((tm // BLK, A_LATENT, A_QLAT), blk3),
                   pl.BlockSpec((tm // BLK, IDX_DIM, IDX_HEADS * BLK), blk3),
                   pl.BlockSpec((IDX_HEADS, tm), lambda i: (0, i)),
                   pl.BlockSpec((tm, IDX_DIM), row),
                   pl.BlockSpec((tm, A_LATENT), row),
                   pl.BlockSpec((tm // SC, A_LATENT, SC), blk3),
                   pl.BlockSpec((gu_rows, wgu2.shape[1]), row),
                   pl.BlockSpec((dn_rows, wd2.shape[1]), row)),
        out_shape=(jax.ShapeDtypeStruct((n // BLK, A_LATENT, A_QLAT), BF16),
                   jax.ShapeDtypeStruct((n // BLK, IDX_DIM, IDX_HEADS * BLK), BF16),
                   jax.ShapeDtypeStruct((IDX_HEADS, n), F32),
                   jax.ShapeDtypeStruct((n, IDX_DIM), BF16),
                   jax.ShapeDtypeStruct((n, A_LATENT), BF16),
                   jax.ShapeDtypeStruct((n // SC, A_LATENT, SC), BF16),
                   jax.ShapeDtypeStruct(wgu2.shape, BF16),
                   jax.ShapeDtypeStruct(wd2.shape, BF16)),
        scratch_shapes=[pltpu.VMEM((tm, d), BF16), pltpu.VMEM((d, tm), BF16),
                        pltpu.VMEM((A_WT_ROWS, d), BF16)],
        compiler_params=_params(1),
        name="a_proj",
    )(x2, g, w, lg, wgu2, wd2)


def _dsa_kernel(qr_ref, qirn_ref, wtn_ref, kidx_ref, ckv_ref, ckvt_ref, wuvt_ref,
                btd_ref, tri_ref, out_ref,
                score_ref, acc_ref, l_ref, *, k_sel, n_sc):
    i = pl.program_id(1)
    nsc = i // 2 + 1
    kf = float(k_sel)
    krow = lax.broadcasted_iota(I32, (SC, BLK), 0)
    qpos = i * BLK + lax.broadcasted_iota(I32, (SC, BLK), 1)

    def causal(sc):
        return sc * SC + krow <= qpos

    def key_rows(ref, sc, count=1):
        return ref[pl.ds(pl.multiple_of(sc * SC, SC), count * SC), :]

    def colsum(x):
        return jnp.sum(x.reshape(SC // 8, 8, BLK), axis=0)

    w2 = wtn_ref[...] * (IDX_DIM ** -0.5)
    nsc_next = (i + 1) // 2 + 1
    qpos_next = qpos + BLK

    def index_next(sc0, count):
        d = _dot(key_rows(kidx_ref, sc0, count), qirn_ref[0])
        for c in range(count):
            rows = slice(c * SC, (c + 1) * SC)
            s = w2[0:1, :] * jnp.maximum(d[rows, 0:BLK], 0.0)
            for j in range(1, IDX_HEADS):
                s = s + w2[j:j + 1, :] * jnp.maximum(d[rows, j * BLK:(j + 1) * BLK], 0.0)
            score_ref[sc0 + c] = jnp.where((sc0 + c) * SC + krow <= qpos_next, s, NEG_INF)

    def put_threshold(thr, n_ge):
        score_ref[n_sc, 0:8, :] = jnp.broadcast_to(thr, (8, BLK))
        score_ref[n_sc, 8:16, :] = jnp.broadcast_to(n_ge, (8, BLK))

    @pl.when((i + 1) * BLK <= k_sel)
    def _():
        def body(sc, carry):
            score_ref[sc] = jnp.where(causal(sc), 0.0, NEG_INF)
            return carry
        lax.fori_loop(0, nsc, body, 0)
        put_threshold(jnp.zeros((1, BLK), F32), jnp.full((1, BLK), kf, F32))

    def key_to_float(key):
        return lax.bitcast_convert_type(key ^ ((key >> 31) & 0x7FFFFFFF), F32)

    def search(nblk):
        def count_ge(cand):
            cb = jnp.broadcast_to(cand, (8, BLK))
            parts = [jnp.zeros((8, BLK), F32)] * 8
            for t in range(nblk * BLK // 8):
                sc, r = divmod(t * 8, SC)
                ge = score_ref[sc, r:r + 8, :] >= cb
                parts[t % 8] = parts[t % 8] + jnp.where(ge, 1.0, 0.0)
            part = sum(parts[1:], parts[0])
            return jnp.sum(part, axis=0, keepdims=True)

        n0 = count_ge(jnp.zeros((1, BLK), F32))
        take = n0 >= kf
        ans = jnp.where(take, 0, INT_MIN)
        n_ans = jnp.where(take, n0, kf)
        first = jnp.left_shift(1, 30)
        cand_f = jnp.where(take, key_to_float(first), key_to_float(INT_MIN | first))

        def bit_body(it, carry):
            ans, cand_f, n_ans = carry
            cand = ans | jnp.left_shift(1, 30 - it)
            nxt = jnp.left_shift(1, jnp.maximum(29 - it, 0))
            f_take, f_keep = key_to_float(cand | nxt), key_to_float(ans | nxt)
            n_cand = count_ge(cand_f)
            take = n_cand >= kf
            return (jnp.where(take, cand, ans), jnp.where(take, f_take, f_keep),
                    jnp.where(take, n_cand, n_ans))
        ans, _, n_ans = lax.fori_loop(0, 31, bit_body, (ans, cand_f, n_ans))
        return key_to_float(ans), n_ans

    @pl.when((i + 1) * BLK > k_sel)
    def _():
        for nblk in range(k_sel // BLK + 1, 2 * n_sc + 1):
            @pl.when(i + 1 == nblk)
            def _(nblk=nblk):
                put_threshold(*search(nblk))
        thr = score_ref[n_sc, 0:1, :]
        n_ge = score_ref[n_sc, 8:9, :]

        @pl.when(jnp.max(n_ge) > kf)
        def _():
            def gt_body(sc, part):
                return part + colsum(jnp.where(score_ref[sc] > thr, 1.0, 0.0))
            n_gt = jnp.sum(lax.fori_loop(0, nsc, gt_body, jnp.zeros((8, BLK), F32)),
                           axis=0, keepdims=True)
            need = kf - n_gt

            def mask_body(sc, off):
                score = score_ref[sc]
                eq = score == thr
                eqf = jnp.where(eq, 1.0, 0.0)
                before = _dot(tri_ref[...], eqf.astype(BF16)) + off
                score_ref[sc] = jnp.where(eq & (before >= need), NEG_INF, score)
                return off + jnp.sum(eqf, axis=0, keepdims=True)
            lax.fori_loop(0, nsc, mask_body, jnp.zeros((1, BLK), F32))

    c1 = (A_LATENT ** -0.5) * LOG2E
    hcols = N_HEADS * BLK
    thr_a = score_ref[n_sc, 0:1, :]

    def logits_group(sc0, m, count, kinds=None):
        lt = _dot(key_rows(ckv_ref, sc0, count), qr_ref[0])
        madds = [jnp.where(score_ref[sc0 + c] >= thr_a, 0.0, NEG_INF) for c in range(count)]
        new_m = []
        for h in range(N_HEADS):
            hs = slice(h * BLK, (h + 1) * BLK)
            mh = m[:, hs]
            for c in range(count):
                v = lt[c * SC:(c + 1) * SC, hs] * c1 + madds[c]
                if kinds is not None and kinds[2 * c:2 * c + 2] != (None, None):
                    halves = []
                    for half in range(2):
                        part = v[half * BLK:(half + 1) * BLK]
                        kind = kinds[2 * c + half]
                        halves.append(part if kind is None else part + btd_ref[kind, h])
                    v = jnp.concatenate(halves, axis=0)
                l_ref[sc0 + c, h] = v
                mh = jnp.maximum(mh, jnp.max(v, axis=0, keepdims=True))
            new_m.append(mh)
        return jnp.concatenate(new_m, axis=1)

    n_far = nsc - 2
    m = jnp.full((1, hcols), NEG_INF, F32)
    m = lax.fori_loop(0, n_far // 2, lambda t, m: logits_group(2 * t, m, 2), m)
    m = lax.cond((n_far > 0) & (n_far % 2 == 1),
                 lambda m: logits_group(n_far - 1, m, 1), lambda m: m, m)
    even = i % 2 == 0
    m = lax.cond(
        nsc >= 2,
        lambda m: lax.cond(even,
                           lambda m: logits_group(nsc - 2, m, 2, (None, 1, 0, None)),
                           lambda m: logits_group(nsc - 2, m, 2, (None, None, 1, 0)), m),
        lambda m: lax.cond(even,
                           lambda m: logits_group(0, m, 1, (0, None)),
                           lambda m: logits_group(0, m, 1, (1, 0)), m),
        m)

    def pv_group(sc0, s, count):
        index_next(sc0, count)
        vt = jnp.concatenate([ckvt_ref[sc0 + c] for c in range(count)], axis=1)
        new_s = []
        for hp in range(N_HEADS // 2):
            cols = []
            for h in (2 * hp, 2 * hp + 1):
                hs = slice(h * BLK, (h + 1) * BLK)
                sh = s[:, hs]
                ps = []
                for c in range(count):
                    p = jnp.exp2(l_ref[sc0 + c, h] - m[:, hs])
                    sh = sh + colsum(p)
                    ps.append(p.astype(BF16))
                new_s.append(sh)
                cols.append(jnp.concatenate(ps, axis=0))
            pair = jnp.concatenate(cols, axis=1)
            acc_ref[hp] = acc_ref[hp] + _dot(vt, pair)
        return jnp.concatenate(new_s, axis=1)

    acc_ref[...] = jnp.zeros(acc_ref.shape, F32)
    s = lax.fori_loop(0, nsc // 2, lambda t, s: pv_group(2 * t, s, 2),
                      jnp.zeros((8, hcols), F32))
    s = lax.cond(nsc % 2 == 1, lambda s: pv_group(nsc - 1, s, 1), lambda s: s, s)
    inv = 1.0 / jnp.sum(s, axis=0, keepdims=True)

    @pl.when((nsc_next > nsc) & (nsc_next <= n_sc))
    def _():
        index_next(nsc, 1)

    outs = []
    for h in range(N_HEADS):
        hs = slice(h * BLK, (h + 1) * BLK)
        a = acc_ref[h // 2, :, (h % 2) * BLK:(h % 2 + 1) * BLK]
        o = (a * inv[:, hs]).astype(BF16)
        outs.append(_dot(wuvt_ref[h], o))
    out_ref[...] = jnp.concatenate(outs, axis=0).T.astype(BF16)


def _dsa_attn(qr, qir, wt, kidx, ckv, ckvt, wuvt, btd, batch, seq):
    nblk = seq // BLK
    n_sc = seq // SC
    k_sel = min(TOPK_MAX, seq // 4)
    hcols = N_HEADS * BLK
    tri = jnp.asarray(np.tril(np.ones((SC, SC), np.float32), -1), BF16)
    return pl.pallas_call(
        functools.partial(_dsa_kernel, k_sel=k_sel, n_sc=n_sc),
        grid=(batch, nblk),
        in_specs=[pl.BlockSpec((1, A_LATENT, hcols), lambda b, i: (b * nblk + i, 0, 0)),
                  pl.BlockSpec((1, IDX_DIM, IDX_HEADS * BLK),
                               lambda b, i: (b * nblk + jnp.minimum(i + 1, nblk - 1), 0, 0)),
                  pl.BlockSpec((IDX_HEADS, BLK),
                               lambda b, i: (0, b * nblk + jnp.minimum(i + 1, nblk - 1))),
                  pl.BlockSpec((seq, IDX_DIM), lambda b, i: (b, 0)),
                  pl.BlockSpec((seq, A_LATENT), lambda b, i: (b, 0)),
                  pl.BlockSpec((n_sc, A_LATENT, SC), lambda b, i: (b, 0, 0)),
                  _resident((N_HEADS, A_VHEAD, A_LATENT)),
                  _resident((3, N_HEADS, BLK, BLK)),
                  _resident((SC, SC))],
        out_specs=pl.BlockSpec((BLK, N_HEADS * A_VHEAD), lambda b, i: (b * nblk + i, 0)),
        out_shape=jax.ShapeDtypeStruct((batch * seq, N_HEADS * A_VHEAD), BF16),
        scratch_shapes=[
            pltpu.VMEM((n_sc + 1, SC, BLK), F32),
            pltpu.VMEM((N_HEADS // 2, A_LATENT, 2 * BLK), F32),
            pltpu.VMEM((n_sc, N_HEADS, SC, BLK), F32),
        ],
        compiler_params=_params(2),
        name="dsa_attn",
    )(qr, qir, wt, kidx, ckv, ckvt, wuvt, btd, tri)


B_Q = N_HEADS * B_HEAD_DIM
B_KV = B_KV_HEADS * B_HEAD_DIM


def _b_proj_kernel(x_ref, g_ref, wt_ref, bcol_ref, wk_ref, bk_ref,
                   qr_ref, kk_ref, vt_ref, h_ref, ht_ref):
    tm = x_ref.shape[0]
    hf = _rms(x_ref[...], g_ref[...])
    h_ref[...] = hf.astype(BF16)
    ht_ref[...] = hf.T.astype(BF16)
    heads_per_dot = PROJ_ROWS // B_HEAD_DIM
    for c in range(B_Q // PROJ_ROWS):
        rows = slice(c * PROJ_ROWS, (c + 1) * PROJ_ROWS)
        r = (_dot(wt_ref[rows, :], ht_ref[...]) + bcol_ref[rows, :]).astype(BF16)
        for hh in range(heads_per_dot):
            h = c * heads_per_dot + hh
            for blk in range(tm // BLK):
                qr_ref[blk, :, h * BLK:(h + 1) * BLK] = (
                    r[hh * B_HEAD_DIM:(hh + 1) * B_HEAD_DIM, blk * BLK:(blk + 1) * BLK])
    vt_ref[...] = (_dot(wt_ref[B_Q:, :], ht_ref[...]) + bcol_ref[B_Q:, :]).astype(BF16)
    t = _dot(h_ref[...], wk_ref[...]) + bk_ref[...]
    for kv in range(B_KV_HEADS):
        kk_ref[kv] = t[:, kv * B_HEAD_DIM:(kv + 1) * B_HEAD_DIM].astype(BF16)


def _b_proj(x2, g, wt, bcol, wk, bk, tm=512):
    n, d = x2.shape
    return pl.pallas_call(
        _b_proj_kernel,
        grid=(n // tm,),
        in_specs=[pl.BlockSpec((tm, d), lambda i: (i, 0)),
                  _resident((1, d)),
                  _resident((B_Q + B_KV, d)),
                  _resident((B_Q + B_KV, 1)),
                  _resident((d, B_KV)),
                  _resident((1, B_KV))],
        out_specs=(pl.BlockSpec((tm // BLK, B_HEAD_DIM, N_HEADS * BLK), lambda i: (i, 0, 0)),
                   pl.BlockSpec((B_KV_HEADS, tm, B_HEAD_DIM), lambda i: (0, i, 0)),
                   pl.BlockSpec((B_KV, tm), lambda i: (0, i))),
        out_shape=(jax.ShapeDtypeStruct((n // BLK, B_HEAD_DIM, N_HEADS * BLK), BF16),
                   jax.ShapeDtypeStruct((B_KV_HEADS, n, B_HEAD_DIM), BF16),
                   jax.ShapeDtypeStruct((B_KV, n), BF16)),
        scratch_shapes=[pltpu.VMEM((tm, d), BF16), pltpu.VMEM((d, tm), BF16)],
        compiler_params=_params(1),
        name="b_proj",
    )(x2, g, wt, bcol, wk, bk)


def _swa_kernel(sink_ref, qr_ref, kp_ref, kc_ref, vp_ref, vc_ref, bts_ref, out_ref):
    n = pl.program_id(1)
    c1 = (B_HEAD_DIM ** -0.5) * LOG2E
    group = N_HEADS // B_KV_HEADS
    blocks = (
        (qr_ref.at[0], kp_ref, kc_ref.at[:, 0:BLK, :], vp_ref, vc_ref.at[:, 0:BLK],
         jnp.where(n >= 1, 1, 2)),
        (qr_ref.at[1], kc_ref.at[:, 0:BLK, :], kc_ref.at[:, BLK:2 * BLK, :],
         vc_ref.at[:, 0:BLK], vc_ref.at[:, BLK:2 * BLK], 1))
    tasks = [(qb, kv, g2) for qb in range(2) for kv in range(B_KV_HEADS)
             for g2 in range(group // 2)]

    def scores(task):
        qb, kv, g2 = task
        q_ref, kp, kc = blocks[qb][0:3]
        h0 = kv * group + 2 * g2
        q = q_ref[:, h0 * BLK:(h0 + 2) * BLK]
        return _dot(kp[kv], q), _dot(kc[kv], q)

    def attend(task, lp, lc):
        qb, kv, g2 = task
        vp, vc, prev_kind = blocks[qb][3:6]
        vrows = slice(kv * B_HEAD_DIM, (kv + 1) * B_HEAD_DIM)
        pps, pcs, invs = [], [], []
        for j in range(2):
            h = kv * group + 2 * g2 + j
            lanes = slice(j * BLK, (j + 1) * BLK)
            sink = sink_ref[:, h * BLK:(h + 1) * BLK] * LOG2E
            a = lp[:, lanes] * c1 + bts_ref[prev_kind, h]
            b = lc[:, lanes] * c1 + bts_ref[0, h]
            m = jnp.maximum(jnp.max(jnp.maximum(a, b), axis=0, keepdims=True), sink)
            pp = jnp.exp2(a - m)
            pc = jnp.exp2(b - m)
            den = jnp.sum(pp + pc, axis=0, keepdims=True) + jnp.exp2(sink - m)
            invs.append(1.0 / den)
            pps.append(pp.astype(BF16))
            pcs.append(pc.astype(BF16))
        ot = (_dot(vp[vrows, :], jnp.concatenate(pps, axis=1))
              + _dot(vc[vrows, :], jnp.concatenate(pcs, axis=1)))
        return [ot[:, j * BLK:(j + 1) * BLK] * invs[j] for j in range(2)]

    outs = [[], []]
    ahead = 3
    queue = [scores(task) for task in tasks[:ahead]]
    for t, task in enumerate(tasks):
        if t + ahead < len(tasks):
            queue.append(scores(tasks[t + ahead]))
        outs[task[0]] += attend(task, *queue.pop(0))
    for qb in range(2):
        out_ref[qb * BLK:(qb + 1) * BLK, :] = (
            jnp.concatenate(outs[qb], axis=0).T.astype(BF16))


def _swa_attn(qr, kk, vt, sinks_row, bts, batch, seq):
    nb2 = seq // (2 * BLK)
    cur = lambda b, n: b * nb2 + n
    prev = lambda b, n: 2 * (b * nb2 + n) - jnp.minimum(n, 1)
    return pl.pallas_call(
        _swa_kernel,
        grid=(batch, nb2),
        in_specs=[_resident((1, N_HEADS * BLK)),
                  pl.BlockSpec((2, B_HEAD_DIM, N_HEADS * BLK), lambda b, n: (cur(b, n), 0, 0)),
                  pl.BlockSpec((B_KV_HEADS, BLK, B_HEAD_DIM), lambda b, n: (0, prev(b, n), 0)),
                  pl.BlockSpec((B_KV_HEADS, 2 * BLK, B_HEAD_DIM), lambda b, n: (0, cur(b, n), 0)),
                  pl.BlockSpec((B_KV, BLK), lambda b, n: (0, prev(b, n))),
                  pl.BlockSpec((B_KV, 2 * BLK), lambda b, n: (0, cur(b, n))),
                  _resident((3, N_HEADS, BLK, BLK))],
        out_specs=pl.BlockSpec((2 * BLK, B_Q), lambda b, n: (cur(b, n), 0)),
        out_shape=jax.ShapeDtypeStruct((batch * seq, B_Q), BF16),
        compiler_params=_params(2),
        name="swa_attn",
    )(sinks_row, qr, kk, kk, vt, vt, bts)


FF_CHUNK = 256


def _ffn_kernel(x_ref, o_ref, wo_ref, bo_ref, g_ref, wgu_ref, wd_ref, fg_ref,
                out_ref, h_ref, act_ref, *, d_ff, final):
    x1 = x_ref[...] + _dot(o_ref[...], wo_ref[...]) + bo_ref[...]
    out_ref[...] = x1
    h_ref[...] = _rms(x1, g_ref[...]).astype(BF16)
    for c in range(d_ff // FF_CHUNK):
        gate = _dot(h_ref[...], wgu_ref[:, c * FF_CHUNK:(c + 1) * FF_CHUNK])
        up = _dot(h_ref[...], wgu_ref[:, d_ff + c * FF_CHUNK:d_ff + (c + 1) * FF_CHUNK])
        act_ref[:, c * FF_CHUNK:(c + 1) * FF_CHUNK] = (
            gate * jax.nn.sigmoid(gate) * up).astype(BF16)
    y = out_ref[...] + _dot(act_ref[...], wd_ref[...])
    if final:
        y = _rms(y, fg_ref[...])
    out_ref[...] = y


def _ffn(x2, o, wo, bo, g, wgu_all, wd_all, layer, fg, final, tm=1024):
    n, d = x2.shape
    d_ff = wd_all.shape[1]
    row = lambda i: (i, 0)
    pick = lambda i: (layer, 0, 0)
    return pl.pallas_call(
        functools.partial(_ffn_kernel, d_ff=d_ff, final=final),
        grid=(n // tm,),
        in_specs=[pl.BlockSpec((tm, d), row),
                  pl.BlockSpec((tm, o.shape[1]), row),
                  _resident(wo.shape),
                  _resident((1, d)),
                  _resident((1, d)),
                  pl.BlockSpec((None,) + wgu_all.shape[1:], pick, pipeline_mode=pl.Buffered(1)),
                  pl.BlockSpec((None,) + wd_all.shape[1:], pick, pipeline_mode=pl.Buffered(1)),
                  _resident((1, d))],
        out_specs=pl.BlockSpec((tm, d), row),
        out_shape=jax.ShapeDtypeStruct((n, d), F32),
        scratch_shapes=[pltpu.VMEM((tm, d), BF16),
                        pltpu.VMEM((tm, d_ff), BF16)],
        compiler_params=_params(1),
        name="ffn",
    )(x2, o, wo, bo, g, wgu_all, wd_all, fg)


def kernel(x, rel_bias, attn_norm, ffn_norm, final_norm, a_w_in, a_latent_norm, a_w_uv, a_w_out, b_w_in, b_b_in, b_sinks, b_w_out, b_b_out, ffn_w_gate_up, ffn_w_down):
    batch, seq, d = x.shape
    assert attn_norm.shape[0] == 2 and a_w_in.shape[0] == 1 and b_w_in.shape[0] == 1
    assert seq % SC == 0 and (batch * seq) % 1024 == 0 and d % BLK == 0
    assert a_w_in.shape[2] == A_IN and b_w_in.shape[2] == B_Q + 2 * B_KV
    assert rel_bias.shape == (REL_BUCKETS, N_HEADS)
    x2 = x.reshape(batch * seq, d)
    bts, btd = _bias_tiles(rel_bias)
    fg = final_norm.reshape(1, d)

    gu_shape, dn_shape = ffn_w_gate_up.shape, ffn_w_down.shape
    qr, qir, wto, kidx, ckv, ckvt, wgu_bf, wd_bf = _a_proj(
        x2, attn_norm[0].reshape(1, d), jnp.swapaxes(a_w_in, 1, 2),
        a_latent_norm[0].reshape(1, A_LATENT),
        ffn_w_gate_up.reshape(-1, gu_shape[-1]), ffn_w_down.reshape(-1, dn_shape[-1]))
    wgu_all, wd_all = wgu_bf.reshape(gu_shape), wd_bf.reshape(dn_shape)
    wuvt = jnp.swapaxes(a_w_uv[0], 1, 2).astype(BF16)
    o = _dsa_attn(qr, qir, wto, kidx, ckv, ckvt, wuvt, btd, batch, seq)
    x2 = _ffn(x2, o, a_w_out[0].astype(BF16), jnp.zeros((1, d), F32),
              ffn_norm[0].reshape(1, d), wgu_all, wd_all, 0, fg, final=False)

    wb, bb = b_w_in[0].astype(BF16), b_b_in[0]
    wqv = jnp.concatenate([wb[:, :B_Q], wb[:, B_Q + B_KV:]], axis=1).T
    bqv = jnp.concatenate([bb[:B_Q], bb[B_Q + B_KV:]]).reshape(-1, 1)
    qr, kk, vt = _b_proj(
        x2, attn_norm[1].reshape(1, d), wqv, bqv, wb[:, B_Q:B_Q + B_KV],
        bb[B_Q:B_Q + B_KV].reshape(1, B_KV))
    sinks_row = jnp.repeat(b_sinks[0], BLK).reshape(1, N_HEADS * BLK)
    o = _swa_attn(qr, kk, vt, sinks_row, bts, batch, seq)
    x2 = _ffn(x2, o, b_w_out[0].astype(BF16), b_b_out[0].reshape(1, d),
              ffn_norm[1].reshape(1, d), wgu_all, wd_all, 1, fg, final=True)
    return x2.reshape(batch, seq, d)
```

```python
import functools
import math

import numpy as np
import jax
import jax.numpy as jnp
from jax import lax
from jax.experimental import pallas as pl
from jax.experimental.pallas import tpu as pltpu

F32 = jnp.float32
BF16 = jnp.bfloat16
I32 = jnp.int32

EPS = 1e-6
N_HEADS = 16
A_LATENT = 128
A_VHEAD = 64
IDX_HEADS = 8
IDX_DIM = 64
TOPK_MAX = 256
B_KV_HEADS = 2
B_HEAD_DIM = 64
REL_BUCKETS = 32
REL_MAX_DIST = 128

BLK = 128
SC = 256
PROJ_ROWS = 512
SWA_TASK_HEADS = 1
SWA_AHEAD = 4
INT_MIN = -(2 ** 31)
NEG_INF = float("-inf")
LOG2E = math.log2(math.e)
VMEM_LIMIT = 56 * 1024 * 1024


def _rms(xf, g):
    ms = jnp.mean(xf * xf, axis=-1, keepdims=True)
    return xf * lax.rsqrt(ms + EPS) * g


def _dot(a, b):
    return jnp.dot(a, b, preferred_element_type=F32)


def _dot_nt(a, b):
    return lax.dot_general(a, b, (((1,), (1,)), ((), ())), preferred_element_type=F32)


def _params(n_axes):
    return pltpu.CompilerParams(
        dimension_semantics=("arbitrary",) * n_axes, vmem_limit_bytes=VMEM_LIMIT)


def _resident(shape):
    zeros = (0,) * len(shape)
    return pl.BlockSpec(shape, lambda *_: zeros, pipeline_mode=pl.Buffered(1))


def _bucket(rel):
    n = np.maximum(rel, 0)
    max_exact = REL_BUCKETS // 2
    nf = np.maximum(n, max_exact).astype(np.float64)
    large = max_exact + (np.log(nf / max_exact) / math.log(REL_MAX_DIST / max_exact)
                         * (REL_BUCKETS - max_exact)).astype(np.int32)
    large = np.minimum(large, REL_BUCKETS - 1)
    return np.where(n < max_exact, n, large).astype(np.int32)


def _bucket_tables():
    k = np.arange(BLK)[:, None]
    q = np.arange(BLK)[None, :]
    return _bucket(np.stack([q - k, BLK + q - k]))


def _bias_kernel(rb_ref, bidx_ref, bts_ref, btd_ref):
    k = lax.broadcasted_iota(I32, (BLK, BLK), 0)
    q = lax.broadcasted_iota(I32, (BLK, BLK), 1)
    window = (k <= q, k > q)
    for h in range(N_HEADS):
        for kind in range(2):
            bidx = bidx_ref[kind]

            t = jnp.zeros((BLK, BLK), F32)
            for b in range(REL_BUCKETS):
                t = jnp.where(bidx == b, rb_ref[b, h], t)
            bts_ref[kind, h] = jnp.where(window[kind], t * LOG2E, NEG_INF)
            btd_ref[kind, h] = (t - rb_ref[REL_BUCKETS - 1, h]) * LOG2E
        bts_ref[2, h] = jnp.full((BLK, BLK), NEG_INF, F32)
        btd_ref[2, h] = jnp.zeros((BLK, BLK), F32)


def _bias_tiles(rel_bias):
    return pl.pallas_call(
        _bias_kernel,
        out_shape=(jax.ShapeDtypeStruct((3, N_HEADS, BLK, BLK), F32),
                   jax.ShapeDtypeStruct((3, N_HEADS, BLK, BLK), F32)),
        in_specs=[pl.BlockSpec(memory_space=pltpu.SMEM),
                  pl.BlockSpec(memory_space=pltpu.VMEM)],
        out_specs=(pl.BlockSpec(memory_space=pltpu.VMEM),
                   pl.BlockSpec(memory_space=pltpu.VMEM)),
        name="bias_tiles",
    )(rel_bias, jnp.asarray(_bucket_tables()))


A_QLAT = N_HEADS * A_LATENT
A_QIDX = IDX_HEADS * IDX_DIM
A_QIDX_ROW = A_QLAT + A_LATENT
A_KIDX_ROW = A_QIDX_ROW + A_QIDX
A_IN = A_KIDX_ROW + IDX_DIM + IDX_HEADS
A_KC_ROW = A_KIDX_ROW + IDX_DIM + 16
A_WT_ROWS = A_KC_ROW + 2 * A_LATENT


def _a_proj_kernel(x_ref, g_ref, w_ref, lg_ref, wgu_ref, wd_ref,
                   qr_ref, qir_ref, wto_ref, kidx_ref, ckv_ref, ckvt_ref,
                   wgu_bf_ref, wd_bf_ref, h_ref, ht_ref, wt_ref):
    tm = x_ref.shape[0]

    @pl.when(pl.program_id(0) == 0)
    def _():
        wt_ref[0:A_IN, :] = w_ref[...].astype(BF16)
        wt_ref[A_IN:A_KC_ROW, :] = jnp.zeros((A_KC_ROW - A_IN, wt_ref.shape[1]), BF16)
        wt_ref[A_KC_ROW:A_KC_ROW + A_LATENT, :] = w_ref[A_QLAT:A_QIDX_ROW, :].astype(BF16)
        wt_ref[A_KC_ROW + A_LATENT:A_KC_ROW + A_LATENT + IDX_DIM, :] = (
            w_ref[A_KIDX_ROW:A_KIDX_ROW + IDX_DIM, :].astype(BF16))
        wt_ref[A_KC_ROW + A_LATENT + IDX_DIM:, :] = jnp.zeros(
            (2 * A_LATENT - A_LATENT - IDX_DIM, wt_ref.shape[1]), BF16)

    wgu_bf_ref[...] = wgu_ref[...].astype(BF16)
    wd_bf_ref[...] = wd_ref[...].astype(BF16)
    hf = _rms(x_ref[...], g_ref[...])
    h_ref[...] = hf.astype(BF16)
    ht_ref[...] = hf.T.astype(BF16)
    heads_per_dot = PROJ_ROWS // A_LATENT
    for c in range(A_QLAT // PROJ_ROWS):
        r = _dot(wt_ref[c * PROJ_ROWS:(c + 1) * PROJ_ROWS, :], ht_ref[...]).astype(BF16)
        for hh in range(heads_per_dot):
            h = heads_per_dot * c + hh
            for blk in range(tm // BLK):
                qr_ref[blk, :, h * BLK:(h + 1) * BLK] = (
                    r[hh * A_LATENT:(hh + 1) * A_LATENT, blk * BLK:(blk + 1) * BLK])
    r = _dot(wt_ref[A_QIDX_ROW:A_KIDX_ROW, :], ht_ref[...]).astype(BF16)
    for j in range(IDX_HEADS):
        for blk in range(tm // BLK):
            qir_ref[blk, :, j * BLK:(j + 1) * BLK] = (
                r[j * IDX_DIM:(j + 1) * IDX_DIM, blk * BLK:(blk + 1) * BLK])
    w = _dot(wt_ref[A_KIDX_ROW + IDX_DIM:A_KC_ROW, :], ht_ref[...])
    wto_ref[...] = w[:IDX_HEADS] * (IDX_HEADS ** -0.5)
    t = _dot_nt(h_ref[...], wt_ref[A_KC_ROW:, :])
    ckv = _rms(t[:, :A_LATENT], lg_ref[...])
    ckv_ref[...] = ckv.astype(BF16)
    kidx_ref[...] = t[:, A_LATENT:A_LATENT + IDX_DIM].astype(BF16)
    for b2 in range(tm // SC):
        ckvt_ref[b2] = ckv[b2 * SC:(b2 + 1) * SC, :].T.astype(BF16)


def _a_proj(x2, g, w, lg, wgu2, wd2, tm=512):
    n, d = x2.shape
    steps = n // tm
    gu_rows, dn_rows = wgu2.shape[0] // steps, wd2.shape[0] // steps
    row = lambda i: (i, 0)
    blk3 = lambda i: (i, 0, 0)
    return pl.pallas_call(
        _a_proj_kernel,
        grid=(steps,),
        in_specs=[pl.BlockSpec((tm, d), row),
                  _resident((1, d)),
                  pl.BlockSpec((None,) + w.shape[1:], lambda i: (0, 0, 0),
                               pipeline_mode=pl.Buffered(1)),
                  _resident((1, A_LATENT)),
                  pl.BlockSpec((gu_rows, wgu2.shape[1]), row),
                  pl.BlockSpec((dn_rows, wd2.shape[1]), row)],
        out_specs=(pl.BlockSpec((tm // BLK, A_LATENT, A_QLAT), blk3),
                   pl.BlockSpec((tm // BLK, IDX_DIM, IDX_HEADS * BLK), blk3),
                   pl.BlockSpec((IDX_HEADS, tm), lambda i: (0, i)),
                   pl.BlockSpec((tm, IDX_DIM), row),
                   pl.BlockSpec((tm, A_LATENT), row),
                   pl.BlockSpec((tm // SC, A_LATENT, SC), blk3),
                   pl.BlockSpec((gu_rows, wgu2.shape[1]), row),
                   pl.BlockSpec((dn_rows, wd2.shape[1]), row)),
        out_shape=(jax.ShapeDtypeStruct((n // BLK, A_LATENT, A_QLAT), BF16),
                   jax.ShapeDtypeStruct((n // BLK, IDX_DIM, IDX_HEADS * BLK), BF16),
                   jax.ShapeDtypeStruct((IDX_HEADS, n), F32),
                   jax.ShapeDtypeStruct((n, IDX_DIM), BF16),
                   jax.ShapeDtypeStruct((n, A_LATENT), BF16),
                   jax.ShapeDtypeStruct((n // SC, A_LATENT, SC), BF16),
                   jax.ShapeDtypeStruct(wgu2.shape, BF16),
                   jax.ShapeDtypeStruct(wd2.shape, BF16)),
        scratch_shapes=[pltpu.VMEM((tm, d), BF16), pltpu.VMEM((d, tm), BF16),
                        pltpu.VMEM((A_WT_ROWS, d), BF16)],
        compiler_params=_params(1),
        name="a_proj",
    )(x2, g, w, lg, wgu2, wd2)


def _dsa_kernel(qr_ref, qirn_ref, wtn_ref, kidx_ref, ckv_ref, ckvt_ref, wuvt_ref,
                btd_ref, tri_ref, out_ref,
                score_ref, acc_ref, l_ref, *, k_sel, n_sc):
    i = pl.program_id(1)
    nsc = i // 2 + 1
    kf = float(k_sel)
    krow = lax.broadcasted_iota(I32, (SC, BLK), 0)
    qpos = i * BLK + lax.broadcasted_iota(I32, (SC, BLK), 1)

    def causal(sc):
        return sc * SC + krow <= qpos

    def key_rows(ref, sc, count=1):
        return ref[pl.ds(pl.multiple_of(sc * SC, SC), count * SC), :]

    def colsum(x):
        return jnp.sum(x.reshape(SC // 8, 8, BLK), axis=0)

    w2 = wtn_ref[...] * (IDX_DIM ** -0.5)
    nsc_next = (i + 1) // 2 + 1
    qpos_next = qpos + BLK

    def index_next(sc0, count):
        d = _dot(key_rows(kidx_ref, sc0, count), qirn_ref[0])
        for c in range(count):
            rows = slice(c * SC, (c + 1) * SC)
            s = w2[0:1, :] * jnp.maximum(d[rows, 0:BLK], 0.0)
            for j in range(1, IDX_HEADS):
                s = s + w2[j:j + 1, :] * jnp.maximum(d[rows, j * BLK:(j + 1) * BLK], 0.0)
            score_ref[sc0 + c] = jnp.where((sc0 + c) * SC + krow <= qpos_next, s, NEG_INF)

    def put_threshold(thr, n_ge):
        score_ref[n_sc, 0:8, :] = jnp.broadcast_to(thr, (8, BLK))
        score_ref[n_sc, 8:16, :] = jnp.broadcast_to(n_ge, (8, BLK))

    @pl.when((i + 1) * BLK <= k_sel)
    def _():
        def body(sc, carry):
            score_ref[sc] = jnp.where(causal(sc), 0.0, NEG_INF)
            return carry
        lax.fori_loop(0, nsc, body, 0)
        put_threshold(jnp.zeros((1, BLK), F32), jnp.full((1, BLK), kf, F32))

    def key_to_float(key):
        return lax.bitcast_convert_type(key ^ ((key >> 31) & 0x7FFFFFFF), F32)

    def search(nblk):
        def count_ge(cand):
            cb = jnp.broadcast_to(cand, (8, BLK))
            parts = [jnp.zeros((8, BLK), F32)] * 8
            for t in range(nblk * BLK // 8):
                sc, r = divmod(t * 8, SC)
                ge = score_ref[sc, r:r + 8, :] >= cb
                parts[t % 8] = parts[t % 8] + jnp.where(ge, 1.0, 0.0)
            part = sum(parts[1:], parts[0])
            return jnp.sum(part, axis=0, keepdims=True)

        n0 = count_ge(jnp.zeros((1, BLK), F32))
        take = n0 >= kf
        ans = jnp.where(take, 0, INT_MIN)
        n_ans = jnp.where(take, n0, kf)
        first = jnp.left_shift(1, 30)
        cand_f = jnp.where(take, key_to_float(first), key_to_float(INT_MIN | first))

        def bit_body(it, carry):
            ans, cand_f, n_ans = carry
            cand = ans | jnp.left_shift(1, 30 - it)
            nxt = jnp.left_shift(1, jnp.maximum(29 - it, 0))
            f_take, f_keep = key_to_float(cand | nxt), key_to_float(ans | nxt)
            n_cand = count_ge(cand_f)
            take = n_cand >= kf
            return (jnp.where(take, cand, ans), jnp.where(take, f_take, f_keep),
                    jnp.where(take, n_cand, n_ans))
        ans, _, n_ans = lax.fori_loop(0, 31, bit_body, (ans, cand_f, n_ans))
        return key_to_float(ans), n_ans

    @pl.when((i + 1) * BLK > k_sel)
    def _():
        for nblk in range(k_sel // BLK + 1, 2 * n_sc + 1):
            @pl.when(i + 1 == nblk)
            def _(nblk=nblk):
                put_threshold(*search(nblk))
        thr = score_ref[n_sc, 0:1, :]
        n_ge = score_ref[n_sc, 8:9, :]

        @pl.when(jnp.max(n_ge) > kf)
        def _():
            def gt_body(sc, part):
                return part + colsum(jnp.where(score_ref[sc] > thr, 1.0, 0.0))
            n_gt = jnp.sum(lax.fori_loop(0, nsc, gt_body, jnp.zeros((8, BLK), F32)),
                           axis=0, keepdims=True)
            need = kf - n_gt

            def mask_body(sc, off):
                score = score_ref[sc]
                eq = score == thr
                eqf = jnp.where(eq, 1.0, 0.0)
                before = _dot(tri_ref[...], eqf.astype(BF16)) + off
                score_ref[sc] = jnp.where(eq & (before >= need), NEG_INF, score)
                return off + jnp.sum(eqf, axis=0, keepdims=True)
            lax.fori_loop(0, nsc, mask_body, jnp.zeros((1, BLK), F32))

    c1 = (A_LATENT ** -0.5) * LOG2E
    hcols = N_HEADS * BLK
    thr_a = score_ref[n_sc, 0:1, :]

    def logits_group(sc0, m, count, kinds=None):
        lt = _dot(key_rows(ckv_ref, sc0, count), qr_ref[0])
        madds = [jnp.where(score_ref[sc0 + c] >= thr_a, 0.0, NEG_INF) for c in range(count)]
        new_m = []
        for h in range(N_HEADS):
            hs = slice(h * BLK, (h + 1) * BLK)
            mh = m[:, hs]
            for c in range(count):
                v = lt[c * SC:(c + 1) * SC, hs] * c1 + madds[c]
                if kinds is not None and kinds[2 * c:2 * c + 2] != (None, None):
                    halves = []
                    for half in range(2):
                        part = v[half * BLK:(half + 1) * BLK]
                        kind = kinds[2 * c + half]
                        halves.append(part if kind is None else part + btd_ref[kind, h])
                    v = jnp.concatenate(halves, axis=0)
                l_ref[sc0 + c, h] = v
                mh = jnp.maximum(mh, jnp.max(v, axis=0, keepdims=True))
            new_m.append(mh)
        return jnp.concatenate(new_m, axis=1)

    n_far = nsc - 2
    m = jnp.full((1, hcols), NEG_INF, F32)
    m = lax.fori_loop(0, n_far // 2, lambda t, m: logits_group(2 * t, m, 2), m)
    m = lax.cond((n_far > 0) & (n_far % 2 == 1),
                 lambda m: logits_group(n_far - 1, m, 1), lambda m: m, m)
    even = i % 2 == 0
    m = lax.cond(
        nsc >= 2,
        lambda m: lax.cond(even,
                           lambda m: logits_group(nsc - 2, m, 2, (None, 1, 0, None)),
                           lambda m: logits_group(nsc - 2, m, 2, (None, None, 1, 0)), m),
        lambda m: lax.cond(even,
                           lambda m: logits_group(0, m, 1, (0, None)),
                           lambda m: logits_group(0, m, 1, (1, 0)), m),
        m)

    def pv_group(sc0, s, count):
        index_next(sc0, count)
        vt = jnp.concatenate([ckvt_ref[sc0 + c] for c in range(count)], axis=1)
        new_s = []
        for hp in range(N_HEADS // 2):
            cols = []
            for h in (2 * hp, 2 * hp + 1):
                hs = slice(h * BLK, (h + 1) * BLK)
                sh = s[:, hs]
                ps = []
                for c in range(count):
                    p = jnp.exp2(l_ref[sc0 + c, h] - m[:, hs])
                    sh = sh + colsum(p)
                    ps.append(p.astype(BF16))
                new_s.append(sh)
                cols.append(jnp.concatenate(ps, axis=0))
            pair = jnp.concatenate(cols, axis=1)
            acc_ref[hp] = acc_ref[hp] + _dot(vt, pair)
        return jnp.concatenate(new_s, axis=1)

    acc_ref[...] = jnp.zeros(acc_ref.shape, F32)
    s = lax.fori_loop(0, nsc // 2, lambda t, s: pv_group(2 * t, s, 2),
                      jnp.zeros((8, hcols), F32))
    s = lax.cond(nsc % 2 == 1, lambda s: pv_group(nsc - 1, s, 1), lambda s: s, s)
    inv = 1.0 / jnp.sum(s, axis=0, keepdims=True)

    @pl.when((nsc_next > nsc) & (nsc_next <= n_sc))
    def _():
        index_next(nsc, 1)

    outs = []
    for h in range(N_HEADS):
        hs = slice(h * BLK, (h + 1) * BLK)
        a = acc_ref[h // 2, :, (h % 2) * BLK:(h % 2 + 1) * BLK]
        o = (a * inv[:, hs]).astype(BF16)
        outs.append(_dot(wuvt_ref[h], o))
    out_ref[...] = jnp.concatenate(outs, axis=0).T.astype(BF16)


def _dsa_attn(qr, qir, wt, kidx, ckv, ckvt, wuvt, btd, batch, seq):
    nblk = seq // BLK
    n_sc = seq // SC
    k_sel = min(TOPK_MAX, seq // 4)
    hcols = N_HEADS * BLK
    tri = jnp.asarray(np.tril(np.ones((SC, SC), np.float32), -1), BF16)
    return pl.pallas_call(
        functools.partial(_dsa_kernel, k_sel=k_sel, n_sc=n_sc),
        grid=(batch, nblk),
        in_specs=[pl.BlockSpec((1, A_LATENT, hcols), lambda b, i: (b * nblk + i, 0, 0)),
                  pl.BlockSpec((1, IDX_DIM, IDX_HEADS * BLK),
                               lambda b, i: (b * nblk + jnp.minimum(i + 1, nblk - 1), 0, 0)),
                  pl.BlockSpec((IDX_HEADS, BLK),
                               lambda b, i: (0, b * nblk + jnp.minimum(i + 1, nblk - 1))),
                  pl.BlockSpec((seq, IDX_DIM), lambda b, i: (b, 0)),
                  pl.BlockSpec((seq, A_LATENT), lambda b, i: (b, 0)),
                  pl.BlockSpec((n_sc, A_LATENT, SC), lambda b, i: (b, 0, 0)),
                  _resident((N_HEADS, A_VHEAD, A_LATENT)),
                  _resident((3, N_HEADS, BLK, BLK)),
                  _resident((SC, SC))],
        out_specs=pl.BlockSpec((BLK, N_HEADS * A_VHEAD), lambda b, i: (b * nblk + i, 0)),
        out_shape=jax.ShapeDtypeStruct((batch * seq, N_HEADS * A_VHEAD), BF16),
        scratch_shapes=[
            pltpu.VMEM((n_sc + 1, SC, BLK), F32),
            pltpu.VMEM((N_HEADS // 2, A_LATENT, 2 * BLK), F32),
            pltpu.VMEM((n_sc, N_HEADS, SC, BLK), F32),
        ],
        compiler_params=_params(2),
        name="dsa_attn",
    )(qr, qir, wt, kidx, ckv, ckvt, wuvt, btd, tri)


B_Q = N_HEADS * B_HEAD_DIM
B_KV = B_KV_HEADS * B_HEAD_DIM


def _b_proj_kernel(x_ref, g_ref, wt_ref, bcol_ref, wk_ref, bk_ref,
                   qr_ref, kk_ref, vt_ref, h_ref, ht_ref):
    tm = x_ref.shape[0]
    hf = _rms(x_ref[...], g_ref[...])
    h_ref[...] = hf.astype(BF16)
    ht_ref[...] = hf.T.astype(BF16)
    heads_per_dot = PROJ_ROWS // B_HEAD_DIM
    for c in range(B_Q // PROJ_ROWS):
        rows = slice(c * PROJ_ROWS, (c + 1) * PROJ_ROWS)
        r = (_dot(wt_ref[rows, :], ht_ref[...]) + bcol_ref[rows, :]).astype(BF16)
        for hh in range(heads_per_dot):
            h = c * heads_per_dot + hh
            for blk in range(tm // BLK):
                qr_ref[blk, :, h * BLK:(h + 1) * BLK] = (
                    r[hh * B_HEAD_DIM:(hh + 1) * B_HEAD_DIM, blk * BLK:(blk + 1) * BLK])
    vt_ref[...] = (_dot(wt_ref[B_Q:, :], ht_ref[...]) + bcol_ref[B_Q:, :]).astype(BF16)
    t = _dot(h_ref[...], wk_ref[...]) + bk_ref[...]
    for kv in range(B_KV_HEADS):
        kk_ref[kv] = t[:, kv * B_HEAD_DIM:(kv + 1) * B_HEAD_DIM].astype(BF16)


def _b_proj(x2, g, wt, bcol, wk, bk, tm=512):
    n, d = x2.shape
    return pl.pallas_call(
        _b_proj_kernel,
        grid=(n // tm,),
        in_specs=[pl.BlockSpec((tm, d), lambda i: (i, 0)),
                  _resident((1, d)),
                  _resident((B_Q + B_KV, d)),
                  _resident((B_Q + B_KV, 1)),
                  _resident((d, B_KV)),
                  _resident((1, B_KV))],
        out_specs=(pl.BlockSpec((tm // BLK, B_HEAD_DIM, N_HEADS * BLK), lambda i: (i, 0, 0)),
                   pl.BlockSpec((B_KV_HEADS, tm, B_HEAD_DIM), lambda i: (0, i, 0)),
                   pl.BlockSpec((B_KV, tm), lambda i: (0, i))),
        out_shape=(jax.ShapeDtypeStruct((n // BLK, B_HEAD_DIM, N_HEADS * BLK), BF16),
                   jax.ShapeDtypeStruct((B_KV_HEADS, n, B_HEAD_DIM), BF16),
                   jax.ShapeDtypeStruct((B_KV, n), BF16)),
        scratch_shapes=[pltpu.VMEM((tm, d), BF16), pltpu.VMEM((d, tm), BF16)],
        compiler_params=_params(1),
        name="b_proj",
    )(x2, g, wt, bcol, wk, bk)


def _swa_kernel(sink_ref, qr_ref, kp_ref, kc_ref, vp_ref, vc_ref, bts_ref, out_ref):
    n = pl.program_id(1)
    c1 = (B_HEAD_DIM ** -0.5) * LOG2E
    group = N_HEADS // B_KV_HEADS
    blocks = (
        (qr_ref.at[0], kp_ref, kc_ref.at[:, 0:BLK, :], vp_ref, vc_ref.at[:, 0:BLK],
         jnp.where(n >= 1, 1, 2)),
        (qr_ref.at[1], kc_ref.at[:, 0:BLK, :], kc_ref.at[:, BLK:2 * BLK, :],
         vc_ref.at[:, 0:BLK], vc_ref.at[:, BLK:2 * BLK], 1))
    tasks = [(qb, kv, g2) for qb in range(2) for kv in range(B_KV_HEADS)
             for g2 in range(group // SWA_TASK_HEADS)]

    def scores(task):
        qb, kv, g2 = task
        q_ref, kp, kc = blocks[qb][0:3]
        h0 = kv * group + SWA_TASK_HEADS * g2
        q = q_ref[:, h0 * BLK:(h0 + SWA_TASK_HEADS) * BLK]
        return _dot(kp[kv], q), _dot(kc[kv], q)

    def attend(task, lp, lc):
        qb, kv, g2 = task
        vp, vc, prev_kind = blocks[qb][3:6]
        vrows = slice(kv * B_HEAD_DIM, (kv + 1) * B_HEAD_DIM)
        pps, pcs, invs = [], [], []
        for j in range(SWA_TASK_HEADS):
            h = kv * group + SWA_TASK_HEADS * g2 + j
            lanes = slice(j * BLK, (j + 1) * BLK)
            sink = sink_ref[:, h * BLK:(h + 1) * BLK] * LOG2E
            a = lp[:, lanes] * c1 + bts_ref[prev_kind, h]
            b = lc[:, lanes] * c1 + bts_ref[0, h]
            m = jnp.maximum(jnp.max(jnp.maximum(a, b), axis=0, keepdims=True), sink)
            pp = jnp.exp2(a - m)
            pc = jnp.exp2(b - m)
            den = jnp.sum(pp + pc, axis=0, keepdims=True) + jnp.exp2(sink - m)
            invs.append(1.0 / den)
            pps.append(pp.astype(BF16))
            pcs.append(pc.astype(BF16))
        ot = (_dot(vp[vrows, :], jnp.concatenate(pps, axis=1))
              + _dot(vc[vrows, :], jnp.concatenate(pcs, axis=1)))
        return [ot[:, j * BLK:(j + 1) * BLK] * invs[j] for j in range(SWA_TASK_HEADS)]

    outs = [[], []]
    ahead = SWA_AHEAD
    queue = [scores(task) for task in tasks[:ahead]]
    for t, task in enumerate(tasks):
        if t + ahead < len(tasks):
            queue.append(scores(tasks[t + ahead]))
        outs[task[0]] += attend(task, *queue.pop(0))
    for qb in range(2):
        out_ref[qb * BLK:(qb + 1) * BLK, :] = (
            jnp.concatenate(outs[qb], axis=0).T.astype(BF16))


def _swa_attn(qr, kk, vt, sinks_row, bts, batch, seq):
    nb2 = seq // (2 * BLK)
    cur = lambda b, n: b * nb2 + n
    prev = lambda b, n: 2 * (b * nb2 + n) - jnp.minimum(n, 1)
    return pl.pallas_call(
        _swa_kernel,
        grid=(batch, nb2),
        in_specs=[_resident((1, N_HEADS * BLK)),
                  pl.BlockSpec((2, B_HEAD_DIM, N_HEADS * BLK), lambda b, n: (cur(b, n), 0, 0)),
                  pl.BlockSpec((B_KV_HEADS, BLK, B_HEAD_DIM), lambda b, n: (0, prev(b, n), 0)),
                  pl.BlockSpec((B_KV_HEADS, 2 * BLK, B_HEAD_DIM), lambda b, n: (0, cur(b, n), 0)),
                  pl.BlockSpec((B_KV, BLK), lambda b, n: (0, prev(b, n))),
                  pl.BlockSpec((B_KV, 2 * BLK), lambda b, n: (0, cur(b, n))),
                  _resident((3, N_HEADS, BLK, BLK))],
        out_specs=pl.BlockSpec((2 * BLK, B_Q), lambda b, n: (cur(b, n), 0)),
        out_shape=jax.ShapeDtypeStruct((batch * seq, B_Q), BF16),
        compiler_params=_params(2),
        name="swa_attn",
    )(sinks_row, qr, kk, kk, vt, vt, bts)


FF_CHUNK = 256


def _ffn_kernel(x_ref, o_ref, wo_ref, bo_ref, g_ref, wgu_ref, wd_ref, fg_ref,
                out_ref, h_ref, act_ref, *, d_ff, final):
    x1 = x_ref[...] + _dot(o_ref[...], wo_ref[...]) + bo_ref[...]
    out_ref[...] = x1
    h_ref[...] = _rms(x1, g_ref[...]).astype(BF16)
    for c in range(d_ff // FF_CHUNK):
        gate = _dot(h_ref[...], wgu_ref[:, c * FF_CHUNK:(c + 1) * FF_CHUNK])
        up = _dot(h_ref[...], wgu_ref[:, d_ff + c * FF_CHUNK:d_ff + (c + 1) * FF_CHUNK])
        act_ref[:, c * FF_CHUNK:(c + 1) * FF_CHUNK] = (
            gate * jax.nn.sigmoid(gate) * up).astype(BF16)
    y = out_ref[...] + _dot(act_ref[...], wd_ref[...])
    if final:
        y = _rms(y, fg_ref[...])
    out_ref[...] = y


def _ffn(x2, o, wo, bo, g, wgu_all, wd_all, layer, fg, final, tm=1024):
    n, d = x2.shape
    d_ff = wd_all.shape[1]
    row = lambda i: (i, 0)
    pick = lambda i: (layer, 0, 0)
    return pl.pallas_call(
        functools.partial(_ffn_kernel, d_ff=d_ff, final=final),
        grid=(n // tm,),
        in_specs=[pl.BlockSpec((tm, d), row),
                  pl.BlockSpec((tm, o.shape[1]), row),
                  _resident(wo.shape),
                  _resident((1, d)),
                  _resident((1, d)),
                  pl.BlockSpec((None,) + wgu_all.shape[1:], pick, pipeline_mode=pl.Buffered(1)),
                  pl.BlockSpec((None,) + wd_all.shape[1:], pick, pipeline_mode=pl.Buffered(1)),
                  _resident((1, d))],
        out_specs=pl.BlockSpec((tm, d), row),
        out_shape=jax.ShapeDtypeStruct((n, d), F32),
        scratch_shapes=[pltpu.VMEM((tm, d), BF16),
                        pltpu.VMEM((tm, d_ff), BF16)],
        compiler_params=_params(1),
        name="ffn",
    )(x2, o, wo, bo, g, wgu_all, wd_all, fg)


def kernel(x, rel_bias, attn_norm, ffn_norm, final_norm, a_w_in, a_latent_norm, a_w_uv, a_w_out, b_w_in, b_b_in, b_sinks, b_w_out, b_b_out, ffn_w_gate_up, ffn_w_down):
    batch, seq, d = x.shape
    assert attn_norm.shape[0] == 2 and a_w_in.shape[0] == 1 and b_w_in.shape[0] == 1
    assert seq % SC == 0 and (batch * seq) % 1024 == 0 and d % BLK == 0
    assert a_w_in.shape[2] == A_IN and b_w_in.shape[2] == B_Q + 2 * B_KV
    assert rel_bias.shape == (REL_BUCKETS, N_HEADS)
    x2 = x.reshape(batch * seq, d)
    bts, btd = _bias_tiles(rel_bias)
    fg = final_norm.reshape(1, d)

    gu_shape, dn_shape = ffn_w_gate_up.shape, ffn_w_down.shape
    qr, qir, wto, kidx, ckv, ckvt, wgu_bf, wd_bf = _a_proj(
        x2, attn_norm[0].reshape(1, d), jnp.swapaxes(a_w_in, 1, 2),
        a_latent_norm[0].reshape(1, A_LATENT),
        ffn_w_gate_up.reshape(-1, gu_shape[-1]), ffn_w_down.reshape(-1, dn_shape[-1]))
    wgu_all, wd_all = wgu_bf.reshape(gu_shape), wd_bf.reshape(dn_shape)
    wuvt = jnp.swapaxes(a_w_uv[0], 1, 2).astype(BF16)
    o = _dsa_attn(qr, qir, wto, kidx, ckv, ckvt, wuvt, btd, batch, seq)
    x2 = _ffn(x2, o, a_w_out[0].astype(BF16), jnp.zeros((1, d), F32),
              ffn_norm[0].reshape(1, d), wgu_all, wd_all, 0, fg, final=False)

    wb, bb = b_w_in[0].astype(BF16), b_b_in[0]
    wqv = jnp.concatenate([wb[:, :B_Q], wb[:, B_Q + B_KV:]], axis=1).T
    bqv = jnp.concatenate([bb[:B_Q], bb[B_Q + B_KV:]]).reshape(-1, 1)
    qr, kk, vt = _b_proj(
        x2, attn_norm[1].reshape(1, d), wqv, bqv, wb[:, B_Q:B_Q + B_KV],
        bb[B_Q:B_Q + B_KV].reshape(1, B_KV))
    sinks_row = jnp.repeat(b_sinks[0], BLK).reshape(1, N_HEADS * BLK)
    o = _swa_attn(qr, kk, vt, sinks_row, bts, batch, seq)
    x2 = _ffn(x2, o, b_w_out[0].astype(BF16), b_b_out[0].reshape(1, d),
              ffn_norm[1].reshape(1, d), wgu_all, wd_all, 1, fg, final=True)
    return x2.reshape(batch, seq, d)
```

```python
import functools
import math

import numpy as np
import jax
import jax.numpy as jnp
from jax import lax
from jax.experimental import pallas as pl
from jax.experimental.pallas import tpu as pltpu

F32 = jnp.float32
BF16 = jnp.bfloat16
I32 = jnp.int32

EPS = 1e-6
N_HEADS = 16
A_LATENT = 128
A_VHEAD = 64
IDX_HEADS = 8
IDX_DIM = 64
TOPK_MAX = 256
B_KV_HEADS = 2
B_HEAD_DIM = 64
REL_BUCKETS = 32
REL_MAX_DIST = 128

BLK = 128
SC = 256
PROJ_ROWS = 512
SEARCH_SPLIT = 22
SWA_TASK_HEADS = 1
SWA_AHEAD = 4
INT_MIN = -(2 ** 31)
NEG_INF = float("-inf")
LOG2E = math.log2(math.e)
VMEM_LIMIT = 56 * 1024 * 1024


def _rms(xf, g):
    ms = jnp.mean(xf * xf, axis=-1, keepdims=True)
    return xf * lax.rsqrt(ms + EPS) * g


def _dot(a, b):
    return jnp.dot(a, b, preferred_element_type=F32)


def _dot_nt(a, b):
    return lax.dot_general(a, b, (((1,), (1,)), ((), ())), preferred_element_type=F32)


def _params(n_axes):
    return pltpu.CompilerParams(
        dimension_semantics=("arbitrary",) * n_axes, vmem_limit_bytes=VMEM_LIMIT)


def _resident(shape):
    zeros = (0,) * len(shape)
    return pl.BlockSpec(shape, lambda *_: zeros, pipeline_mode=pl.Buffered(1))


def _bucket(rel):
    n = np.maximum(rel, 0)
    max_exact = REL_BUCKETS // 2
    nf = np.maximum(n, max_exact).astype(np.float64)
    large = max_exact + (np.log(nf / max_exact) / math.log(REL_MAX_DIST / max_exact)
                         * (REL_BUCKETS - max_exact)).astype(np.int32)
    large = np.minimum(large, REL_BUCKETS - 1)
    return np.where(n < max_exact, n, large).astype(np.int32)


def _bucket_tables():
    k = np.arange(BLK)[:, None]
    q = np.arange(BLK)[None, :]
    return _bucket(np.stack([q - k, BLK + q - k]))


def _bias_kernel(rb_ref, bidx_ref, bts_ref, btd_ref):
    k = lax.broadcasted_iota(I32, (BLK, BLK), 0)
    q = lax.broadcasted_iota(I32, (BLK, BLK), 1)
    window = (k <= q, k > q)
    for h in range(N_HEADS):
        for kind in range(2):
            bidx = bidx_ref[kind]

            t = jnp.zeros((BLK, BLK), F32)
            for b in range(REL_BUCKETS):
                t = jnp.where(bidx == b, rb_ref[b, h], t)
            bts_ref[kind, h] = jnp.where(window[kind], t * LOG2E, NEG_INF)
            btd_ref[kind, h] = (t - rb_ref[REL_BUCKETS - 1, h]) * LOG2E
        bts_ref[2, h] = jnp.full((BLK, BLK), NEG_INF, F32)
        btd_ref[2, h] = jnp.zeros((BLK, BLK), F32)


def _bias_tiles(rel_bias):
    return pl.pallas_call(
        _bias_kernel,
        out_shape=(jax.ShapeDtypeStruct((3, N_HEADS, BLK, BLK), F32),
                   jax.ShapeDtypeStruct((3, N_HEADS, BLK, BLK), F32)),
        in_specs=[pl.BlockSpec(memory_space=pltpu.SMEM),
                  pl.BlockSpec(memory_space=pltpu.VMEM)],
        out_specs=(pl.BlockSpec(memory_space=pltpu.VMEM),
                   pl.BlockSpec(memory_space=pltpu.VMEM)),
        name="bias_tiles",
    )(rel_bias, jnp.asarray(_bucket_tables()))


A_QLAT = N_HEADS * A_LATENT
A_QIDX = IDX_HEADS * IDX_DIM
A_QIDX_ROW = A_QLAT + A_LATENT
A_KIDX_ROW = A_QIDX_ROW + A_QIDX
A_IN = A_KIDX_ROW + IDX_DIM + IDX_HEADS
A_KC_ROW = A_KIDX_ROW + IDX_DIM + 16
A_WT_ROWS = A_KC_ROW + 2 * A_LATENT


def _a_proj_kernel(x_ref, g_ref, w_ref, lg_ref, wgu_ref, wd_ref,
                   qr_ref, qir_ref, wto_ref, kidx_ref, ckv_ref, ckvt_ref,
                   wgu_bf_ref, wd_bf_ref, h_ref, ht_ref, wt_ref):
    tm = x_ref.shape[0]

    @pl.when(pl.program_id(0) == 0)
    def _():
        wt_ref[0:A_IN, :] = w_ref[...].astype(BF16)
        wt_ref[A_IN:A_KC_ROW, :] = jnp.zeros((A_KC_ROW - A_IN, wt_ref.shape[1]), BF16)
        wt_ref[A_KC_ROW:A_KC_ROW + A_LATENT, :] = w_ref[A_QLAT:A_QIDX_ROW, :].astype(BF16)
        wt_ref[A_KC_ROW + A_LATENT:A_KC_ROW + A_LATENT + IDX_DIM, :] = (
            w_ref[A_KIDX_ROW:A_KIDX_ROW + IDX_DIM, :].astype(BF16))
        wt_ref[A_KC_ROW + A_LATENT + IDX_DIM:, :] = jnp.zeros(
            (2 * A_LATENT - A_LATENT - IDX_DIM, wt_ref.shape[1]), BF16)

    wgu_bf_ref[...] = wgu_ref[...].astype(BF16)
    wd_bf_ref[...] = wd_ref[...].astype(BF16)
    hf = _rms(x_ref[...], g_ref[...])
    h_ref[...] = hf.astype(BF16)
    ht_ref[...] = hf.T.astype(BF16)
    heads_per_dot = PROJ_ROWS // A_LATENT
    for c in range(A_QLAT // PROJ_ROWS):
        r = _dot(wt_ref[c * PROJ_ROWS:(c + 1) * PROJ_ROWS, :], ht_ref[...]).astype(BF16)
        for hh in range(heads_per_dot):
            h = heads_per_dot * c + hh
            for blk in range(tm // BLK):
                qr_ref[blk, :, h * BLK:(h + 1) * BLK] = (
                    r[hh * A_LATENT:(hh + 1) * A_LATENT, blk * BLK:(blk + 1) * BLK])
    r = _dot(wt_ref[A_QIDX_ROW:A_KIDX_ROW, :], ht_ref[...]).astype(BF16)
    for j in range(IDX_HEADS):
        for blk in range(tm // BLK):
            qir_ref[blk, :, j * BLK:(j + 1) * BLK] = (
                r[j * IDX_DIM:(j + 1) * IDX_DIM, blk * BLK:(blk + 1) * BLK])
    w = _dot(wt_ref[A_KIDX_ROW + IDX_DIM:A_KC_ROW, :], ht_ref[...])
    wto_ref[...] = w[:IDX_HEADS] * (IDX_HEADS ** -0.5)
    t = _dot_nt(h_ref[...], wt_ref[A_KC_ROW:, :])
    ckv = _rms(t[:, :A_LATENT], lg_ref[...])
    ckv_ref[...] = ckv.astype(BF16)
    kidx_ref[...] = t[:, A_LATENT:A_LATENT + IDX_DIM].astype(BF16)
    for b2 in range(tm // SC):
        ckvt_ref[b2] = ckv[b2 * SC:(b2 + 1) * SC, :].T.astype(BF16)


def _a_proj(x2, g, w, lg, wgu2, wd2, tm=512):
    n, d = x2.shape
    steps = n // tm
    gu_rows, dn_rows = wgu2.shape[0] // steps, wd2.shape[0] // steps
    row = lambda i: (i, 0)
    blk3 = lambda i: (i, 0, 0)
    return pl.pallas_call(
        _a_proj_kernel,
        grid=(steps,),
        in_specs=[pl.BlockSpec((tm, d), row),
                  _resident((1, d)),
                  pl.BlockSpec((None,) + w.shape[1:], lambda i: (0, 0, 0),
                               pipeline_mode=pl.Buffered(1)),
                  _resident((1, A_LATENT)),
                  pl.BlockSpec((gu_rows, wgu2.shape[1]), row),
                  pl.BlockSpec((dn_rows, wd2.shape[1]), row)],
        out_specs=(pl.BlockSpec((tm // BLK, A_LATENT, A_QLAT), blk3),
                   pl.BlockSpec((tm // BLK, IDX_DIM, IDX_HEADS * BLK), blk3),
                   pl.BlockSpec((IDX_HEADS, tm), lambda i: (0, i)),
                   pl.BlockSpec((tm, IDX_DIM), row),
                   pl.BlockSpec((tm, A_LATENT), row),
                   pl.BlockSpec((tm // SC, A_LATENT, SC), blk3),
                   pl.BlockSpec((gu_rows, wgu2.shape[1]), row),
                   pl.BlockSpec((dn_rows, wd2.shape[1]), row)),
        out_shape=(jax.ShapeDtypeStruct((n // BLK, A_LATENT, A_QLAT), BF16),
                   jax.ShapeDtypeStruct((n // BLK, IDX_DIM, IDX_HEADS * BLK), BF16),
                   jax.ShapeDtypeStruct((IDX_HEADS, n), F32),
                   jax.ShapeDtypeStruct((n, IDX_DIM), BF16),
                   jax.ShapeDtypeStruct((n, A_LATENT), BF16),
                   jax.ShapeDtypeStruct((n // SC, A_LATENT, SC), BF16),
                   jax.ShapeDtypeStruct(wgu2.shape, BF16),
                   jax.ShapeDtypeStruct(wd2.shape, BF16)),
        scratch_shapes=[pltpu.VMEM((tm, d), BF16), pltpu.VMEM((d, tm), BF16),
                        pltpu.VMEM((A_WT_ROWS, d), BF16)],
        compiler_params=_params(1),
        name="a_proj",
    )(x2, g, w, lg, wgu2, wd2)


def _dsa_kernel(qr_ref, qirn_ref, wtn_ref, kidx_ref, ckv_ref, ckvt_ref, wuvt_ref,
                btd_ref, tri_ref, out_ref,
                score_ref, acc_ref, l_ref, *, k_sel, n_sc):
    i = pl.program_id(1)
    nsc = i // 2 + 1
    kf = float(k_sel)
    krow = lax.broadcasted_iota(I32, (SC, BLK), 0)
    qpos = i * BLK + lax.broadcasted_iota(I32, (SC, BLK), 1)

    def causal(sc):
        return sc * SC + krow <= qpos

    def key_rows(ref, sc, count=1):
        return ref[pl.ds(pl.multiple_of(sc * SC, SC), count * SC), :]

    def colsum(x):
        return jnp.sum(x.reshape(SC // 8, 8, BLK), axis=0)

    w2 = wtn_ref[...] * (IDX_DIM ** -0.5)
    nsc_next = (i + 1) // 2 + 1
    qpos_next = qpos + BLK

    def index_next(sc0, count):
        d = _dot(key_rows(kidx_ref, sc0, count), qirn_ref[0])
        for c in range(count):
            rows = slice(c * SC, (c + 1) * SC)
            s = w2[0:1, :] * jnp.maximum(d[rows, 0:BLK], 0.0)
            for j in range(1, IDX_HEADS):
                s = s + w2[j:j + 1, :] * jnp.maximum(d[rows, j * BLK:(j + 1) * BLK], 0.0)
            score_ref[sc0 + c] = jnp.where((sc0 + c) * SC + krow <= qpos_next, s, NEG_INF)

    def put_threshold(thr, n_ge):
        score_ref[n_sc, 0:8, :] = jnp.broadcast_to(thr, (8, BLK))
        score_ref[n_sc, 8:16, :] = jnp.broadcast_to(n_ge, (8, BLK))

    @pl.when((i + 1) * BLK <= k_sel)
    def _():
        def body(sc, carry):
            score_ref[sc] = jnp.where(causal(sc), 0.0, NEG_INF)
            return carry
        lax.fori_loop(0, nsc, body, 0)
        put_threshold(jnp.zeros((1, BLK), F32), jnp.full((1, BLK), kf, F32))

    def key_to_float(key):
        return lax.bitcast_convert_type(key ^ ((key >> 31) & 0x7FFFFFFF), F32)

    def search(nblk):
        def count_ge(cand):
            cb = jnp.broadcast_to(cand, (8, BLK))
            parts = [jnp.zeros((8, BLK), F32)] * 8
            for t in range(nblk * BLK // 8):
                sc, r = divmod(t * 8, SC)
                ge = score_ref[sc, r:r + 8, :] >= cb
                parts[t % 8] = parts[t % 8] + jnp.where(ge, 1.0, 0.0)
            part = sum(parts[1:], parts[0])
            return jnp.sum(part, axis=0, keepdims=True)

        n0 = count_ge(jnp.zeros((1, BLK), F32))
        take = n0 >= kf
        ans = jnp.where(take, 0, INT_MIN)
        n_ans = jnp.where(take, n0, kf)
        first = jnp.left_shift(1, 30)
        cand_f = jnp.where(take, key_to_float(first), key_to_float(INT_MIN | first))

        def bit_body(it, carry):
            ans, cand_f, n_ans = carry
            cand = ans | jnp.left_shift(1, 30 - it)
            nxt = jnp.left_shift(1, jnp.maximum(29 - it, 0))
            f_take, f_keep = key_to_float(cand | nxt), key_to_float(ans | nxt)
            n_cand = count_ge(cand_f)
            take = n_cand >= kf
            return (jnp.where(take, cand, ans), jnp.where(take, f_take, f_keep),
                    jnp.where(take, n_cand, n_ans))
        carry = lax.fori_loop(0, SEARCH_SPLIT, bit_body, (ans, cand_f, n_ans))
        ans, _, n_ans = lax.cond(
            jnp.max(carry[2]) > kf,
            lambda c: lax.fori_loop(SEARCH_SPLIT, 31, bit_body, c), lambda c: c, carry)
        return key_to_float(ans), n_ans

    @pl.when((i + 1) * BLK > k_sel)
    def _():
        for nblk in range(k_sel // BLK + 1, 2 * n_sc + 1):
            @pl.when(i + 1 == nblk)
            def _(nblk=nblk):
                put_threshold(*search(nblk))
        thr = score_ref[n_sc, 0:1, :]
        n_ge = score_ref[n_sc, 8:9, :]

        @pl.when(jnp.max(n_ge) > kf)
        def _():
            def gt_body(sc, part):
                return part + colsum(jnp.where(score_ref[sc] > thr, 1.0, 0.0))
            n_gt = jnp.sum(lax.fori_loop(0, nsc, gt_body, jnp.zeros((8, BLK), F32)),
                           axis=0, keepdims=True)
            need = kf - n_gt

            def mask_body(sc, off):
                score = score_ref[sc]
                eq = score == thr
                eqf = jnp.where(eq, 1.0, 0.0)
                before = _dot(tri_ref[...], eqf.astype(BF16)) + off
                score_ref[sc] = jnp.where(eq & (before >= need), NEG_INF, score)
                return off + jnp.sum(eqf, axis=0, keepdims=True)
            lax.fori_loop(0, nsc, mask_body, jnp.zeros((1, BLK), F32))

    c1 = (A_LATENT ** -0.5) * LOG2E
    hcols = N_HEADS * BLK
    thr_a = score_ref[n_sc, 0:1, :]

    def logits_group(sc0, m, count, kinds=None):
        lt = _dot(key_rows(ckv_ref, sc0, count), qr_ref[0])
        madds = [jnp.where(score_ref[sc0 + c] >= thr_a, 0.0, NEG_INF) for c in range(count)]
        new_m = []
        for h in range(N_HEADS):
            hs = slice(h * BLK, (h + 1) * BLK)
            mh = m[:, hs]
            for c in range(count):
                v = lt[c * SC:(c + 1) * SC, hs] * c1 + madds[c]
                if kinds is not None and kinds[2 * c:2 * c + 2] != (None, None):
                    halves = []
                    for half in range(2):
                        part = v[half * BLK:(half + 1) * BLK]
                        kind = kinds[2 * c + half]
                        halves.append(part if kind is None else part + btd_ref[kind, h])
                    v = jnp.concatenate(halves, axis=0)
                l_ref[sc0 + c, h] = v
                mh = jnp.maximum(mh, jnp.max(v, axis=0, keepdims=True))
            new_m.append(mh)
        return jnp.concatenate(new_m, axis=1)

    n_far = nsc - 2
    m = jnp.full((1, hcols), NEG_INF, F32)
    m = lax.fori_loop(0, n_far // 2, lambda t, m: logits_group(2 * t, m, 2), m)
    m = lax.cond((n_far > 0) & (n_far % 2 == 1),
                 lambda m: logits_group(n_far - 1, m, 1), lambda m: m, m)
    even = i % 2 == 0
    m = lax.cond(
        nsc >= 2,
        lambda m: lax.cond(even,
                           lambda m: logits_group(nsc - 2, m, 2, (None, 1, 0, None)),
                           lambda m: logits_group(nsc - 2, m, 2, (None, None, 1, 0)), m),
        lambda m: lax.cond(even,
                           lambda m: logits_group(0, m, 1, (0, None)),
                           lambda m: logits_group(0, m, 1, (1, 0)), m),
        m)

    def pv_group(sc0, s, count):
        index_next(sc0, count)
        vt = jnp.concatenate([ckvt_ref[sc0 + c] for c in range(count)], axis=1)
        new_s = []
        for hp in range(N_HEADS // 2):
            cols = []
            for h in (2 * hp, 2 * hp + 1):
                hs = slice(h * BLK, (h + 1) * BLK)
                sh = s[:, hs]
                ps = []
                for c in range(count):
                    p = jnp.exp2(l_ref[sc0 + c, h] - m[:, hs])
                    sh = sh + colsum(p)
                    ps.append(p.astype(BF16))
                new_s.append(sh)
                cols.append(jnp.concatenate(ps, axis=0))
            pair = jnp.concatenate(cols, axis=1)
            acc_ref[hp] = acc_ref[hp] + _dot(vt, pair)
        return jnp.concatenate(new_s, axis=1)

    acc_ref[...] = jnp.zeros(acc_ref.shape, F32)
    s = lax.fori_loop(0, nsc // 2, lambda t, s: pv_group(2 * t, s, 2),
                      jnp.zeros((8, hcols), F32))
    s = lax.cond(nsc % 2 == 1, lambda s: pv_group(nsc - 1, s, 1), lambda s: s, s)
    inv = 1.0 / jnp.sum(s, axis=0, keepdims=True)

    @pl.when((nsc_next > nsc) & (nsc_next <= n_sc))
    def _():
        index_next(nsc, 1)

    outs = []
    for h in range(N_HEADS):
        hs = slice(h * BLK, (h + 1) * BLK)
        a = acc_ref[h // 2, :, (h % 2) * BLK:(h % 2 + 1) * BLK]
        o = (a * inv[:, hs]).astype(BF16)
        outs.append(_dot(wuvt_ref[h], o))
    out_ref[...] = jnp.concatenate(outs, axis=0).T.astype(BF16)


def _dsa_attn(qr, qir, wt, kidx, ckv, ckvt, wuvt, btd, batch, seq):
    nblk = seq // BLK
    n_sc = seq // SC
    k_sel = min(TOPK_MAX, seq // 4)
    hcols = N_HEADS * BLK
    tri = jnp.asarray(np.tril(np.ones((SC, SC), np.float32), -1), BF16)
    return pl.pallas_call(
        functools.partial(_dsa_kernel, k_sel=k_sel, n_sc=n_sc),
        grid=(batch, nblk),
        in_specs=[pl.BlockSpec((1, A_LATENT, hcols), lambda b, i: (b * nblk + i, 0, 0)),
                  pl.BlockSpec((1, IDX_DIM, IDX_HEADS * BLK),
                               lambda b, i: (b * nblk + jnp.minimum(i + 1, nblk - 1), 0, 0)),
                  pl.BlockSpec((IDX_HEADS, BLK),
                               lambda b, i: (0, b * nblk + jnp.minimum(i + 1, nblk - 1))),
                  pl.BlockSpec((seq, IDX_DIM), lambda b, i: (b, 0)),
                  pl.BlockSpec((seq, A_LATENT), lambda b, i: (b, 0)),
                  pl.BlockSpec((n_sc, A_LATENT, SC), lambda b, i: (b, 0, 0)),
                  _resident((N_HEADS, A_VHEAD, A_LATENT)),
                  _resident((3, N_HEADS, BLK, BLK)),
                  _resident((SC, SC))],
        out_specs=pl.BlockSpec((BLK, N_HEADS * A_VHEAD), lambda b, i: (b * nblk + i, 0)),
        out_shape=jax.ShapeDtypeStruct((batch * seq, N_HEADS * A_VHEAD), BF16),
        scratch_shapes=[
            pltpu.VMEM((n_sc + 1, SC, BLK), F32),
            pltpu.VMEM((N_HEADS // 2, A_LATENT, 2 * BLK), F32),
            pltpu.VMEM((n_sc, N_HEADS, SC, BLK), F32),
        ],
        compiler_params=_params(2),
        name="dsa_attn",
    )(qr, qir, wt, kidx, ckv, ckvt, wuvt, btd, tri)


B_Q = N_HEADS * B_HEAD_DIM
B_KV = B_KV_HEADS * B_HEAD_DIM


def _b_proj_kernel(x_ref, g_ref, wt_ref, bcol_ref, wk_ref, bk_ref,
                   qr_ref, kk_ref, vt_ref, h_ref, ht_ref):
    tm = x_ref.shape[0]
    hf = _rms(x_ref[...], g_ref[...])
    h_ref[...] = hf.astype(BF16)
    ht_ref[...] = hf.T.astype(BF16)
    heads_per_dot = PROJ_ROWS // B_HEAD_DIM
    for c in range(B_Q // PROJ_ROWS):
        rows = slice(c * PROJ_ROWS, (c + 1) * PROJ_ROWS)
        r = (_dot(wt_ref[rows, :], ht_ref[...]) + bcol_ref[rows, :]).astype(BF16)
        for hh in range(heads_per_dot):
            h = c * heads_per_dot + hh
            for blk in range(tm // BLK):
                qr_ref[blk, :, h * BLK:(h + 1) * BLK] = (
                    r[hh * B_HEAD_DIM:(hh + 1) * B_HEAD_DIM, blk * BLK:(blk + 1) * BLK])
    vt_ref[...] = (_dot(wt_ref[B_Q:, :], ht_ref[...]) + bcol_ref[B_Q:, :]).astype(BF16)
    t = _dot(h_ref[...], wk_ref[...]) + bk_ref[...]
    for kv in range(B_KV_HEADS):
        kk_ref[kv] = t[:, kv * B_HEAD_DIM:(kv + 1) * B_HEAD_DIM].astype(BF16)


def _b_proj(x2, g, wt, bcol, wk, bk, tm=1024):
    n, d = x2.shape
    return pl.pallas_call(
        _b_proj_kernel,
        grid=(n // tm,),
        in_specs=[pl.BlockSpec((tm, d), lambda i: (i, 0)),
                  _resident((1, d)),
                  _resident((B_Q + B_KV, d)),
                  _resident((B_Q + B_KV, 1)),
                  _resident((d, B_KV)),
                  _resident((1, B_KV))],
        out_specs=(pl.BlockSpec((tm // BLK, B_HEAD_DIM, N_HEADS * BLK), lambda i: (i, 0, 0)),
                   pl.BlockSpec((B_KV_HEADS, tm, B_HEAD_DIM), lambda i: (0, i, 0)),
                   pl.BlockSpec((B_KV, tm), lambda i: (0, i))),
        out_shape=(jax.ShapeDtypeStruct((n // BLK, B_HEAD_DIM, N_HEADS * BLK), BF16),
                   jax.ShapeDtypeStruct((B_KV_HEADS, n, B_HEAD_DIM), BF16),
                   jax.ShapeDtypeStruct((B_KV, n), BF16)),
        scratch_shapes=[pltpu.VMEM((tm, d), BF16), pltpu.VMEM((d, tm), BF16)],
        compiler_params=_params(1),
        name="b_proj",
    )(x2, g, wt, bcol, wk, bk)


def _swa_kernel(sink_ref, qr_ref, kp_ref, kc_ref, vp_ref, vc_ref, bts_ref, out_ref):
    n = pl.program_id(1)
    c1 = (B_HEAD_DIM ** -0.5) * LOG2E
    group = N_HEADS // B_KV_HEADS
    blocks = (
        (qr_ref.at[0], kp_ref, kc_ref.at[:, 0:BLK, :], vp_ref, vc_ref.at[:, 0:BLK],
         jnp.where(n >= 1, 1, 2)),
        (qr_ref.at[1], kc_ref.at[:, 0:BLK, :], kc_ref.at[:, BLK:2 * BLK, :],
         vc_ref.at[:, 0:BLK], vc_ref.at[:, BLK:2 * BLK], 1))
    tasks = [(qb, kv, g2) for qb in range(2) for kv in range(B_KV_HEADS)
             for g2 in range(group // SWA_TASK_HEADS)]

    def scores(task):
        qb, kv, g2 = task
        q_ref, kp, kc = blocks[qb][0:3]
        h0 = kv * group + SWA_TASK_HEADS * g2
        q = q_ref[:, h0 * BLK:(h0 + SWA_TASK_HEADS) * BLK]
        return _dot(kp[kv], q), _dot(kc[kv], q)

    def attend(task, lp, lc):
        qb, kv, g2 = task
        vp, vc, prev_kind = blocks[qb][3:6]
        vrows = slice(kv * B_HEAD_DIM, (kv + 1) * B_HEAD_DIM)
        pps, pcs, invs = [], [], []
        for j in range(SWA_TASK_HEADS):
            h = kv * group + SWA_TASK_HEADS * g2 + j
            lanes = slice(j * BLK, (j + 1) * BLK)
            sink = sink_ref[:, h * BLK:(h + 1) * BLK] * LOG2E
            a = lp[:, lanes] * c1 + bts_ref[prev_kind, h]
            b = lc[:, lanes] * c1 + bts_ref[0, h]
            m = jnp.maximum(jnp.max(jnp.maximum(a, b), axis=0, keepdims=True), sink)
            pp = jnp.exp2(a - m)
            pc = jnp.exp2(b - m)
            den = jnp.sum(pp + pc, axis=0, keepdims=True) + jnp.exp2(sink - m)
            invs.append(1.0 / den)
            pps.append(pp.astype(BF16))
            pcs.append(pc.astype(BF16))
        ot = (_dot(vp[vrows, :], jnp.concatenate(pps, axis=1))
              + _dot(vc[vrows, :], jnp.concatenate(pcs, axis=1)))
        return [ot[:, j * BLK:(j + 1) * BLK] * invs[j] for j in range(SWA_TASK_HEADS)]

    outs = [[], []]
    ahead = SWA_AHEAD
    queue = [scores(task) for task in tasks[:ahead]]
    for t, task in enumerate(tasks):
        if t + ahead < len(tasks):
            queue.append(scores(tasks[t + ahead]))
        outs[task[0]] += attend(task, *queue.pop(0))
    for qb in range(2):
        out_ref[qb * BLK:(qb + 1) * BLK, :] = (
            jnp.concatenate(outs[qb], axis=0).T.astype(BF16))


def _swa_attn(qr, kk, vt, sinks_row, bts, batch, seq):
    nb2 = seq // (2 * BLK)
    cur = lambda b, n: b * nb2 + n
    prev = lambda b, n: 2 * (b * nb2 + n) - jnp.minimum(n, 1)
    return pl.pallas_call(
        _swa_kernel,
        grid=(batch, nb2),
        in_specs=[_resident((1, N_HEADS * BLK)),
                  pl.BlockSpec((2, B_HEAD_DIM, N_HEADS * BLK), lambda b, n: (cur(b, n), 0, 0)),
                  pl.BlockSpec((B_KV_HEADS, BLK, B_HEAD_DIM), lambda b, n: (0, prev(b, n), 0)),
                  pl.BlockSpec((B_KV_HEADS, 2 * BLK, B_HEAD_DIM), lambda b, n: (0, cur(b, n), 0)),
                  pl.BlockSpec((B_KV, BLK), lambda b, n: (0, prev(b, n))),
                  pl.BlockSpec((B_KV, 2 * BLK), lambda b, n: (0, cur(b, n))),
                  _resident((3, N_HEADS, BLK, BLK))],
        out_specs=pl.BlockSpec((2 * BLK, B_Q), lambda b, n: (cur(b, n), 0)),
        out_shape=jax.ShapeDtypeStruct((batch * seq, B_Q), BF16),
        compiler_params=_params(2),
        name="swa_attn",
    )(sinks_row, qr, kk, kk, vt, vt, bts)


FF_CHUNK = 256


def _ffn_kernel(x_ref, o_ref, wo_ref, bo_ref, g_ref, wgu_ref, wd_ref, fg_ref,
                out_ref, h_ref, act_ref, *, d_ff, final):
    x1 = x_ref[...] + _dot(o_ref[...], wo_ref[...]) + bo_ref[...]
    out_ref[...] = x1
    h_ref[...] = _rms(x1, g_ref[...]).astype(BF16)
    for c in range(d_ff // FF_CHUNK):
        gate = _dot(h_ref[...], wgu_ref[:, c * FF_CHUNK:(c + 1) * FF_CHUNK])
        up = _dot(h_ref[...], wgu_ref[:, d_ff + c * FF_CHUNK:d_ff + (c + 1) * FF_CHUNK])
        act_ref[:, c * FF_CHUNK:(c + 1) * FF_CHUNK] = (
            gate * jax.nn.sigmoid(gate) * up).astype(BF16)
    y = out_ref[...] + _dot(act_ref[...], wd_ref[...])
    if final:
        y = _rms(y, fg_ref[...])
    out_ref[...] = y


def _ffn(x2, o, wo, bo, g, wgu_all, wd_all, layer, fg, final, tm=1024):
    n, d = x2.shape
    d_ff = wd_all.shape[1]
    row = lambda i: (i, 0)
    pick = lambda i: (layer, 0, 0)
    return pl.pallas_call(
        functools.partial(_ffn_kernel, d_ff=d_ff, final=final),
        grid=(n // tm,),
        in_specs=[pl.BlockSpec((tm, d), row),
                  pl.BlockSpec((tm, o.shape[1]), row),
                  _resident(wo.shape),
                  _resident((1, d)),
                  _resident((1, d)),
                  pl.BlockSpec((None,) + wgu_all.shape[1:], pick, pipeline_mode=pl.Buffered(1)),
                  pl.BlockSpec((None,) + wd_all.shape[1:], pick, pipeline_mode=pl.Buffered(1)),
                  _resident((1, d))],
        out_specs=pl.BlockSpec((tm, d), row),
        out_shape=jax.ShapeDtypeStruct((n, d), F32),
        scratch_shapes=[pltpu.VMEM((tm, d), BF16),
                        pltpu.VMEM((tm, d_ff), BF16)],
        compiler_params=_params(1),
        name="ffn",
    )(x2, o, wo, bo, g, wgu_all, wd_all, fg)


def kernel(x, rel_bias, attn_norm, ffn_norm, final_norm, a_w_in, a_latent_norm, a_w_uv, a_w_out, b_w_in, b_b_in, b_sinks, b_w_out, b_b_out, ffn_w_gate_up, ffn_w_down):
    batch, seq, d = x.shape
    assert attn_norm.shape[0] == 2 and a_w_in.shape[0] == 1 and b_w_in.shape[0] == 1
    assert seq % SC == 0 and (batch * seq) % 1024 == 0 and d % BLK == 0
    assert a_w_in.shape[2] == A_IN and b_w_in.shape[2] == B_Q + 2 * B_KV
    assert rel_bias.shape == (REL_BUCKETS, N_HEADS)
    x2 = x.reshape(batch * seq, d)
    bts, btd = _bias_tiles(rel_bias)
    fg = final_norm.reshape(1, d)

    gu_shape, dn_shape = ffn_w_gate_up.shape, ffn_w_down.shape
    qr, qir, wto, kidx, ckv, ckvt, wgu_bf, wd_bf = _a_proj(
        x2, attn_norm[0].reshape(1, d), jnp.swapaxes(a_w_in, 1, 2),
        a_latent_norm[0].reshape(1, A_LATENT),
        ffn_w_gate_up.reshape(-1, gu_shape[-1]), ffn_w_down.reshape(-1, dn_shape[-1]))
    wgu_all, wd_all = wgu_bf.reshape(gu_shape), wd_bf.reshape(dn_shape)
    wuvt = jnp.swapaxes(a_w_uv[0], 1, 2).astype(BF16)
    o = _dsa_attn(qr, qir, wto, kidx, ckv, ckvt, wuvt, btd, batch, seq)
    x2 = _ffn(x2, o, a_w_out[0].astype(BF16), jnp.zeros((1, d), F32),
              ffn_norm[0].reshape(1, d), wgu_all, wd_all, 0, fg, final=False)

    wb, bb = b_w_in[0].astype(BF16), b_b_in[0]
    wqv = jnp.concatenate([wb[:, :B_Q], wb[:, B_Q + B_KV:]], axis=1).T
    bqv = jnp.concatenate([bb[:B_Q], bb[B_Q + B_KV:]]).reshape(-1, 1)
    qr, kk, vt = _b_proj(
        x2, attn_norm[1].reshape(1, d), wqv, bqv, wb[:, B_Q:B_Q + B_KV],
        bb[B_Q:B_Q + B_KV].reshape(1, B_KV))
    sinks_row = jnp.repeat(b_sinks[0], BLK).reshape(1, N_HEADS * BLK)
    o = _swa_attn(qr, kk, vt, sinks_row, bts, batch, seq)
    x2 = _ffn(x2, o, b_w_out[0].astype(BF16), b_b_out[0].reshape(1, d),
              ffn_norm[1].reshape(1, d), wgu_all, wd_all, 1, fg, final=True)
    return x2.reshape(batch, seq, d)
```

```python
import functools
import math

import numpy as np
import jax
import jax.numpy as jnp
from jax import lax
from jax.experimental import pallas as pl
from jax.experimental.pallas import tpu as pltpu

F32 = jnp.float32
BF16 = jnp.bfloat16
I32 = jnp.int32

EPS = 1e-6
N_HEADS = 16
A_LATENT = 128
A_VHEAD = 64
IDX_HEADS = 8
IDX_DIM = 64
TOPK_MAX = 256
B_KV_HEADS = 2
B_HEAD_DIM = 64
REL_BUCKETS = 32
REL_MAX_DIST = 128

BLK = 128
SC = 256
PROJ_ROWS = 512
ONES_ROWS = 16
SEARCH_SPLIT = 22
SWA_TASK_HEADS = 1
SWA_AHEAD = 4
INT_MIN = -(2 ** 31)
NEG_INF = float("-inf")
LOG2E = math.log2(math.e)
VMEM_LIMIT = 56 * 1024 * 1024


def _rms(xf, g):
    ms = jnp.mean(xf * xf, axis=-1, keepdims=True)
    return xf * lax.rsqrt(ms + EPS) * g


def _dot(a, b):
    return jnp.dot(a, b, preferred_element_type=F32)


def _dot_nt(a, b):
    return lax.dot_general(a, b, (((1,), (1,)), ((), ())), preferred_element_type=F32)


def _params(n_axes):
    return pltpu.CompilerParams(
        dimension_semantics=("arbitrary",) * n_axes, vmem_limit_bytes=VMEM_LIMIT)


def _resident(shape):
    zeros = (0,) * len(shape)
    return pl.BlockSpec(shape, lambda *_: zeros, pipeline_mode=pl.Buffered(1))


def _bucket(rel):
    n = np.maximum(rel, 0)
    max_exact = REL_BUCKETS // 2
    nf = np.maximum(n, max_exact).astype(np.float64)
    large = max_exact + (np.log(nf / max_exact) / math.log(REL_MAX_DIST / max_exact)
                         * (REL_BUCKETS - max_exact)).astype(np.int32)
    large = np.minimum(large, REL_BUCKETS - 1)
    return np.where(n < max_exact, n, large).astype(np.int32)


def _bucket_tables():
    k = np.arange(BLK)[:, None]
    q = np.arange(BLK)[None, :]
    return _bucket(np.stack([q - k, BLK + q - k]))


def _bias_kernel(rb_ref, bidx_ref, bts_ref, btd_ref):
    k = lax.broadcasted_iota(I32, (BLK, BLK), 0)
    q = lax.broadcasted_iota(I32, (BLK, BLK), 1)
    window = (k <= q, k > q)
    for h in range(N_HEADS):
        for kind in range(2):
            bidx = bidx_ref[kind]

            t = jnp.zeros((BLK, BLK), F32)
            for b in range(REL_BUCKETS):
                t = jnp.where(bidx == b, rb_ref[b, h], t)
            bts_ref[kind, h] = jnp.where(window[kind], t * LOG2E, NEG_INF)
            btd_ref[kind, h] = (t - rb_ref[REL_BUCKETS - 1, h]) * LOG2E
        bts_ref[2, h] = jnp.full((BLK, BLK), NEG_INF, F32)
        btd_ref[2, h] = jnp.zeros((BLK, BLK), F32)


def _bias_tiles(rel_bias):
    return pl.pallas_call(
        _bias_kernel,
        out_shape=(jax.ShapeDtypeStruct((3, N_HEADS, BLK, BLK), F32),
                   jax.ShapeDtypeStruct((3, N_HEADS, BLK, BLK), F32)),
        in_specs=[pl.BlockSpec(memory_space=pltpu.SMEM),
                  pl.BlockSpec(memory_space=pltpu.VMEM)],
        out_specs=(pl.BlockSpec(memory_space=pltpu.VMEM),
                   pl.BlockSpec(memory_space=pltpu.VMEM)),
        name="bias_tiles",
    )(rel_bias, jnp.asarray(_bucket_tables()))


A_QLAT = N_HEADS * A_LATENT
A_QIDX = IDX_HEADS * IDX_DIM
A_QIDX_ROW = A_QLAT + A_LATENT
A_KIDX_ROW = A_QIDX_ROW + A_QIDX
A_IN = A_KIDX_ROW + IDX_DIM + IDX_HEADS
A_KC_ROW = A_KIDX_ROW + IDX_DIM + 16
A_WT_ROWS = A_KC_ROW + 2 * A_LATENT


def _a_proj_kernel(x_ref, g_ref, w_ref, lg_ref, wgu_ref, wd_ref,
                   qr_ref, qir_ref, wto_ref, kidx_ref, ckv_ref, ckvt_ref,
                   wgu_bf_ref, wd_bf_ref, h_ref, ht_ref, wt_ref):
    tm = x_ref.shape[0]

    @pl.when(pl.program_id(0) == 0)
    def _():
        wt_ref[0:A_IN, :] = w_ref[...].astype(BF16)
        wt_ref[A_IN:A_KC_ROW, :] = jnp.zeros((A_KC_ROW - A_IN, wt_ref.shape[1]), BF16)
        wt_ref[A_KC_ROW:A_KC_ROW + A_LATENT, :] = w_ref[A_QLAT:A_QIDX_ROW, :].astype(BF16)
        wt_ref[A_KC_ROW + A_LATENT:A_KC_ROW + A_LATENT + IDX_DIM, :] = (
            w_ref[A_KIDX_ROW:A_KIDX_ROW + IDX_DIM, :].astype(BF16))
        wt_ref[A_KC_ROW + A_LATENT + IDX_DIM:, :] = jnp.zeros(
            (2 * A_LATENT - A_LATENT - IDX_DIM, wt_ref.shape[1]), BF16)

    wgu_bf_ref[...] = wgu_ref[...].astype(BF16)
    wd_bf_ref[...] = wd_ref[...].astype(BF16)
    hf = _rms(x_ref[...], g_ref[...])
    h_ref[...] = hf.astype(BF16)
    ht_ref[...] = hf.T.astype(BF16)
    heads_per_dot = PROJ_ROWS // A_LATENT
    for c in range(A_QLAT // PROJ_ROWS):
        r = _dot(wt_ref[c * PROJ_ROWS:(c + 1) * PROJ_ROWS, :], ht_ref[...]).astype(BF16)
        for hh in range(heads_per_dot):
            h = heads_per_dot * c + hh
            for blk in range(tm // BLK):
                qr_ref[blk, :, h * BLK:(h + 1) * BLK] = (
                    r[hh * A_LATENT:(hh + 1) * A_LATENT, blk * BLK:(blk + 1) * BLK])
    r = _dot(wt_ref[A_QIDX_ROW:A_KIDX_ROW, :], ht_ref[...]).astype(BF16)
    for j in range(IDX_HEADS):
        for blk in range(tm // BLK):
            qir_ref[blk, :, j * BLK:(j + 1) * BLK] = (
                r[j * IDX_DIM:(j + 1) * IDX_DIM, blk * BLK:(blk + 1) * BLK])
    w = _dot(wt_ref[A_KIDX_ROW + IDX_DIM:A_KC_ROW, :], ht_ref[...])
    wto_ref[...] = w[:IDX_HEADS] * (IDX_HEADS ** -0.5)
    t = _dot_nt(h_ref[...], wt_ref[A_KC_ROW:, :])
    ckv = _rms(t[:, :A_LATENT], lg_ref[...])
    ckv_ref[...] = ckv.astype(BF16)
    kidx_ref[...] = t[:, A_LATENT:A_LATENT + IDX_DIM].astype(BF16)
    for b2 in range(tm // SC):
        ckvt_ref[b2] = ckv[b2 * SC:(b2 + 1) * SC, :].T.astype(BF16)


def _a_proj(x2, g, w, lg, wgu2, wd2, tm=512):
    n, d = x2.shape
    steps = n // tm
    gu_rows, dn_rows = wgu2.shape[0] // steps, wd2.shape[0] // steps
    row = lambda i: (i, 0)
    blk3 = lambda i: (i, 0, 0)
    return pl.pallas_call(
        _a_proj_kernel,
        grid=(steps,),
        in_specs=[pl.BlockSpec((tm, d), row),
                  _resident((1, d)),
                  pl.BlockSpec((None,) + w.shape[1:], lambda i: (0, 0, 0),
                               pipeline_mode=pl.Buffered(1)),
                  _resident((1, A_LATENT)),
                  pl.BlockSpec((gu_rows, wgu2.shape[1]), row),
                  pl.BlockSpec((dn_rows, wd2.shape[1]), row)],
        out_specs=(pl.BlockSpec((tm // BLK, A_LATENT, A_QLAT), blk3),
                   pl.BlockSpec((tm // BLK, IDX_DIM, IDX_HEADS * BLK), blk3),
                   pl.BlockSpec((IDX_HEADS, tm), lambda i: (0, i)),
                   pl.BlockSpec((tm, IDX_DIM), row),
                   pl.BlockSpec((tm, A_LATENT), row),
                   pl.BlockSpec((tm // SC, A_LATENT, SC), blk3),
                   pl.BlockSpec((gu_rows, wgu2.shape[1]), row),
                   pl.BlockSpec((dn_rows, wd2.shape[1]), row)),
        out_shape=(jax.ShapeDtypeStruct((n // BLK, A_LATENT, A_QLAT), BF16),
                   jax.ShapeDtypeStruct((n // BLK, IDX_DIM, IDX_HEADS * BLK), BF16),
                   jax.ShapeDtypeStruct((IDX_HEADS, n), F32),
                   jax.ShapeDtypeStruct((n, IDX_DIM), BF16),
                   jax.ShapeDtypeStruct((n, A_LATENT), BF16),
                   jax.ShapeDtypeStruct((n // SC, A_LATENT, SC), BF16),
                   jax.ShapeDtypeStruct(wgu2.shape, BF16),
                   jax.ShapeDtypeStruct(wd2.shape, BF16)),
        scratch_shapes=[pltpu.VMEM((tm, d), BF16), pltpu.VMEM((d, tm), BF16),
                        pltpu.VMEM((A_WT_ROWS, d), BF16)],
        compiler_params=_params(1),
        name="a_proj",
    )(x2, g, w, lg, wgu2, wd2)


def _dsa_kernel(qr_ref, qirn_ref, wtn_ref, kidx_ref, ckv_ref, ckvt_ref, wuvt_ref,
                btd_ref, tri_ref, out_ref,
                score_ref, acc_ref, l_ref, *, k_sel, n_sc):
    i = pl.program_id(1)
    nsc = i // 2 + 1
    kf = float(k_sel)
    krow = lax.broadcasted_iota(I32, (SC, BLK), 0)
    qpos = i * BLK + lax.broadcasted_iota(I32, (SC, BLK), 1)

    def causal(sc):
        return sc * SC + krow <= qpos

    def key_rows(ref, sc, count=1):
        return ref[pl.ds(pl.multiple_of(sc * SC, SC), count * SC), :]

    def colsum(x):
        return jnp.sum(x.reshape(SC // 8, 8, BLK), axis=0)

    w2 = wtn_ref[...] * (IDX_DIM ** -0.5)
    nsc_next = (i + 1) // 2 + 1
    qpos_next = qpos + BLK

    def index_next(sc0, count):
        d = _dot(key_rows(kidx_ref, sc0, count), qirn_ref[0])
        for c in range(count):
            rows = slice(c * SC, (c + 1) * SC)
            s = w2[0:1, :] * jnp.maximum(d[rows, 0:BLK], 0.0)
            for j in range(1, IDX_HEADS):
                s = s + w2[j:j + 1, :] * jnp.maximum(d[rows, j * BLK:(j + 1) * BLK], 0.0)
            score_ref[sc0 + c] = jnp.where((sc0 + c) * SC + krow <= qpos_next, s, NEG_INF)

    def put_threshold(thr, n_ge):
        score_ref[n_sc, 0:8, :] = jnp.broadcast_to(thr, (8, BLK))
        score_ref[n_sc, 8:16, :] = jnp.broadcast_to(n_ge, (8, BLK))

    @pl.when((i + 1) * BLK <= k_sel)
    def _():
        def body(sc, carry):
            score_ref[sc] = jnp.where(causal(sc), 0.0, NEG_INF)
            return carry
        lax.fori_loop(0, nsc, body, 0)
        put_threshold(jnp.zeros((1, BLK), F32), jnp.full((1, BLK), kf, F32))

    def key_to_float(key):
        return lax.bitcast_convert_type(key ^ ((key >> 31) & 0x7FFFFFFF), F32)

    def search(nblk):
        def count_ge(cand):
            cb = jnp.broadcast_to(cand, (8, BLK))
            parts = [jnp.zeros((8, BLK), F32)] * 8
            for t in range(nblk * BLK // 8):
                sc, r = divmod(t * 8, SC)
                ge = score_ref[sc, r:r + 8, :] >= cb
                parts[t % 8] = parts[t % 8] + jnp.where(ge, 1.0, 0.0)
            part = sum(parts[1:], parts[0])
            return jnp.sum(part, axis=0, keepdims=True)

        n0 = count_ge(jnp.zeros((1, BLK), F32))
        take = n0 >= kf
        ans = jnp.where(take, 0, INT_MIN)
        n_ans = jnp.where(take, n0, kf)
        first = jnp.left_shift(1, 30)
        cand_f = jnp.where(take, key_to_float(first), key_to_float(INT_MIN | first))

        def bit_body(it, carry):
            ans, cand_f, n_ans = carry
            cand = ans | jnp.left_shift(1, 30 - it)
            nxt = jnp.left_shift(1, jnp.maximum(29 - it, 0))
            f_take, f_keep = key_to_float(cand | nxt), key_to_float(ans | nxt)
            n_cand = count_ge(cand_f)
            take = n_cand >= kf
            return (jnp.where(take, cand, ans), jnp.where(take, f_take, f_keep),
                    jnp.where(take, n_cand, n_ans))
        carry = lax.fori_loop(0, SEARCH_SPLIT, bit_body, (ans, cand_f, n_ans))
        ans, _, n_ans = lax.cond(
            jnp.max(carry[2]) > kf,
            lambda c: lax.fori_loop(SEARCH_SPLIT, 31, bit_body, c), lambda c: c, carry)
        return key_to_float(ans), n_ans

    @pl.when((i + 1) * BLK > k_sel)
    def _():
        for nblk in range(k_sel // BLK + 1, 2 * n_sc + 1):
            @pl.when(i + 1 == nblk)
            def _(nblk=nblk):
                put_threshold(*search(nblk))
        thr = score_ref[n_sc, 0:1, :]
        n_ge = score_ref[n_sc, 8:9, :]

        @pl.when(jnp.max(n_ge) > kf)
        def _():
            def gt_body(sc, part):
                return part + colsum(jnp.where(score_ref[sc] > thr, 1.0, 0.0))
            n_gt = jnp.sum(lax.fori_loop(0, nsc, gt_body, jnp.zeros((8, BLK), F32)),
                           axis=0, keepdims=True)
            need = kf - n_gt

            def mask_body(sc, off):
                score = score_ref[sc]
                eq = score == thr
                eqf = jnp.where(eq, 1.0, 0.0)
                before = _dot(tri_ref[...], eqf.astype(BF16)) + off
                score_ref[sc] = jnp.where(eq & (before >= need), NEG_INF, score)
                return off + jnp.sum(eqf, axis=0, keepdims=True)
            lax.fori_loop(0, nsc, mask_body, jnp.zeros((1, BLK), F32))

    c1 = (A_LATENT ** -0.5) * LOG2E
    hcols = N_HEADS * BLK
    thr_a = score_ref[n_sc, 0:1, :]

    def logits_group(sc0, m, count, kinds=None):
        lt = _dot(key_rows(ckv_ref, sc0, count), qr_ref[0])
        madds = [jnp.where(score_ref[sc0 + c] >= thr_a, 0.0, NEG_INF) for c in range(count)]
        new_m = []
        for h in range(N_HEADS):
            hs = slice(h * BLK, (h + 1) * BLK)
            mh = m[:, hs]
            for c in range(count):
                v = lt[c * SC:(c + 1) * SC, hs] * c1 + madds[c]
                if kinds is not None and kinds[2 * c:2 * c + 2] != (None, None):
                    halves = []
                    for half in range(2):
                        part = v[half * BLK:(half + 1) * BLK]
                        kind = kinds[2 * c + half]
                        halves.append(part if kind is None else part + btd_ref[kind, h])
                    v = jnp.concatenate(halves, axis=0)
                l_ref[sc0 + c, h] = v
                mh = jnp.maximum(mh, jnp.max(v, axis=0, keepdims=True))
            new_m.append(mh)
        return jnp.concatenate(new_m, axis=1)

    n_far = nsc - 2
    m = jnp.full((1, hcols), NEG_INF, F32)
    m = lax.fori_loop(0, n_far // 2, lambda t, m: logits_group(2 * t, m, 2), m)
    m = lax.cond((n_far > 0) & (n_far % 2 == 1),
                 lambda m: logits_group(n_far - 1, m, 1), lambda m: m, m)
    even = i % 2 == 0
    m = lax.cond(
        nsc >= 2,
        lambda m: lax.cond(even,
                           lambda m: logits_group(nsc - 2, m, 2, (None, 1, 0, None)),
                           lambda m: logits_group(nsc - 2, m, 2, (None, None, 1, 0)), m),
        lambda m: lax.cond(even,
                           lambda m: logits_group(0, m, 1, (0, None)),
                           lambda m: logits_group(0, m, 1, (1, 0)), m),
        m)

    def pv_group(sc0, count):
        index_next(sc0, count)
        vt = jnp.concatenate(
            [jnp.concatenate([ckvt_ref[sc0 + c] for c in range(count)], axis=1),
             jnp.ones((ONES_ROWS, count * SC), BF16)], axis=0)
        for hp in range(N_HEADS // 2):
            cols = []
            for h in (2 * hp, 2 * hp + 1):
                hs = slice(h * BLK, (h + 1) * BLK)
                ps = [jnp.exp2(l_ref[sc0 + c, h] - m[:, hs]).astype(BF16) for c in range(count)]
                cols.append(jnp.concatenate(ps, axis=0))
            pair = jnp.concatenate(cols, axis=1)
            acc_ref[hp] = acc_ref[hp] + _dot(vt, pair)

    acc_ref[...] = jnp.zeros(acc_ref.shape, F32)

    def pv_body(t, carry):
        pv_group(2 * t, 2)
        return carry
    lax.fori_loop(0, nsc // 2, pv_body, 0)

    @pl.when(nsc % 2 == 1)
    def _():
        pv_group(nsc - 1, 1)

    @pl.when((nsc_next > nsc) & (nsc_next <= n_sc))
    def _():
        index_next(nsc, 1)

    outs = []
    for h in range(N_HEADS):
        lanes = slice((h % 2) * BLK, (h % 2 + 1) * BLK)
        inv = 1.0 / acc_ref[h // 2, A_LATENT:A_LATENT + 1, lanes]
        o = (acc_ref[h // 2, 0:A_LATENT, lanes] * inv).astype(BF16)
        outs.append(_dot(wuvt_ref[h], o))
    out_ref[...] = jnp.concatenate(outs, axis=0).T.astype(BF16)


def _dsa_attn(qr, qir, wt, kidx, ckv, ckvt, wuvt, btd, batch, seq):
    nblk = seq // BLK
    n_sc = seq // SC
    k_sel = min(TOPK_MAX, seq // 4)
    hcols = N_HEADS * BLK
    tri = jnp.asarray(np.tril(np.ones((SC, SC), np.float32), -1), BF16)
    return pl.pallas_call(
        functools.partial(_dsa_kernel, k_sel=k_sel, n_sc=n_sc),
        grid=(batch, nblk),
        in_specs=[pl.BlockSpec((1, A_LATENT, hcols), lambda b, i: (b * nblk + i, 0, 0)),
                  pl.BlockSpec((1, IDX_DIM, IDX_HEADS * BLK),
                               lambda b, i: (b * nblk + jnp.minimum(i + 1, nblk - 1), 0, 0)),
                  pl.BlockSpec((IDX_HEADS, BLK),
                               lambda b, i: (0, b * nblk + jnp.minimum(i + 1, nblk - 1))),
                  pl.BlockSpec((seq, IDX_DIM), lambda b, i: (b, 0)),
                  pl.BlockSpec((seq, A_LATENT), lambda b, i: (b, 0)),
                  pl.BlockSpec((n_sc, A_LATENT, SC), lambda b, i: (b, 0, 0)),
                  _resident((N_HEADS, A_VHEAD, A_LATENT)),
                  _resident((3, N_HEADS, BLK, BLK)),
                  _resident((SC, SC))],
        out_specs=pl.BlockSpec((BLK, N_HEADS * A_VHEAD), lambda b, i: (b * nblk + i, 0)),
        out_shape=jax.ShapeDtypeStruct((batch * seq, N_HEADS * A_VHEAD), BF16),
        scratch_shapes=[
            pltpu.VMEM((n_sc + 1, SC, BLK), F32),
            pltpu.VMEM((N_HEADS // 2, A_LATENT + ONES_ROWS, 2 * BLK), F32),
            pltpu.VMEM((n_sc, N_HEADS, SC, BLK), F32),
        ],
        compiler_params=_params(2),
        name="dsa_attn",
    )(qr, qir, wt, kidx, ckv, ckvt, wuvt, btd, tri)


B_Q = N_HEADS * B_HEAD_DIM
B_KV = B_KV_HEADS * B_HEAD_DIM


def _b_proj_kernel(x_ref, g_ref, wt_ref, bcol_ref, wk_ref, bk_ref,
                   qr_ref, kk_ref, vt_ref, h_ref, ht_ref):
    tm = x_ref.shape[0]
    hf = _rms(x_ref[...], g_ref[...])
    h_ref[...] = hf.astype(BF16)
    ht_ref[...] = hf.T.astype(BF16)
    heads_per_dot = PROJ_ROWS // B_HEAD_DIM
    for c in range(B_Q // PROJ_ROWS):
        rows = slice(c * PROJ_ROWS, (c + 1) * PROJ_ROWS)
        r = (_dot(wt_ref[rows, :], ht_ref[...]) + bcol_ref[rows, :]).astype(BF16)
        for hh in range(heads_per_dot):
            h = c * heads_per_dot + hh
            for blk in range(tm // BLK):
                qr_ref[blk, :, h * BLK:(h + 1) * BLK] = (
                    r[hh * B_HEAD_DIM:(hh + 1) * B_HEAD_DIM, blk * BLK:(blk + 1) * BLK])
    vt_ref[...] = (_dot(wt_ref[B_Q:, :], ht_ref[...]) + bcol_ref[B_Q:, :]).astype(BF16)
    t = _dot(h_ref[...], wk_ref[...]) + bk_ref[...]
    for kv in range(B_KV_HEADS):
        kk_ref[kv] = t[:, kv * B_HEAD_DIM:(kv + 1) * B_HEAD_DIM].astype(BF16)


def _b_proj(x2, g, wt, bcol, wk, bk, tm=1024):
    n, d = x2.shape
    return pl.pallas_call(
        _b_proj_kernel,
        grid=(n // tm,),
        in_specs=[pl.BlockSpec((tm, d), lambda i: (i, 0)),
                  _resident((1, d)),
                  _resident((B_Q + B_KV, d)),
                  _resident((B_Q + B_KV, 1)),
                  _resident((d, B_KV)),
                  _resident((1, B_KV))],
        out_specs=(pl.BlockSpec((tm // BLK, B_HEAD_DIM, N_HEADS * BLK), lambda i: (i, 0, 0)),
                   pl.BlockSpec((B_KV_HEADS, tm, B_HEAD_DIM), lambda i: (0, i, 0)),
                   pl.BlockSpec((B_KV, tm), lambda i: (0, i))),
        out_shape=(jax.ShapeDtypeStruct((n // BLK, B_HEAD_DIM, N_HEADS * BLK), BF16),
                   jax.ShapeDtypeStruct((B_KV_HEADS, n, B_HEAD_DIM), BF16),
                   jax.ShapeDtypeStruct((B_KV, n), BF16)),
        scratch_shapes=[pltpu.VMEM((tm, d), BF16), pltpu.VMEM((d, tm), BF16)],
        compiler_params=_params(1),
        name="b_proj",
    )(x2, g, wt, bcol, wk, bk)


def _swa_kernel(sink_ref, qr_ref, kp_ref, kc_ref, vp_ref, vc_ref, bts_ref, out_ref):
    n = pl.program_id(1)
    c1 = (B_HEAD_DIM ** -0.5) * LOG2E
    group = N_HEADS // B_KV_HEADS
    blocks = (
        (qr_ref.at[0], kp_ref, kc_ref.at[:, 0:BLK, :], vp_ref, vc_ref.at[:, 0:BLK],
         jnp.where(n >= 1, 1, 2)),
        (qr_ref.at[1], kc_ref.at[:, 0:BLK, :], kc_ref.at[:, BLK:2 * BLK, :],
         vc_ref.at[:, 0:BLK], vc_ref.at[:, BLK:2 * BLK], 1))
    tasks = [(qb, kv, g2) for qb in range(2) for kv in range(B_KV_HEADS)
             for g2 in range(group // SWA_TASK_HEADS)]

    def scores(task):
        qb, kv, g2 = task
        q_ref, kp, kc = blocks[qb][0:3]
        h0 = kv * group + SWA_TASK_HEADS * g2
        q = q_ref[:, h0 * BLK:(h0 + SWA_TASK_HEADS) * BLK]
        return _dot(kp[kv], q), _dot(kc[kv], q)

    def attend(task, lp, lc):
        qb, kv, g2 = task
        vp, vc, prev_kind = blocks[qb][3:6]
        vrows = slice(kv * B_HEAD_DIM, (kv + 1) * B_HEAD_DIM)
        pps, pcs, invs = [], [], []
        for j in range(SWA_TASK_HEADS):
            h = kv * group + SWA_TASK_HEADS * g2 + j
            lanes = slice(j * BLK, (j + 1) * BLK)
            sink = sink_ref[:, h * BLK:(h + 1) * BLK] * LOG2E
            a = lp[:, lanes] * c1 + bts_ref[prev_kind, h]
            b = lc[:, lanes] * c1 + bts_ref[0, h]
            m = jnp.maximum(jnp.max(jnp.maximum(a, b), axis=0, keepdims=True), sink)
            pp = jnp.exp2(a - m)
            pc = jnp.exp2(b - m)
            den = jnp.sum(pp + pc, axis=0, keepdims=True) + jnp.exp2(sink - m)
            invs.append(1.0 / den)
            pps.append(pp.astype(BF16))
            pcs.append(pc.astype(BF16))
        ot = (_dot(vp[vrows, :], jnp.concatenate(pps, axis=1))
              + _dot(vc[vrows, :], jnp.concatenate(pcs, axis=1)))
        return [ot[:, j * BLK:(j + 1) * BLK] * invs[j] for j in range(SWA_TASK_HEADS)]

    outs = [[], []]
    ahead = SWA_AHEAD
    queue = [scores(task) for task in tasks[:ahead]]
    for t, task in enumerate(tasks):
        if t + ahead < len(tasks):
            queue.append(scores(tasks[t + ahead]))
        outs[task[0]] += attend(task, *queue.pop(0))
    for qb in range(2):
        out_ref[qb * BLK:(qb + 1) * BLK, :] = (
            jnp.concatenate(outs[qb], axis=0).T.astype(BF16))


def _swa_attn(qr, kk, vt, sinks_row, bts, batch, seq):
    nb2 = seq // (2 * BLK)
    cur = lambda b, n: b * nb2 + n
    prev = lambda b, n: 2 * (b * nb2 + n) - jnp.minimum(n, 1)
    return pl.pallas_call(
        _swa_kernel,
        grid=(batch, nb2),
        in_specs=[_resident((1, N_HEADS * BLK)),
                  pl.BlockSpec((2, B_HEAD_DIM, N_HEADS * BLK), lambda b, n: (cur(b, n), 0, 0)),
                  pl.BlockSpec((B_KV_HEADS, BLK, B_HEAD_DIM), lambda b, n: (0, prev(b, n), 0)),
                  pl.BlockSpec((B_KV_HEADS, 2 * BLK, B_HEAD_DIM), lambda b, n: (0, cur(b, n), 0)),
                  pl.BlockSpec((B_KV, BLK), lambda b, n: (0, prev(b, n))),
                  pl.BlockSpec((B_KV, 2 * BLK), lambda b, n: (0, cur(b, n))),
                  _resident((3, N_HEADS, BLK, BLK))],
        out_specs=pl.BlockSpec((2 * BLK, B_Q), lambda b, n: (cur(b, n), 0)),
        out_shape=jax.ShapeDtypeStruct((batch * seq, B_Q), BF16),
        compiler_params=_params(2),
        name="swa_attn",
    )(sinks_row, qr, kk, kk, vt, vt, bts)


FF_CHUNK = 256


def _ffn_kernel(x_ref, o_ref, wo_ref, bo_ref, g_ref, wgu_ref, wd_ref, fg_ref,
                out_ref, h_ref, act_ref, *, d_ff, final):
    x1 = x_ref[...] + _dot(o_ref[...], wo_ref[...]) + bo_ref[...]
    out_ref[...] = x1
    h_ref[...] = _rms(x1, g_ref[...]).astype(BF16)
    for c in range(d_ff // FF_CHUNK):
        gate = _dot(h_ref[...], wgu_ref[:, c * FF_CHUNK:(c + 1) * FF_CHUNK])
        up = _dot(h_ref[...], wgu_ref[:, d_ff + c * FF_CHUNK:d_ff + (c + 1) * FF_CHUNK])
        act_ref[:, c * FF_CHUNK:(c + 1) * FF_CHUNK] = (
            gate * jax.nn.sigmoid(gate) * up).astype(BF16)
    y = out_ref[...] + _dot(act_ref[...], wd_ref[...])
    if final:
        y = _rms(y, fg_ref[...])
    out_ref[...] = y


def _ffn(x2, o, wo, bo, g, wgu_all, wd_all, layer, fg, final, tm=1024):
    n, d = x2.shape
    d_ff = wd_all.shape[1]
    row = lambda i: (i, 0)
    pick = lambda i: (layer, 0, 0)
    return pl.pallas_call(
        functools.partial(_ffn_kernel, d_ff=d_ff, final=final),
        grid=(n // tm,),
        in_specs=[pl.BlockSpec((tm, d), row),
                  pl.BlockSpec((tm, o.shape[1]), row),
                  _resident(wo.shape),
                  _resident((1, d)),
                  _resident((1, d)),
                  pl.BlockSpec((None,) + wgu_all.shape[1:], pick, pipeline_mode=pl.Buffered(1)),
                  pl.BlockSpec((None,) + wd_all.shape[1:], pick, pipeline_mode=pl.Buffered(1)),
                  _resident((1, d))],
        out_specs=pl.BlockSpec((tm, d), row),
        out_shape=jax.ShapeDtypeStruct((n, d), F32),
        scratch_shapes=[pltpu.VMEM((tm, d), BF16),
                        pltpu.VMEM((tm, d_ff), BF16)],
        compiler_params=_params(1),
        name="ffn",
    )(x2, o, wo, bo, g, wgu_all, wd_all, fg)


def kernel(x, rel_bias, attn_norm, ffn_norm, final_norm, a_w_in, a_latent_norm, a_w_uv, a_w_out, b_w_in, b_b_in, b_sinks, b_w_out, b_b_out, ffn_w_gate_up, ffn_w_down):
    batch, seq, d = x.shape
    assert attn_norm.shape[0] == 2 and a_w_in.shape[0] == 1 and b_w_in.shape[0] == 1
    assert seq % SC == 0 and (batch * seq) % 1024 == 0 and d % BLK == 0
    assert a_w_in.shape[2] == A_IN and b_w_in.shape[2] == B_Q + 2 * B_KV
    assert rel_bias.shape == (REL_BUCKETS, N_HEADS)
    x2 = x.reshape(batch * seq, d)
    bts, btd = _bias_tiles(rel_bias)
    fg = final_norm.reshape(1, d)

    gu_shape, dn_shape = ffn_w_gate_up.shape, ffn_w_down.shape
    qr, qir, wto, kidx, ckv, ckvt, wgu_bf, wd_bf = _a_proj(
        x2, attn_norm[0].reshape(1, d), jnp.swapaxes(a_w_in, 1, 2),
        a_latent_norm[0].reshape(1, A_LATENT),
        ffn_w_gate_up.reshape(-1, gu_shape[-1]), ffn_w_down.reshape(-1, dn_shape[-1]))
    wgu_all, wd_all = wgu_bf.reshape(gu_shape), wd_bf.reshape(dn_shape)
    wuvt = jnp.swapaxes(a_w_uv[0], 1, 2).astype(BF16)
    o = _dsa_attn(qr, qir, wto, kidx, ckv, ckvt, wuvt, btd, batch, seq)
    x2 = _ffn(x2, o, a_w_out[0].astype(BF16), jnp.zeros((1, d), F32),
              ffn_norm[0].reshape(1, d), wgu_all, wd_all, 0, fg, final=False)

    wb, bb = b_w_in[0].astype(BF16), b_b_in[0]
    wqv = jnp.concatenate([wb[:, :B_Q], wb[:, B_Q + B_KV:]], axis=1).T
    bqv = jnp.concatenate([bb[:B_Q], bb[B_Q + B_KV:]]).reshape(-1, 1)
    qr, kk, vt = _b_proj(
        x2, attn_norm[1].reshape(1, d), wqv, bqv, wb[:, B_Q:B_Q + B_KV],
        bb[B_Q:B_Q + B_KV].reshape(1, B_KV))
    sinks_row = jnp.repeat(b_sinks[0], BLK).reshape(1, N_HEADS * BLK)
    o = _swa_attn(qr, kk, vt, sinks_row, bts, batch, seq)
    x2 = _ffn(x2, o, b_w_out[0].astype(BF16), b_b_out[0].reshape(1, d),
              ffn_norm[1].reshape(1, d), wgu_all, wd_all, 1, fg, final=True)
    return x2.reshape(batch, seq, d)
```

```python
import functools
import math

import numpy as np
import jax
import jax.numpy as jnp
from jax import lax
from jax.experimental import pallas as pl
from jax.experimental.pallas import tpu as pltpu

F32 = jnp.float32
BF16 = jnp.bfloat16
I32 = jnp.int32

EPS = 1e-6
N_HEADS = 16
A_LATENT = 128
A_VHEAD = 64
IDX_HEADS = 8
IDX_DIM = 64
TOPK_MAX = 256
B_KV_HEADS = 2
B_HEAD_DIM = 64
REL_BUCKETS = 32
REL_MAX_DIST = 128

BLK = 128
SC = 256
PROJ_ROWS = 512
ONES_ROWS = 16
SEARCH_SPLIT = 25
SWA_TASK_HEADS = 1
SWA_AHEAD = 4
INT_MIN = -(2 ** 31)
NEG_INF = float("-inf")
LOG2E = math.log2(math.e)
VMEM_LIMIT = 56 * 1024 * 1024


def _rms(xf, g):
    ms = jnp.mean(xf * xf, axis=-1, keepdims=True)
    return xf * lax.rsqrt(ms + EPS) * g


def _dot(a, b):
    return jnp.dot(a, b, preferred_element_type=F32)


def _dot_nt(a, b):
    return lax.dot_general(a, b, (((1,), (1,)), ((), ())), preferred_element_type=F32)


def _params(n_axes):
    return pltpu.CompilerParams(
        dimension_semantics=("arbitrary",) * n_axes, vmem_limit_bytes=VMEM_LIMIT)


def _resident(shape):
    zeros = (0,) * len(shape)
    return pl.BlockSpec(shape, lambda *_: zeros, pipeline_mode=pl.Buffered(1))


def _bucket(rel):
    n = np.maximum(rel, 0)
    max_exact = REL_BUCKETS // 2
    nf = np.maximum(n, max_exact).astype(np.float64)
    large = max_exact + (np.log(nf / max_exact) / math.log(REL_MAX_DIST / max_exact)
                         * (REL_BUCKETS - max_exact)).astype(np.int32)
    large = np.minimum(large, REL_BUCKETS - 1)
    return np.where(n < max_exact, n, large).astype(np.int32)


def _bucket_tables():
    k = np.arange(BLK)[:, None]
    q = np.arange(BLK)[None, :]
    return _bucket(np.stack([q - k, BLK + q - k]))


def _bias_kernel(rb_ref, bidx_ref, bts_ref, btd_ref):
    k = lax.broadcasted_iota(I32, (BLK, BLK), 0)
    q = lax.broadcasted_iota(I32, (BLK, BLK), 1)
    window = (k <= q, k > q)
    for h in range(N_HEADS):
        for kind in range(2):
            bidx = bidx_ref[kind]

            t = jnp.zeros((BLK, BLK), F32)
            for b in range(REL_BUCKETS):
                t = jnp.where(bidx == b, rb_ref[b, h], t)
            bts_ref[kind, h] = jnp.where(window[kind], t * LOG2E, NEG_INF)
            btd_ref[kind, h] = (t - rb_ref[REL_BUCKETS - 1, h]) * LOG2E
        bts_ref[2, h] = jnp.full((BLK, BLK), NEG_INF, F32)
        btd_ref[2, h] = jnp.zeros((BLK, BLK), F32)


def _bias_tiles(rel_bias):
    return pl.pallas_call(
        _bias_kernel,
        out_shape=(jax.ShapeDtypeStruct((3, N_HEADS, BLK, BLK), F32),
                   jax.ShapeDtypeStruct((3, N_HEADS, BLK, BLK), F32)),
        in_specs=[pl.BlockSpec(memory_space=pltpu.SMEM),
                  pl.BlockSpec(memory_space=pltpu.VMEM)],
        out_specs=(pl.BlockSpec(memory_space=pltpu.VMEM),
                   pl.BlockSpec(memory_space=pltpu.VMEM)),
        name="bias_tiles",
    )(rel_bias, jnp.asarray(_bucket_tables()))


A_QLAT = N_HEADS * A_LATENT
A_QIDX = IDX_HEADS * IDX_DIM
A_QIDX_ROW = A_QLAT + A_LATENT
A_KIDX_ROW = A_QIDX_ROW + A_QIDX
A_IN = A_KIDX_ROW + IDX_DIM + IDX_HEADS
A_KC_ROW = A_KIDX_ROW + IDX_DIM + 16
A_WT_ROWS = A_KC_ROW + 2 * A_LATENT


def _a_proj_kernel(x_ref, g_ref, w_ref, lg_ref, wgu_ref, wd_ref,
                   qr_ref, qir_ref, wto_ref, kidx_ref, ckv_ref, ckvt_ref,
                   wgu_bf_ref, wd_bf_ref, h_ref, ht_ref, wt_ref):
    tm = x_ref.shape[0]

    @pl.when(pl.program_id(0) == 0)
    def _():
        wt_ref[0:A_IN, :] = w_ref[...].astype(BF16)
        wt_ref[A_IN:A_KC_ROW, :] = jnp.zeros((A_KC_ROW - A_IN, wt_ref.shape[1]), BF16)
        wt_ref[A_KC_ROW:A_KC_ROW + A_LATENT, :] = w_ref[A_QLAT:A_QIDX_ROW, :].astype(BF16)
        wt_ref[A_KC_ROW + A_LATENT:A_KC_ROW + A_LATENT + IDX_DIM, :] = (
            w_ref[A_KIDX_ROW:A_KIDX_ROW + IDX_DIM, :].astype(BF16))
        wt_ref[A_KC_ROW + A_LATENT + IDX_DIM:, :] = jnp.zeros(
            (2 * A_LATENT - A_LATENT - IDX_DIM, wt_ref.shape[1]), BF16)

    wgu_bf_ref[...] = wgu_ref[...].astype(BF16)
    wd_bf_ref[...] = wd_ref[...].astype(BF16)
    hf = _rms(x_ref[...], g_ref[...])
    h_ref[...] = hf.astype(BF16)
    ht_ref[...] = hf.T.astype(BF16)
    heads_per_dot = PROJ_ROWS // A_LATENT
    for c in range(A_QLAT // PROJ_ROWS):
        r = _dot(wt_ref[c * PROJ_ROWS:(c + 1) * PROJ_ROWS, :], ht_ref[...]).astype(BF16)
        for hh in range(heads_per_dot):
            h = heads_per_dot * c + hh
            for blk in range(tm // BLK):
                qr_ref[blk, :, h * BLK:(h + 1) * BLK] = (
                    r[hh * A_LATENT:(hh + 1) * A_LATENT, blk * BLK:(blk + 1) * BLK])
    r = _dot(wt_ref[A_QIDX_ROW:A_KIDX_ROW, :], ht_ref[...]).astype(BF16)
    for j in range(IDX_HEADS):
        for blk in range(tm // BLK):
            qir_ref[blk, :, j * BLK:(j + 1) * BLK] = (
                r[j * IDX_DIM:(j + 1) * IDX_DIM, blk * BLK:(blk + 1) * BLK])
    w = _dot(wt_ref[A_KIDX_ROW + IDX_DIM:A_KC_ROW, :], ht_ref[...])
    wto_ref[...] = w[:IDX_HEADS] * (IDX_HEADS ** -0.5)
    t = _dot_nt(h_ref[...], wt_ref[A_KC_ROW:, :])
    ckv = _rms(t[:, :A_LATENT], lg_ref[...])
    ckv_ref[...] = ckv.astype(BF16)
    kidx_ref[...] = t[:, A_LATENT:A_LATENT + IDX_DIM].astype(BF16)
    for b2 in range(tm // SC):
        ckvt_ref[b2] = ckv[b2 * SC:(b2 + 1) * SC, :].T.astype(BF16)


def _a_proj(x2, g, w, lg, wgu2, wd2, tm=512):
    n, d = x2.shape
    steps = n // tm
    gu_rows, dn_rows = wgu2.shape[0] // steps, wd2.shape[0] // steps
    row = lambda i: (i, 0)
    blk3 = lambda i: (i, 0, 0)
    return pl.pallas_call(
        _a_proj_kernel,
        grid=(steps,),
        in_specs=[pl.BlockSpec((tm, d), row),
                  _resident((1, d)),
                  pl.BlockSpec((None,) + w.shape[1:], lambda i: (0, 0, 0),
                               pipeline_mode=pl.Buffered(1)),
                  _resident((1, A_LATENT)),
                  pl.BlockSpec((gu_rows, wgu2.shape[1]), row),
                  pl.BlockSpec((dn_rows, wd2.shape[1]), row)],
        out_specs=(pl.BlockSpec((tm // BLK, A_LATENT, A_QLAT), blk3),
                   pl.BlockSpec((tm // BLK, IDX_DIM, IDX_HEADS * BLK), blk3),
                   pl.BlockSpec((IDX_HEADS, tm), lambda i: (0, i)),
                   pl.BlockSpec((tm, IDX_DIM), row),
                   pl.BlockSpec((tm, A_LATENT), row),
                   pl.BlockSpec((tm // SC, A_LATENT, SC), blk3),
                   pl.BlockSpec((gu_rows, wgu2.shape[1]), row),
                   pl.BlockSpec((dn_rows, wd2.shape[1]), row)),
        out_shape=(jax.ShapeDtypeStruct((n // BLK, A_LATENT, A_QLAT), BF16),
                   jax.ShapeDtypeStruct((n // BLK, IDX_DIM, IDX_HEADS * BLK), BF16),
                   jax.ShapeDtypeStruct((IDX_HEADS, n), F32),
                   jax.ShapeDtypeStruct((n, IDX_DIM), BF16),
                   jax.ShapeDtypeStruct((n, A_LATENT), BF16),
                   jax.ShapeDtypeStruct((n // SC, A_LATENT, SC), BF16),
                   jax.ShapeDtypeStruct(wgu2.shape, BF16),
                   jax.ShapeDtypeStruct(wd2.shape, BF16)),
        scratch_shapes=[pltpu.VMEM((tm, d), BF16), pltpu.VMEM((d, tm), BF16),
                        pltpu.VMEM((A_WT_ROWS, d), BF16)],
        compiler_params=_params(1),
        name="a_proj",
    )(x2, g, w, lg, wgu2, wd2)


def _dsa_kernel(qr_ref, qirn_ref, wtn_ref, kidx_ref, ckv_ref, ckvt_ref, wuvt_ref,
                btd_ref, tri_ref, out_ref,
                score_ref, acc_ref, l_ref, *, k_sel, n_sc):
    i = pl.program_id(1)
    nsc = i // 2 + 1
    kf = float(k_sel)
    krow = lax.broadcasted_iota(I32, (SC, BLK), 0)
    qpos = i * BLK + lax.broadcasted_iota(I32, (SC, BLK), 1)

    def causal(sc):
        return sc * SC + krow <= qpos

    def key_rows(ref, sc, count=1):
        return ref[pl.ds(pl.multiple_of(sc * SC, SC), count * SC), :]

    def colsum(x):
        return jnp.sum(x.reshape(SC // 8, 8, BLK), axis=0)

    w2 = wtn_ref[...] * (IDX_DIM ** -0.5)
    nsc_next = (i + 1) // 2 + 1
    qpos_next = qpos + BLK

    def index_next(sc0, count):
        d = _dot(key_rows(kidx_ref, sc0, count), qirn_ref[0])
        for c in range(count):
            rows = slice(c * SC, (c + 1) * SC)
            s = w2[0:1, :] * jnp.maximum(d[rows, 0:BLK], 0.0)
            for j in range(1, IDX_HEADS):
                s = s + w2[j:j + 1, :] * jnp.maximum(d[rows, j * BLK:(j + 1) * BLK], 0.0)
            score_ref[sc0 + c] = jnp.where((sc0 + c) * SC + krow <= qpos_next, s, NEG_INF)

    def put_threshold(thr, n_ge):
        score_ref[n_sc, 0:8, :] = jnp.broadcast_to(thr, (8, BLK))
        score_ref[n_sc, 8:16, :] = jnp.broadcast_to(n_ge, (8, BLK))

    @pl.when((i + 1) * BLK <= k_sel)
    def _():
        def body(sc, carry):
            score_ref[sc] = jnp.where(causal(sc), 0.0, NEG_INF)
            return carry
        lax.fori_loop(0, nsc, body, 0)
        put_threshold(jnp.zeros((1, BLK), F32), jnp.full((1, BLK), kf, F32))

    def key_to_float(key):
        return lax.bitcast_convert_type(key ^ ((key >> 31) & 0x7FFFFFFF), F32)

    def search(nblk):
        def count_ge(cand):
            cb = jnp.broadcast_to(cand, (8, BLK))
            parts = [jnp.zeros((8, BLK), F32)] * 8
            for t in range(nblk * BLK // 8):
                sc, r = divmod(t * 8, SC)
                ge = score_ref[sc, r:r + 8, :] >= cb
                parts[t % 8] = parts[t % 8] + jnp.where(ge, 1.0, 0.0)
            while len(parts) > 1:
                parts = [a + b for a, b in zip(parts[::2], parts[1::2])]
            return jnp.sum(parts[0], axis=0, keepdims=True)

        n0 = count_ge(jnp.zeros((1, BLK), F32))
        take = n0 >= kf
        ans = jnp.where(take, 0, INT_MIN)
        n_ans = jnp.where(take, n0, kf)
        first = jnp.left_shift(1, 30)
        cand_f = jnp.where(take, key_to_float(first), key_to_float(INT_MIN | first))

        def bit_body(it, carry):
            ans, cand_f, n_ans = carry
            cand = ans | jnp.left_shift(1, 30 - it)
            nxt = jnp.left_shift(1, jnp.maximum(29 - it, 0))
            f_take, f_keep = key_to_float(cand | nxt), key_to_float(ans | nxt)
            n_cand = count_ge(cand_f)
            take = n_cand >= kf
            return (jnp.where(take, cand, ans), jnp.where(take, f_take, f_keep),
                    jnp.where(take, n_cand, n_ans))
        carry = lax.fori_loop(0, SEARCH_SPLIT, bit_body, (ans, cand_f, n_ans))
        ans, _, n_ans = lax.cond(
            jnp.max(carry[2]) > kf,
            lambda c: lax.fori_loop(SEARCH_SPLIT, 31, bit_body, c), lambda c: c, carry)
        return key_to_float(ans), n_ans

    @pl.when((i + 1) * BLK > k_sel)
    def _():
        for nblk in range(k_sel // BLK + 1, 2 * n_sc + 1):
            @pl.when(i + 1 == nblk)
            def _(nblk=nblk):
                put_threshold(*search(nblk))
        thr = score_ref[n_sc, 0:1, :]
        n_ge = score_ref[n_sc, 8:9, :]

        @pl.when(jnp.max(n_ge) > kf)
        def _():
            def gt_body(sc, part):
                return part + colsum(jnp.where(score_ref[sc] > thr, 1.0, 0.0))
            n_gt = jnp.sum(lax.fori_loop(0, nsc, gt_body, jnp.zeros((8, BLK), F32)),
                           axis=0, keepdims=True)
            need = kf - n_gt

            def mask_body(sc, off):
                score = score_ref[sc]
                eq = score == thr
                eqf = jnp.where(eq, 1.0, 0.0)
                before = _dot(tri_ref[...], eqf.astype(BF16)) + off
                score_ref[sc] = jnp.where(eq & (before >= need), NEG_INF, score)
                return off + jnp.sum(eqf, axis=0, keepdims=True)
            lax.fori_loop(0, nsc, mask_body, jnp.zeros((1, BLK), F32))

    c1 = (A_LATENT ** -0.5) * LOG2E
    hcols = N_HEADS * BLK
    thr_a = score_ref[n_sc, 0:1, :]

    def logits_group(sc0, m, count, kinds=None):
        lt = _dot(key_rows(ckv_ref, sc0, count), qr_ref[0])
        madds = [jnp.where(score_ref[sc0 + c] >= thr_a, 0.0, NEG_INF) for c in range(count)]
        new_m = []
        for h in range(N_HEADS):
            hs = slice(h * BLK, (h + 1) * BLK)
            mh = m[:, hs]
            for c in range(count):
                v = lt[c * SC:(c + 1) * SC, hs] * c1 + madds[c]
                if kinds is not None and kinds[2 * c:2 * c + 2] != (None, None):
                    halves = []
                    for half in range(2):
                        part = v[half * BLK:(half + 1) * BLK]
                        kind = kinds[2 * c + half]
                        halves.append(part if kind is None else part + btd_ref[kind, h])
                    v = jnp.concatenate(halves, axis=0)
                l_ref[sc0 + c, h] = v
                mh = jnp.maximum(mh, jnp.max(v, axis=0, keepdims=True))
            new_m.append(mh)
        return jnp.concatenate(new_m, axis=1)

    n_far = nsc - 2
    m = jnp.full((1, hcols), NEG_INF, F32)
    m = lax.fori_loop(0, n_far // 2, lambda t, m: logits_group(2 * t, m, 2), m)
    m = lax.cond((n_far > 0) & (n_far % 2 == 1),
                 lambda m: logits_group(n_far - 1, m, 1), lambda m: m, m)
    even = i % 2 == 0
    m = lax.cond(
        nsc >= 2,
        lambda m: lax.cond(even,
                           lambda m: logits_group(nsc - 2, m, 2, (None, 1, 0, None)),
                           lambda m: logits_group(nsc - 2, m, 2, (None, None, 1, 0)), m),
        lambda m: lax.cond(even,
                           lambda m: logits_group(0, m, 1, (0, None)),
                           lambda m: logits_group(0, m, 1, (1, 0)), m),
        m)

    def pv_group(sc0, count):
        index_next(sc0, count)
        vt = jnp.concatenate(
            [jnp.concatenate([ckvt_ref[sc0 + c] for c in range(count)], axis=1),
             jnp.ones((ONES_ROWS, count * SC), BF16)], axis=0)
        for hp in range(N_HEADS // 2):
            cols = []
            for h in (2 * hp, 2 * hp + 1):
                hs = slice(h * BLK, (h + 1) * BLK)
                ps = [jnp.exp2(l_ref[sc0 + c, h] - m[:, hs]).astype(BF16) for c in range(count)]
                cols.append(jnp.concatenate(ps, axis=0))
            pair = jnp.concatenate(cols, axis=1)
            acc_ref[hp] = acc_ref[hp] + _dot(vt, pair)

    acc_ref[...] = jnp.zeros(acc_ref.shape, F32)

    def pv_body(t, carry):
        pv_group(2 * t, 2)
        return carry
    lax.fori_loop(0, nsc // 2, pv_body, 0)

    @pl.when(nsc % 2 == 1)
    def _():
        pv_group(nsc - 1, 1)

    @pl.when((nsc_next > nsc) & (nsc_next <= n_sc))
    def _():
        index_next(nsc, 1)

    outs = []
    for h in range(N_HEADS):
        lanes = slice((h % 2) * BLK, (h % 2 + 1) * BLK)
        inv = 1.0 / acc_ref[h // 2, A_LATENT:A_LATENT + 1, lanes]
        o = (acc_ref[h // 2, 0:A_LATENT, lanes] * inv).astype(BF16)
        outs.append(_dot(wuvt_ref[h], o))
    out_ref[...] = jnp.concatenate(outs, axis=0).T.astype(BF16)


def _dsa_attn(qr, qir, wt, kidx, ckv, ckvt, wuvt, btd, batch, seq):
    nblk = seq // BLK
    n_sc = seq // SC
    k_sel = min(TOPK_MAX, seq // 4)
    hcols = N_HEADS * BLK
    tri = jnp.asarray(np.tril(np.ones((SC, SC), np.float32), -1), BF16)
    return pl.pallas_call(
        functools.partial(_dsa_kernel, k_sel=k_sel, n_sc=n_sc),
        grid=(batch, nblk),
        in_specs=[pl.BlockSpec((1, A_LATENT, hcols), lambda b, i: (b * nblk + i, 0, 0)),
                  pl.BlockSpec((1, IDX_DIM, IDX_HEADS * BLK),
                               lambda b, i: (b * nblk + jnp.minimum(i + 1, nblk - 1), 0, 0)),
                  pl.BlockSpec((IDX_HEADS, BLK),
                               lambda b, i: (0, b * nblk + jnp.minimum(i + 1, nblk - 1))),
                  pl.BlockSpec((seq, IDX_DIM), lambda b, i: (b, 0)),
                  pl.BlockSpec((seq, A_LATENT), lambda b, i: (b, 0)),
                  pl.BlockSpec((n_sc, A_LATENT, SC), lambda b, i: (b, 0, 0)),
                  _resident((N_HEADS, A_VHEAD, A_LATENT)),
                  _resident((3, N_HEADS, BLK, BLK)),
                  _resident((SC, SC))],
        out_specs=pl.BlockSpec((BLK, N_HEADS * A_VHEAD), lambda b, i: (b * nblk + i, 0)),
        out_shape=jax.ShapeDtypeStruct((batch * seq, N_HEADS * A_VHEAD), BF16),
        scratch_shapes=[
            pltpu.VMEM((n_sc + 1, SC, BLK), F32),
            pltpu.VMEM((N_HEADS // 2, A_LATENT + ONES_ROWS, 2 * BLK), F32),
            pltpu.VMEM((n_sc, N_HEADS, SC, BLK), F32),
        ],
        compiler_params=_params(2),
        name="dsa_attn",
    )(qr, qir, wt, kidx, ckv, ckvt, wuvt, btd, tri)


B_Q = N_HEADS * B_HEAD_DIM
B_KV = B_KV_HEADS * B_HEAD_DIM


def _b_proj_kernel(x_ref, g_ref, wt_ref, bcol_ref, wk_ref, bk_ref,
                   qr_ref, kk_ref, vt_ref, h_ref, ht_ref):
    tm = x_ref.shape[0]
    hf = _rms(x_ref[...], g_ref[...])
    h_ref[...] = hf.astype(BF16)
    ht_ref[...] = hf.T.astype(BF16)
    heads_per_dot = PROJ_ROWS // B_HEAD_DIM
    for c in range(B_Q // PROJ_ROWS):
        rows = slice(c * PROJ_ROWS, (c + 1) * PROJ_ROWS)
        r = (_dot(wt_ref[rows, :], ht_ref[...]) + bcol_ref[rows, :]).astype(BF16)
        for hh in range(heads_per_dot):
            h = c * heads_per_dot + hh
            for blk in range(tm // BLK):
                qr_ref[blk, :, h * BLK:(h + 1) * BLK] = (
                    r[hh * B_HEAD_DIM:(hh + 1) * B_HEAD_DIM, blk * BLK:(blk + 1) * BLK])
    vt_ref[...] = (_dot(wt_ref[B_Q:, :], ht_ref[...]) + bcol_ref[B_Q:, :]).astype(BF16)
    t = _dot(h_ref[...], wk_ref[...]) + bk_ref[...]
    for kv in range(B_KV_HEADS):
        kk_ref[kv] = t[:, kv * B_HEAD_DIM:(kv + 1) * B_HEAD_DIM].astype(BF16)


def _b_proj(x2, g, wt, bcol, wk, bk, tm=1024):
    n, d = x2.shape
    return pl.pallas_call(
        _b_proj_kernel,
        grid=(n // tm,),
        in_specs=[pl.BlockSpec((tm, d), lambda i: (i, 0)),
                  _resident((1, d)),
                  _resident((B_Q + B_KV, d)),
                  _resident((B_Q + B_KV, 1)),
                  _resident((d, B_KV)),
                  _resident((1, B_KV))],
        out_specs=(pl.BlockSpec((tm // BLK, B_HEAD_DIM, N_HEADS * BLK), lambda i: (i, 0, 0)),
                   pl.BlockSpec((B_KV_HEADS, tm, B_HEAD_DIM), lambda i: (0, i, 0)),
                   pl.BlockSpec((B_KV, tm), lambda i: (0, i))),
        out_shape=(jax.ShapeDtypeStruct((n // BLK, B_HEAD_DIM, N_HEADS * BLK), BF16),
                   jax.ShapeDtypeStruct((B_KV_HEADS, n, B_HEAD_DIM), BF16),
                   jax.ShapeDtypeStruct((B_KV, n), BF16)),
        scratch_shapes=[pltpu.VMEM((tm, d), BF16), pltpu.VMEM((d, tm), BF16)],
        compiler_params=_params(1),
        name="b_proj",
    )(x2, g, wt, bcol, wk, bk)


def _swa_kernel(sink_ref, qr_ref, kp_ref, kc_ref, vp_ref, vc_ref, bts_ref, out_ref):
    n = pl.program_id(1)
    c1 = (B_HEAD_DIM ** -0.5) * LOG2E
    group = N_HEADS // B_KV_HEADS
    blocks = (
        (qr_ref.at[0], kp_ref, kc_ref.at[:, 0:BLK, :], vp_ref, vc_ref.at[:, 0:BLK],
         jnp.where(n >= 1, 1, 2)),
        (qr_ref.at[1], kc_ref.at[:, 0:BLK, :], kc_ref.at[:, BLK:2 * BLK, :],
         vc_ref.at[:, 0:BLK], vc_ref.at[:, BLK:2 * BLK], 1))
    tasks = [(qb, kv, g2) for qb in range(2) for kv in range(B_KV_HEADS)
             for g2 in range(group // SWA_TASK_HEADS)]

    def scores(task):
        qb, kv, g2 = task
        q_ref, kp, kc = blocks[qb][0:3]
        h0 = kv * group + SWA_TASK_HEADS * g2
        q = q_ref[:, h0 * BLK:(h0 + SWA_TASK_HEADS) * BLK]
        return _dot(kp[kv], q), _dot(kc[kv], q)

    def attend(task, lp, lc):
        qb, kv, g2 = task
        vp, vc, prev_kind = blocks[qb][3:6]
        vrows = slice(kv * B_HEAD_DIM, (kv + 1) * B_HEAD_DIM)
        pps, pcs, invs = [], [], []
        for j in range(SWA_TASK_HEADS):
            h = kv * group + SWA_TASK_HEADS * g2 + j
            lanes = slice(j * BLK, (j + 1) * BLK)
            sink = sink_ref[:, h * BLK:(h + 1) * BLK] * LOG2E
            a = lp[:, lanes] * c1 + bts_ref[prev_kind, h]
            b = lc[:, lanes] * c1 + bts_ref[0, h]
            m = jnp.maximum(jnp.max(jnp.maximum(a, b), axis=0, keepdims=True), sink)
            pp = jnp.exp2(a - m)
            pc = jnp.exp2(b - m)
            den = jnp.sum(pp + pc, axis=0, keepdims=True) + jnp.exp2(sink - m)
            invs.append(1.0 / den)
            pps.append(pp.astype(BF16))
            pcs.append(pc.astype(BF16))
        ot = (_dot(vp[vrows, :], jnp.concatenate(pps, axis=1))
              + _dot(vc[vrows, :], jnp.concatenate(pcs, axis=1)))
        return [ot[:, j * BLK:(j + 1) * BLK] * invs[j] for j in range(SWA_TASK_HEADS)]

    outs = [[], []]
    ahead = SWA_AHEAD
    queue = [scores(task) for task in tasks[:ahead]]
    for t, task in enumerate(tasks):
        if t + ahead < len(tasks):
            queue.append(scores(tasks[t + ahead]))
        outs[task[0]] += attend(task, *queue.pop(0))
    for qb in range(2):
        out_ref[qb * BLK:(qb + 1) * BLK, :] = (
            jnp.concatenate(outs[qb], axis=0).T.astype(BF16))


def _swa_attn(qr, kk, vt, sinks_row, bts, batch, seq):
    nb2 = seq // (2 * BLK)
    cur = lambda b, n: b * nb2 + n
    prev = lambda b, n: 2 * (b * nb2 + n) - jnp.minimum(n, 1)
    return pl.pallas_call(
        _swa_kernel,
        grid=(batch, nb2),
        in_specs=[_resident((1, N_HEADS * BLK)),
                  pl.BlockSpec((2, B_HEAD_DIM, N_HEADS * BLK), lambda b, n: (cur(b, n), 0, 0)),
                  pl.BlockSpec((B_KV_HEADS, BLK, B_HEAD_DIM), lambda b, n: (0, prev(b, n), 0)),
                  pl.BlockSpec((B_KV_HEADS, 2 * BLK, B_HEAD_DIM), lambda b, n: (0, cur(b, n), 0)),
                  pl.BlockSpec((B_KV, BLK), lambda b, n: (0, prev(b, n))),
                  pl.BlockSpec((B_KV, 2 * BLK), lambda b, n: (0, cur(b, n))),
                  _resident((3, N_HEADS, BLK, BLK))],
        out_specs=pl.BlockSpec((2 * BLK, B_Q), lambda b, n: (cur(b, n), 0)),
        out_shape=jax.ShapeDtypeStruct((batch * seq, B_Q), BF16),
        compiler_params=_params(2),
        name="swa_attn",
    )(sinks_row, qr, kk, kk, vt, vt, bts)


FF_CHUNK = 256


def _ffn_kernel(x_ref, o_ref, wo_ref, bo_ref, g_ref, wgu_ref, wd_ref, fg_ref,
                out_ref, h_ref, act_ref, *, d_ff, final):
    x1 = x_ref[...] + _dot(o_ref[...], wo_ref[...]) + bo_ref[...]
    out_ref[...] = x1
    h_ref[...] = _rms(x1, g_ref[...]).astype(BF16)
    for c in range(d_ff // FF_CHUNK):
        gate = _dot(h_ref[...], wgu_ref[:, c * FF_CHUNK:(c + 1) * FF_CHUNK])
        up = _dot(h_ref[...], wgu_ref[:, d_ff + c * FF_CHUNK:d_ff + (c + 1) * FF_CHUNK])
        act_ref[:, c * FF_CHUNK:(c + 1) * FF_CHUNK] = (
            gate * jax.nn.sigmoid(gate) * up).astype(BF16)
    y = out_ref[...] + _dot(act_ref[...], wd_ref[...])
    if final:
        y = _rms(y, fg_ref[...])
    out_ref[...] = y


def _ffn(x2, o, wo, bo, g, wgu_all, wd_all, layer, fg, final, tm=1024):
    n, d = x2.shape
    d_ff = wd_all.shape[1]
    row = lambda i: (i, 0)
    pick = lambda i: (layer, 0, 0)
    return pl.pallas_call(
        functools.partial(_ffn_kernel, d_ff=d_ff, final=final),
        grid=(n // tm,),
        in_specs=[pl.BlockSpec((tm, d), row),
                  pl.BlockSpec((tm, o.shape[1]), row),
                  _resident(wo.shape),
                  _resident((1, d)),
                  _resident((1, d)),
                  pl.BlockSpec((None,) + wgu_all.shape[1:], pick, pipeline_mode=pl.Buffered(1)),
                  pl.BlockSpec((None,) + wd_all.shape[1:], pick, pipeline_mode=pl.Buffered(1)),
                  _resident((1, d))],
        out_specs=pl.BlockSpec((tm, d), row),
        out_shape=jax.ShapeDtypeStruct((n, d), F32),
        scratch_shapes=[pltpu.VMEM((tm, d), BF16),
                        pltpu.VMEM((tm, d_ff), BF16)],
        compiler_params=_params(1),
        name="ffn",
    )(x2, o, wo, bo, g, wgu_all, wd_all, fg)


def kernel(x, rel_bias, attn_norm, ffn_norm, final_norm, a_w_in, a_latent_norm, a_w_uv, a_w_out, b_w_in, b_b_in, b_sinks, b_w_out, b_b_out, ffn_w_gate_up, ffn_w_down):
    batch, seq, d = x.shape
    assert attn_norm.shape[0] == 2 and a_w_in.shape[0] == 1 and b_w_in.shape[0] == 1
    assert seq % SC == 0 and (batch * seq) % 1024 == 0 and d % BLK == 0
    assert a_w_in.shape[2] == A_IN and b_w_in.shape[2] == B_Q + 2 * B_KV
    assert rel_bias.shape == (REL_BUCKETS, N_HEADS)
    x2 = x.reshape(batch * seq, d)
    bts, btd = _bias_tiles(rel_bias)
    fg = final_norm.reshape(1, d)

    gu_shape, dn_shape = ffn_w_gate_up.shape, ffn_w_down.shape
    qr, qir, wto, kidx, ckv, ckvt, wgu_bf, wd_bf = _a_proj(
        x2, attn_norm[0].reshape(1, d), jnp.swapaxes(a_w_in, 1, 2),
        a_latent_norm[0].reshape(1, A_LATENT),
        ffn_w_gate_up.reshape(-1, gu_shape[-1]), ffn_w_down.reshape(-1, dn_shape[-1]))
    wgu_all, wd_all = wgu_bf.reshape(gu_shape), wd_bf.reshape(dn_shape)
    wuvt = jnp.swapaxes(a_w_uv[0], 1, 2).astype(BF16)
    o = _dsa_attn(qr, qir, wto, kidx, ckv, ckvt, wuvt, btd, batch, seq)
    x2 = _ffn(x2, o, a_w_out[0].astype(BF16), jnp.zeros((1, d), F32),
              ffn_norm[0].reshape(1, d), wgu_all, wd_all, 0, fg, final=False)

    wb, bb = b_w_in[0].astype(BF16), b_b_in[0]
    wqv = jnp.concatenate([wb[:, :B_Q], wb[:, B_Q + B_KV:]], axis=1).T
    bqv = jnp.concatenate([bb[:B_Q], bb[B_Q + B_KV:]]).reshape(-1, 1)
    qr, kk, vt = _b_proj(
        x2, attn_norm[1].reshape(1, d), wqv, bqv, wb[:, B_Q:B_Q + B_KV],
        bb[B_Q:B_Q + B_KV].reshape(1, B_KV))
    sinks_row = jnp.repeat(b_sinks[0], BLK).reshape(1, N_HEADS * BLK)
    o = _swa_attn(qr, kk, vt, sinks_row, bts, batch, seq)
    x2 = _ffn(x2, o, b_w_out[0].astype(BF16), b_b_out[0].reshape(1, d),
              ffn_norm[1].reshape(1, d), wgu_all, wd_all, 1, fg, final=True)
    return x2.reshape(batch, seq, d)
```

```python
import functools
import math

import numpy as np
import jax
import jax.numpy as jnp
from jax import lax
from jax.experimental import pallas as pl
from jax.experimental.pallas import tpu as pltpu

F32 = jnp.float32
BF16 = jnp.bfloat16
I32 = jnp.int32

EPS = 1e-6
N_HEADS = 16
A_LATENT = 128
A_VHEAD = 64
IDX_HEADS = 8
IDX_DIM = 64
TOPK_MAX = 256
B_KV_HEADS = 2
B_HEAD_DIM = 64
REL_BUCKETS = 32
REL_MAX_DIST = 128

BLK = 128
SC = 256
PROJ_ROWS = 512
ONES_ROWS = 16
SEARCH_SPLIT = 25
SWA_TASK_HEADS = 1
SWA_AHEAD = 4
INT_MIN = -(2 ** 31)
NEG_INF = float("-inf")
LOG2E = math.log2(math.e)
VMEM_LIMIT = 56 * 1024 * 1024


def _rms(xf, g):
    ms = jnp.mean(xf * xf, axis=-1, keepdims=True)
    return xf * lax.rsqrt(ms + EPS) * g


def _dot(a, b):
    return jnp.dot(a, b, preferred_element_type=F32)


def _dot_nt(a, b):
    return lax.dot_general(a, b, (((1,), (1,)), ((), ())), preferred_element_type=F32)


def _params(n_axes):
    return pltpu.CompilerParams(
        dimension_semantics=("arbitrary",) * n_axes, vmem_limit_bytes=VMEM_LIMIT)


def _resident(shape):
    zeros = (0,) * len(shape)
    return pl.BlockSpec(shape, lambda *_: zeros, pipeline_mode=pl.Buffered(1))


def _bucket(rel):
    n = np.maximum(rel, 0)
    max_exact = REL_BUCKETS // 2
    nf = np.maximum(n, max_exact).astype(np.float64)
    large = max_exact + (np.log(nf / max_exact) / math.log(REL_MAX_DIST / max_exact)
                         * (REL_BUCKETS - max_exact)).astype(np.int32)
    large = np.minimum(large, REL_BUCKETS - 1)
    return np.where(n < max_exact, n, large).astype(np.int32)


def _bucket_tables():
    k = np.arange(BLK)[:, None]
    q = np.arange(BLK)[None, :]
    return _bucket(np.stack([q - k, BLK + q - k]))


def _bias_kernel(rb_ref, bidx_ref, bts_ref, btd_ref):
    k = lax.broadcasted_iota(I32, (BLK, BLK), 0)
    q = lax.broadcasted_iota(I32, (BLK, BLK), 1)
    window = (k <= q, k > q)
    for h in range(N_HEADS):
        for kind in range(2):
            bidx = bidx_ref[kind]

            t = jnp.zeros((BLK, BLK), F32)
            for b in range(REL_BUCKETS):
                t = jnp.where(bidx == b, rb_ref[b, h], t)
            bts_ref[kind, h] = jnp.where(window[kind], t * LOG2E, NEG_INF)
            btd_ref[kind, h] = (t - rb_ref[REL_BUCKETS - 1, h]) * LOG2E
        bts_ref[2, h] = jnp.full((BLK, BLK), NEG_INF, F32)
        btd_ref[2, h] = jnp.zeros((BLK, BLK), F32)


def _bias_tiles(rel_bias):
    return pl.pallas_call(
        _bias_kernel,
        out_shape=(jax.ShapeDtypeStruct((3, N_HEADS, BLK, BLK), F32),
                   jax.ShapeDtypeStruct((3, N_HEADS, BLK, BLK), F32)),
        in_specs=[pl.BlockSpec(memory_space=pltpu.SMEM),
                  pl.BlockSpec(memory_space=pltpu.VMEM)],
        out_specs=(pl.BlockSpec(memory_space=pltpu.VMEM),
                   pl.BlockSpec(memory_space=pltpu.VMEM)),
        name="bias_tiles",
    )(rel_bias, jnp.asarray(_bucket_tables()))


A_QLAT = N_HEADS * A_LATENT
A_QIDX = IDX_HEADS * IDX_DIM
A_QIDX_ROW = A_QLAT + A_LATENT
A_KIDX_ROW = A_QIDX_ROW + A_QIDX
A_IN = A_KIDX_ROW + IDX_DIM + IDX_HEADS
A_KC_ROW = A_KIDX_ROW + IDX_DIM + 16
A_WT_ROWS = A_KC_ROW + 2 * A_LATENT


def _a_proj_kernel(x_ref, g_ref, w_ref, lg_ref, wgu_ref, wd_ref,
                   qr_ref, qir_ref, wto_ref, kidx_ref, ckv_ref, ckvt_ref,
                   wgu_bf_ref, wd_bf_ref, h_ref, ht_ref, wt_ref):
    tm = x_ref.shape[0]

    @pl.when(pl.program_id(0) == 0)
    def _():
        wt_ref[0:A_IN, :] = w_ref[...].astype(BF16)
        wt_ref[A_IN:A_KC_ROW, :] = jnp.zeros((A_KC_ROW - A_IN, wt_ref.shape[1]), BF16)
        wt_ref[A_KC_ROW:A_KC_ROW + A_LATENT, :] = w_ref[A_QLAT:A_QIDX_ROW, :].astype(BF16)
        wt_ref[A_KC_ROW + A_LATENT:A_KC_ROW + A_LATENT + IDX_DIM, :] = (
            w_ref[A_KIDX_ROW:A_KIDX_ROW + IDX_DIM, :].astype(BF16))
        wt_ref[A_KC_ROW + A_LATENT + IDX_DIM:, :] = jnp.zeros(
            (2 * A_LATENT - A_LATENT - IDX_DIM, wt_ref.shape[1]), BF16)

    wgu_bf_ref[...] = wgu_ref[...].astype(BF16)
    wd_bf_ref[...] = wd_ref[...].astype(BF16)
    hf = _rms(x_ref[...], g_ref[...])
    h_ref[...] = hf.astype(BF16)
    ht_ref[...] = hf.T.astype(BF16)
    heads_per_dot = PROJ_ROWS // A_LATENT
    for c in range(A_QLAT // PROJ_ROWS):
        r = _dot(wt_ref[c * PROJ_ROWS:(c + 1) * PROJ_ROWS, :], ht_ref[...]).astype(BF16)
        for hh in range(heads_per_dot):
            h = heads_per_dot * c + hh
            for blk in range(tm // BLK):
                qr_ref[blk, :, h * BLK:(h + 1) * BLK] = (
                    r[hh * A_LATENT:(hh + 1) * A_LATENT, blk * BLK:(blk + 1) * BLK])
    r = _dot(wt_ref[A_QIDX_ROW:A_KIDX_ROW, :], ht_ref[...]).astype(BF16)
    for j in range(IDX_HEADS):
        for blk in range(tm // BLK):
            qir_ref[blk, :, j * BLK:(j + 1) * BLK] = (
                r[j * IDX_DIM:(j + 1) * IDX_DIM, blk * BLK:(blk + 1) * BLK])
    w = _dot(wt_ref[A_KIDX_ROW + IDX_DIM:A_KC_ROW, :], ht_ref[...])
    wto_ref[...] = w[:IDX_HEADS] * (IDX_HEADS ** -0.5)
    t = _dot_nt(h_ref[...], wt_ref[A_KC_ROW:, :])
    ckv = _rms(t[:, :A_LATENT], lg_ref[...])
    ckv_ref[...] = ckv.astype(BF16)
    kidx_ref[...] = t[:, A_LATENT:A_LATENT + IDX_DIM].astype(BF16)
    for b2 in range(tm // SC):
        ckvt_ref[b2] = ckv[b2 * SC:(b2 + 1) * SC, :].T.astype(BF16)


def _a_proj(x2, g, w, lg, wgu2, wd2, tm=512):
    n, d = x2.shape
    steps = n // tm
    gu_rows, dn_rows = wgu2.shape[0] // steps, wd2.shape[0] // steps
    row = lambda i: (i, 0)
    blk3 = lambda i: (i, 0, 0)
    return pl.pallas_call(
        _a_proj_kernel,
        grid=(steps,),
        in_specs=[pl.BlockSpec((tm, d), row),
                  _resident((1, d)),
                  pl.BlockSpec((None,) + w.shape[1:], lambda i: (0, 0, 0),
                               pipeline_mode=pl.Buffered(1)),
                  _resident((1, A_LATENT)),
                  pl.BlockSpec((gu_rows, wgu2.shape[1]), row),
                  pl.BlockSpec((dn_rows, wd2.shape[1]), row)],
        out_specs=(pl.BlockSpec((tm // BLK, A_LATENT, A_QLAT), blk3),
                   pl.BlockSpec((tm // BLK, IDX_DIM, IDX_HEADS * BLK), blk3),
                   pl.BlockSpec((IDX_HEADS, tm), lambda i: (0, i)),
                   pl.BlockSpec((tm, IDX_DIM), row),
                   pl.BlockSpec((tm, A_LATENT), row),
                   pl.BlockSpec((tm // SC, A_LATENT, SC), blk3),
                   pl.BlockSpec((gu_rows, wgu2.shape[1]), row),
                   pl.BlockSpec((dn_rows, wd2.shape[1]), row)),
        out_shape=(jax.ShapeDtypeStruct((n // BLK, A_LATENT, A_QLAT), BF16),
                   jax.ShapeDtypeStruct((n // BLK, IDX_DIM, IDX_HEADS * BLK), BF16),
                   jax.ShapeDtypeStruct((IDX_HEADS, n), F32),
                   jax.ShapeDtypeStruct((n, IDX_DIM), BF16),
                   jax.ShapeDtypeStruct((n, A_LATENT), BF16),
                   jax.ShapeDtypeStruct((n // SC, A_LATENT, SC), BF16),
                   jax.ShapeDtypeStruct(wgu2.shape, BF16),
                   jax.ShapeDtypeStruct(wd2.shape, BF16)),
        scratch_shapes=[pltpu.VMEM((tm, d), BF16), pltpu.VMEM((d, tm), BF16),
                        pltpu.VMEM((A_WT_ROWS, d), BF16)],
        compiler_params=_params(1),
        name="a_proj",
    )(x2, g, w, lg, wgu2, wd2)


def _dsa_kernel(qr_ref, qirn_ref, wtn_ref, kidx_ref, ckv_ref, ckvt_ref, wuvt_ref,
                btd_ref, tri_ref, out_ref,
                score_ref, acc_ref, l_ref, *, k_sel, n_sc):
    i = pl.program_id(1)
    nsc = i // 2 + 1
    kf = float(k_sel)
    krow = lax.broadcasted_iota(I32, (SC, BLK), 0)
    qpos = i * BLK + lax.broadcasted_iota(I32, (SC, BLK), 1)

    def causal(sc):
        return sc * SC + krow <= qpos

    def key_rows(ref, sc, count=1):
        return ref[pl.ds(pl.multiple_of(sc * SC, SC), count * SC), :]

    def colsum(x):
        return jnp.sum(x.reshape(SC // 8, 8, BLK), axis=0)

    w2 = wtn_ref[...] * (IDX_DIM ** -0.5)
    nsc_next = (i + 1) // 2 + 1
    qpos_next = qpos + BLK

    def index_next(sc0, count):
        d = _dot(key_rows(kidx_ref, sc0, count), qirn_ref[0])
        for c in range(count):
            rows = slice(c * SC, (c + 1) * SC)
            s = w2[0:1, :] * jnp.maximum(d[rows, 0:BLK], 0.0)
            for j in range(1, IDX_HEADS):
                s = s + w2[j:j + 1, :] * jnp.maximum(d[rows, j * BLK:(j + 1) * BLK], 0.0)
            score_ref[sc0 + c] = jnp.where((sc0 + c) * SC + krow <= qpos_next, s, NEG_INF)

    def put_threshold(thr, n_ge):
        score_ref[n_sc, 0:8, :] = jnp.broadcast_to(thr, (8, BLK))
        score_ref[n_sc, 8:16, :] = jnp.broadcast_to(n_ge, (8, BLK))

    @pl.when((i + 1) * BLK <= k_sel)
    def _():
        def body(sc, carry):
            score_ref[sc] = jnp.where(causal(sc), 0.0, NEG_INF)
            return carry
        lax.fori_loop(0, nsc, body, 0)
        put_threshold(jnp.zeros((1, BLK), F32), jnp.full((1, BLK), kf, F32))

    def key_to_float(key):
        return lax.bitcast_convert_type(key ^ ((key >> 31) & 0x7FFFFFFF), F32)

    def search(nblk):
        def count_ge(cand):
            cb = jnp.broadcast_to(cand, (8, BLK))
            parts = [jnp.zeros((8, BLK), F32)] * 8
            for t in range(nblk * BLK // 8):
                sc, r = divmod(t * 8, SC)
                ge = score_ref[sc, r:r + 8, :] >= cb
                parts[t % 8] = parts[t % 8] + jnp.where(ge, 1.0, 0.0)
            while len(parts) > 1:
                parts = [a + b for a, b in zip(parts[::2], parts[1::2])]
            return jnp.sum(parts[0], axis=0, keepdims=True)

        n0 = count_ge(jnp.zeros((1, BLK), F32))
        take = n0 >= kf
        ans = jnp.where(take, 0, INT_MIN)
        n_ans = jnp.where(take, n0, kf)
        first = jnp.left_shift(1, 30)
        cand_f = jnp.where(take, key_to_float(first), key_to_float(INT_MIN | first))

        def bit_body(it, carry):
            ans, cand_f, n_ans = carry
            cand = ans | jnp.left_shift(1, 30 - it)
            nxt = jnp.left_shift(1, jnp.maximum(29 - it, 0))
            f_take, f_keep = key_to_float(cand | nxt), key_to_float(ans | nxt)
            n_cand = count_ge(cand_f)
            take = n_cand >= kf
            return (jnp.where(take, cand, ans), jnp.where(take, f_take, f_keep),
                    jnp.where(take, n_cand, n_ans))
        carry = lax.fori_loop(0, SEARCH_SPLIT, bit_body, (ans, cand_f, n_ans))
        ans, _, n_ans = lax.cond(
            jnp.max(carry[2]) > kf,
            lambda c: lax.fori_loop(SEARCH_SPLIT, 31, bit_body, c), lambda c: c, carry)
        return key_to_float(ans), n_ans

    @pl.when((i + 1) * BLK > k_sel)
    def _():
        for nblk in range(k_sel // BLK + 1, 2 * n_sc + 1):
            @pl.when(i + 1 == nblk)
            def _(nblk=nblk):
                put_threshold(*search(nblk))
        thr = score_ref[n_sc, 0:1, :]
        n_ge = score_ref[n_sc, 8:9, :]

        @pl.when(jnp.max(n_ge) > kf)
        def _():
            def gt_body(sc, part):
                return part + colsum(jnp.where(score_ref[sc] > thr, 1.0, 0.0))
            n_gt = jnp.sum(lax.fori_loop(0, nsc, gt_body, jnp.zeros((8, BLK), F32)),
                           axis=0, keepdims=True)
            need = kf - n_gt

            def mask_body(sc, off):
                score = score_ref[sc]
                eq = score == thr
                eqf = jnp.where(eq, 1.0, 0.0)
                before = _dot(tri_ref[...], eqf.astype(BF16)) + off
                score_ref[sc] = jnp.where(eq & (before >= need), NEG_INF, score)
                return off + jnp.sum(eqf, axis=0, keepdims=True)
            lax.fori_loop(0, nsc, mask_body, jnp.zeros((1, BLK), F32))

    c1 = (A_LATENT ** -0.5) * LOG2E
    hcols = N_HEADS * BLK
    thr_a = score_ref[n_sc, 0:1, :]

    def logits_group(sc0, m, count, kinds=None):
        lt = _dot(key_rows(ckv_ref, sc0, count), qr_ref[0])
        madds = [jnp.where(score_ref[sc0 + c] >= thr_a, 0.0, NEG_INF) for c in range(count)]
        new_m = []
        for h in range(N_HEADS):
            hs = slice(h * BLK, (h + 1) * BLK)
            mh = m[:, hs]
            for c in range(count):
                v = lt[c * SC:(c + 1) * SC, hs] * c1 + madds[c]
                if kinds is not None and kinds[2 * c:2 * c + 2] != (None, None):
                    halves = []
                    for half in range(2):
                        part = v[half * BLK:(half + 1) * BLK]
                        kind = kinds[2 * c + half]
                        halves.append(part if kind is None else part + btd_ref[kind, h])
                    v = jnp.concatenate(halves, axis=0)
                l_ref[sc0 + c, h] = v
                mh = jnp.maximum(mh, jnp.max(v, axis=0, keepdims=True))
            new_m.append(mh)
        return jnp.concatenate(new_m, axis=1)

    n_far = nsc - 2
    m = jnp.full((1, hcols), NEG_INF, F32)
    m = lax.fori_loop(0, n_far // 2, lambda t, m: logits_group(2 * t, m, 2), m)
    m = lax.cond((n_far > 0) & (n_far % 2 == 1),
                 lambda m: logits_group(n_far - 1, m, 1), lambda m: m, m)
    even = i % 2 == 0
    m = lax.cond(
        nsc >= 2,
        lambda m: lax.cond(even,
                           lambda m: logits_group(nsc - 2, m, 2, (None, 1, 0, None)),
                           lambda m: logits_group(nsc - 2, m, 2, (None, None, 1, 0)), m),
        lambda m: lax.cond(even,
                           lambda m: logits_group(0, m, 1, (0, None)),
                           lambda m: logits_group(0, m, 1, (1, 0)), m),
        m)

    def pv_group(sc0, count):
        index_next(sc0, count)
        vt = jnp.concatenate(
            [jnp.concatenate([ckvt_ref[sc0 + c] for c in range(count)], axis=1),
             jnp.ones((ONES_ROWS, count * SC), BF16)], axis=0)
        for hp in range(N_HEADS // 2):
            cols = []
            for h in (2 * hp, 2 * hp + 1):
                hs = slice(h * BLK, (h + 1) * BLK)
                ps = [jnp.exp2(l_ref[sc0 + c, h] - m[:, hs]).astype(BF16) for c in range(count)]
                cols.append(jnp.concatenate(ps, axis=0))
            pair = jnp.concatenate(cols, axis=1)
            acc_ref[hp] = acc_ref[hp] + _dot(vt, pair)

    acc_ref[...] = jnp.zeros(acc_ref.shape, F32)

    def pv_body(t, carry):
        pv_group(3 * t, 3)
        return carry
    lax.fori_loop(0, nsc // 3, pv_body, 0)

    @pl.when(nsc % 3 == 1)
    def _():
        pv_group(nsc - 1, 1)

    @pl.when(nsc % 3 == 2)
    def _():
        pv_group(nsc - 2, 2)

    @pl.when((nsc_next > nsc) & (nsc_next <= n_sc))
    def _():
        index_next(nsc, 1)

    outs = []
    for h in range(N_HEADS):
        lanes = slice((h % 2) * BLK, (h % 2 + 1) * BLK)
        inv = 1.0 / acc_ref[h // 2, A_LATENT:A_LATENT + 1, lanes]
        o = (acc_ref[h // 2, 0:A_LATENT, lanes] * inv).astype(BF16)
        outs.append(_dot(wuvt_ref[h], o))
    out_ref[...] = jnp.concatenate(outs, axis=0).T.astype(BF16)


def _dsa_attn(qr, qir, wt, kidx, ckv, ckvt, wuvt, btd, batch, seq):
    nblk = seq // BLK
    n_sc = seq // SC
    k_sel = min(TOPK_MAX, seq // 4)
    hcols = N_HEADS * BLK
    tri = jnp.asarray(np.tril(np.ones((SC, SC), np.float32), -1), BF16)
    return pl.pallas_call(
        functools.partial(_dsa_kernel, k_sel=k_sel, n_sc=n_sc),
        grid=(batch, nblk),
        in_specs=[pl.BlockSpec((1, A_LATENT, hcols), lambda b, i: (b * nblk + i, 0, 0)),
                  pl.BlockSpec((1, IDX_DIM, IDX_HEADS * BLK),
                               lambda b, i: (b * nblk + jnp.minimum(i + 1, nblk - 1), 0, 0)),
                  pl.BlockSpec((IDX_HEADS, BLK),
                               lambda b, i: (0, b * nblk + jnp.minimum(i + 1, nblk - 1))),
                  pl.BlockSpec((seq, IDX_DIM), lambda b, i: (b, 0)),
                  pl.BlockSpec((seq, A_LATENT), lambda b, i: (b, 0)),
                  pl.BlockSpec((n_sc, A_LATENT, SC), lambda b, i: (b, 0, 0)),
                  _resident((N_HEADS, A_VHEAD, A_LATENT)),
                  _resident((3, N_HEADS, BLK, BLK)),
                  _resident((SC, SC))],
        out_specs=pl.BlockSpec((BLK, N_HEADS * A_VHEAD), lambda b, i: (b * nblk + i, 0)),
        out_shape=jax.ShapeDtypeStruct((batch * seq, N_HEADS * A_VHEAD), BF16),
        scratch_shapes=[
            pltpu.VMEM((n_sc + 1, SC, BLK), F32),
            pltpu.VMEM((N_HEADS // 2, A_LATENT + ONES_ROWS, 2 * BLK), F32),
            pltpu.VMEM((n_sc, N_HEADS, SC, BLK), F32),
        ],
        compiler_params=_params(2),
        name="dsa_attn",
    )(qr, qir, wt, kidx, ckv, ckvt, wuvt, btd, tri)


B_Q = N_HEADS * B_HEAD_DIM
B_KV = B_KV_HEADS * B_HEAD_DIM


def _b_proj_kernel(x_ref, g_ref, wt_ref, bcol_ref, wk_ref, bk_ref,
                   qr_ref, kk_ref, vt_ref, h_ref, ht_ref):
    tm = x_ref.shape[0]
    hf = _rms(x_ref[...], g_ref[...])
    h_ref[...] = hf.astype(BF16)
    ht_ref[...] = hf.T.astype(BF16)
    heads_per_dot = PROJ_ROWS // B_HEAD_DIM
    for c in range(B_Q // PROJ_ROWS):
        rows = slice(c * PROJ_ROWS, (c + 1) * PROJ_ROWS)
        r = (_dot(wt_ref[rows, :], ht_ref[...]) + bcol_ref[rows, :]).astype(BF16)
        for hh in range(heads_per_dot):
            h = c * heads_per_dot + hh
            for blk in range(tm // BLK):
                qr_ref[blk, :, h * BLK:(h + 1) * BLK] = (
                    r[hh * B_HEAD_DIM:(hh + 1) * B_HEAD_DIM, blk * BLK:(blk + 1) * BLK])
    vt_ref[...] = (_dot(wt_ref[B_Q:, :], ht_ref[...]) + bcol_ref[B_Q:, :]).astype(BF16)
    t = _dot(h_ref[...], wk_ref[...]) + bk_ref[...]
    for kv in range(B_KV_HEADS):
        kk_ref[kv] = t[:, kv * B_HEAD_DIM:(kv + 1) * B_HEAD_DIM].astype(BF16)


def _b_proj(x2, g, wt, bcol, wk, bk, tm=1024):
    n, d = x2.shape
    return pl.pallas_call(
        _b_proj_kernel,
        grid=(n // tm,),
        in_specs=[pl.BlockSpec((tm, d), lambda i: (i, 0)),
                  _resident((1, d)),
                  _resident((B_Q + B_KV, d)),
                  _resident((B_Q + B_KV, 1)),
                  _resident((d, B_KV)),
                  _resident((1, B_KV))],
        out_specs=(pl.BlockSpec((tm // BLK, B_HEAD_DIM, N_HEADS * BLK), lambda i: (i, 0, 0)),
                   pl.BlockSpec((B_KV_HEADS, tm, B_HEAD_DIM), lambda i: (0, i, 0)),
                   pl.BlockSpec((B_KV, tm), lambda i: (0, i))),
        out_shape=(jax.ShapeDtypeStruct((n // BLK, B_HEAD_DIM, N_HEADS * BLK), BF16),
                   jax.ShapeDtypeStruct((B_KV_HEADS, n, B_HEAD_DIM), BF16),
                   jax.ShapeDtypeStruct((B_KV, n), BF16)),
        scratch_shapes=[pltpu.VMEM((tm, d), BF16), pltpu.VMEM((d, tm), BF16)],
        compiler_params=_params(1),
        name="b_proj",
    )(x2, g, wt, bcol, wk, bk)


def _swa_kernel(sink_ref, qr_ref, kp_ref, kc_ref, vp_ref, vc_ref, bts_ref, out_ref):
    n = pl.program_id(1)
    c1 = (B_HEAD_DIM ** -0.5) * LOG2E
    group = N_HEADS // B_KV_HEADS
    blocks = (
        (qr_ref.at[0], kp_ref, kc_ref.at[:, 0:BLK, :], vp_ref, vc_ref.at[:, 0:BLK],
         jnp.where(n >= 1, 1, 2)),
        (qr_ref.at[1], kc_ref.at[:, 0:BLK, :], kc_ref.at[:, BLK:2 * BLK, :],
         vc_ref.at[:, 0:BLK], vc_ref.at[:, BLK:2 * BLK], 1))
    tasks = [(qb, kv, g2) for qb in range(2) for kv in range(B_KV_HEADS)
             for g2 in range(group // SWA_TASK_HEADS)]

    def scores(task):
        qb, kv, g2 = task
        q_ref, kp, kc = blocks[qb][0:3]
        h0 = kv * group + SWA_TASK_HEADS * g2
        q = q_ref[:, h0 * BLK:(h0 + SWA_TASK_HEADS) * BLK]
        return _dot(kp[kv], q), _dot(kc[kv], q)

    def attend(task, lp, lc):
        qb, kv, g2 = task
        vp, vc, prev_kind = blocks[qb][3:6]
        vrows = slice(kv * B_HEAD_DIM, (kv + 1) * B_HEAD_DIM)
        pps, pcs, invs = [], [], []
        for j in range(SWA_TASK_HEADS):
            h = kv * group + SWA_TASK_HEADS * g2 + j
            lanes = slice(j * BLK, (j + 1) * BLK)
            sink = sink_ref[:, h * BLK:(h + 1) * BLK] * LOG2E
            a = lp[:, lanes] * c1 + bts_ref[prev_kind, h]
            b = lc[:, lanes] * c1 + bts_ref[0, h]
            m = jnp.maximum(jnp.max(jnp.maximum(a, b), axis=0, keepdims=True), sink)
            pp = jnp.exp2(a - m)
            pc = jnp.exp2(b - m)
            den = jnp.sum(pp + pc, axis=0, keepdims=True) + jnp.exp2(sink - m)
            invs.append(1.0 / den)
            pps.append(pp.astype(BF16))
            pcs.append(pc.astype(BF16))
        ot = (_dot(vp[vrows, :], jnp.concatenate(pps, axis=1))
              + _dot(vc[vrows, :], jnp.concatenate(pcs, axis=1)))
        return [ot[:, j * BLK:(j + 1) * BLK] * invs[j] for j in range(SWA_TASK_HEADS)]

    outs = [[], []]
    ahead = SWA_AHEAD
    queue = [scores(task) for task in tasks[:ahead]]
    for t, task in enumerate(tasks):
        if t + ahead < len(tasks):
            queue.append(scores(tasks[t + ahead]))
        outs[task[0]] += attend(task, *queue.pop(0))
    for qb in range(2):
        out_ref[qb * BLK:(qb + 1) * BLK, :] = (
            jnp.concatenate(outs[qb], axis=0).T.astype(BF16))


def _swa_attn(qr, kk, vt, sinks_row, bts, batch, seq):
    nb2 = seq // (2 * BLK)
    cur = lambda b, n: b * nb2 + n
    prev = lambda b, n: 2 * (b * nb2 + n) - jnp.minimum(n, 1)
    return pl.pallas_call(
        _swa_kernel,
        grid=(batch, nb2),
        in_specs=[_resident((1, N_HEADS * BLK)),
                  pl.BlockSpec((2, B_HEAD_DIM, N_HEADS * BLK), lambda b, n: (cur(b, n), 0, 0)),
                  pl.BlockSpec((B_KV_HEADS, BLK, B_HEAD_DIM), lambda b, n: (0, prev(b, n), 0)),
                  pl.BlockSpec((B_KV_HEADS, 2 * BLK, B_HEAD_DIM), lambda b, n: (0, cur(b, n), 0)),
                  pl.BlockSpec((B_KV, BLK), lambda b, n: (0, prev(b, n))),
                  pl.BlockSpec((B_KV, 2 * BLK), lambda b, n: (0, cur(b, n))),
                  _resident((3, N_HEADS, BLK, BLK))],
        out_specs=pl.BlockSpec((2 * BLK, B_Q), lambda b, n: (cur(b, n), 0)),
        out_shape=jax.ShapeDtypeStruct((batch * seq, B_Q), BF16),
        compiler_params=_params(2),
        name="swa_attn",
    )(sinks_row, qr, kk, kk, vt, vt, bts)


FF_CHUNK = 256


def _ffn_kernel(x_ref, o_ref, wo_ref, bo_ref, g_ref, wgu_ref, wd_ref, fg_ref,
                out_ref, h_ref, act_ref, *, d_ff, final):
    x1 = x_ref[...] + _dot(o_ref[...], wo_ref[...]) + bo_ref[...]
    out_ref[...] = x1
    h_ref[...] = _rms(x1, g_ref[...]).astype(BF16)
    for c in range(d_ff // FF_CHUNK):
        gate = _dot(h_ref[...], wgu_ref[:, c * FF_CHUNK:(c + 1) * FF_CHUNK])
        up = _dot(h_ref[...], wgu_ref[:, d_ff + c * FF_CHUNK:d_ff + (c + 1) * FF_CHUNK])
        act_ref[:, c * FF_CHUNK:(c + 1) * FF_CHUNK] = (
            gate * jax.nn.sigmoid(gate) * up).astype(BF16)
    y = out_ref[...] + _dot(act_ref[...], wd_ref[...])
    if final:
        y = _rms(y, fg_ref[...])
    out_ref[...] = y


def _ffn(x2, o, wo, bo, g, wgu_all, wd_all, layer, fg, final, tm=1024):
    n, d = x2.shape
    d_ff = wd_all.shape[1]
    row = lambda i: (i, 0)
    pick = lambda i: (layer, 0, 0)
    return pl.pallas_call(
        functools.partial(_ffn_kernel, d_ff=d_ff, final=final),
        grid=(n // tm,),
        in_specs=[pl.BlockSpec((tm, d), row),
                  pl.BlockSpec((tm, o.shape[1]), row),
                  _resident(wo.shape),
                  _resident((1, d)),
                  _resident((1, d)),
                  pl.BlockSpec((None,) + wgu_all.shape[1:], pick, pipeline_mode=pl.Buffered(1)),
                  pl.BlockSpec((None,) + wd_all.shape[1:], pick, pipeline_mode=pl.Buffered(1)),
                  _resident((1, d))],
        out_specs=pl.BlockSpec((tm, d), row),
        out_shape=jax.ShapeDtypeStruct((n, d), F32),
        scratch_shapes=[pltpu.VMEM((tm, d), BF16),
                        pltpu.VMEM((tm, d_ff), BF16)],
        compiler_params=_params(1),
        name="ffn",
    )(x2, o, wo, bo, g, wgu_all, wd_all, fg)


def kernel(x, rel_bias, attn_norm, ffn_norm, final_norm, a_w_in, a_latent_norm, a_w_uv, a_w_out, b_w_in, b_b_in, b_sinks, b_w_out, b_b_out, ffn_w_gate_up, ffn_w_down):
    batch, seq, d = x.shape
    assert attn_norm.shape[0] == 2 and a_w_in.shape[0] == 1 and b_w_in.shape[0] == 1
    assert seq % SC == 0 and (batch * seq) % 1024 == 0 and d % BLK == 0
    assert a_w_in.shape[2] == A_IN and b_w_in.shape[2] == B_Q + 2 * B_KV
    assert rel_bias.shape == (REL_BUCKETS, N_HEADS)
    x2 = x.reshape(batch * seq, d)
    bts, btd = _bias_tiles(rel_bias)
    fg = final_norm.reshape(1, d)

    gu_shape, dn_shape = ffn_w_gate_up.shape, ffn_w_down.shape
    qr, qir, wto, kidx, ckv, ckvt, wgu_bf, wd_bf = _a_proj(
        x2, attn_norm[0].reshape(1, d), jnp.swapaxes(a_w_in, 1, 2),
        a_latent_norm[0].reshape(1, A_LATENT),
        ffn_w_gate_up.reshape(-1, gu_shape[-1]), ffn_w_down.reshape(-1, dn_shape[-1]))
    wgu_all, wd_all = wgu_bf.reshape(gu_shape), wd_bf.reshape(dn_shape)
    wuvt = jnp.swapaxes(a_w_uv[0], 1, 2).astype(BF16)
    o = _dsa_attn(qr, qir, wto, kidx, ckv, ckvt, wuvt, btd, batch, seq)
    x2 = _ffn(x2, o, a_w_out[0].astype(BF16), jnp.zeros((1, d), F32),
              ffn_norm[0].reshape(1, d), wgu_all, wd_all, 0, fg, final=False)

    wb, bb = b_w_in[0].astype(BF16), b_b_in[0]
    wqv = jnp.concatenate([wb[:, :B_Q], wb[:, B_Q + B_KV:]], axis=1).T
    bqv = jnp.concatenate([bb[:B_Q], bb[B_Q + B_KV:]]).reshape(-1, 1)
    qr, kk, vt = _b_proj(
        x2, attn_norm[1].reshape(1, d), wqv, bqv, wb[:, B_Q:B_Q + B_KV],
        bb[B_Q:B_Q + B_KV].reshape(1, B_KV))
    sinks_row = jnp.repeat(b_sinks[0], BLK).reshape(1, N_HEADS * BLK)
    o = _swa_attn(qr, kk, vt, sinks_row, bts, batch, seq)
    x2 = _ffn(x2, o, b_w_out[0].astype(BF16), b_b_out[0].reshape(1, d),
              ffn_norm[1].reshape(1, d), wgu_all, wd_all, 1, fg, final=True)
    return x2.reshape(batch, seq, d)
```
